```python
import jax
import jax.numpy as jnp
from jax import lax
import numpy as np

D_MODEL = 1024
BATCH = 16
SEQ = 2048
DEPTH = 1
DEC_BATCH = 32
DEC_SEQ = 4
PAST_LEN = 16384
PAGE_SIZE = 128

MIX_W = D_MODEL
HEAD_DIM = 64
RET_HEADS = (MIX_W // 2) // HEAD_DIM
RET_W = RET_HEADS * HEAD_DIM
MOBA_HEADS = (MIX_W - RET_W) // HEAD_DIM
MOBA_W = MOBA_HEADS * HEAD_DIM
RET_CHUNK = 128
MOBA_BLOCK = 256
MOBA_TOPK = 3
MOBA_Q_CHUNK = 16
D_FF = -(-8 * D_MODEL // (3 * 256)) * 256
ROPE_BASE = 10000.0
NORM_EPS = 1e-6
IN_COLS = 4 * RET_W + 3 * MOBA_W
SPLITS = (RET_W, 2 * RET_W, 3 * RET_W, 4 * RET_W, 4 * RET_W + MOBA_W, 4 * RET_W + 2 * MOBA_W)

kernel_name = 'hybrid_retention_moba_adaln_step'


def _rmsnorm(x, w):
    xf = x.astype(jnp.float32)
    y = xf * lax.rsqrt(jnp.mean(xf * xf, axis=-1, keepdims=True) + NORM_EPS)
    return (y * w.astype(jnp.float32)).astype(x.dtype)


def _modulation(c, w_mod, b_mod):
    m = jax.nn.silu(c) @ w_mod + b_mod
    return jnp.split(m[:, None, :], 6, axis=-1)


def _rotary(x, pos):
    half = x.shape[-1] // 2
    inv = ROPE_BASE ** (-jnp.arange(half, dtype=jnp.float32) / half)
    ang = pos.astype(jnp.float32)[:, None] * inv[None, :]
    cos = jnp.cos(ang)[None, :, None, :]
    sin = jnp.sin(ang)[None, :, None, :]
    x1 = x[..., :half].astype(jnp.float32)
    x2 = x[..., half:].astype(jnp.float32)
    return jnp.concatenate([x1 * cos - x2 * sin, x1 * sin + x2 * cos], axis=-1)


def _retention_scan(q, k, v, s0):
    B, H, L, _ = q.shape
    C = RET_CHUNK if L % RET_CHUNK == 0 else L
    n = L // C
    log_g = jnp.log1p(-jnp.exp2(-5.0 - jnp.arange(H, dtype=jnp.float32)))
    idx = jnp.arange(C, dtype=jnp.float32)
    diff = idx[:, None] - idx[None, :]
    dmat = jnp.where(diff >= 0, jnp.exp(jnp.maximum(diff, 0.0) * log_g[:, None, None]), 0.0)
    q_dec = jnp.exp((idx + 1.0) * log_g[:, None])[..., None]
    k_dec = jnp.exp((C - 1.0 - idx) * log_g[:, None])[..., None]
    c_dec = jnp.exp(C * log_g)[:, None, None]

    def blocks(t):
        return t.reshape(B, H, n, C, t.shape[-1]).transpose(2, 0, 1, 3, 4)

    def step(s, inp):
        qc, kc, vc = inp
        att = jnp.einsum('bhid,bhjd->bhij', qc, kc) * dmat
        o = jnp.einsum('bhij,bhjv->bhiv', att, vc) + jnp.einsum('bhid,bhdv->bhiv', qc, s) * q_dec
        s = s * c_dec + jnp.einsum('bhjd,bhjv->bhdv', kc * k_dec, vc)
        return s, o

    s, o = lax.scan(step, s0, (blocks(q), blocks(k), blocks(v)))
    return o.transpose(1, 2, 0, 3, 4).reshape(B, H, L, -1), s


def _retention_branch(rq, rk, rv, rg, pos, s0, gn_w):
    B, L = rq.shape[:2]
    q = _rotary(rq, pos).transpose(0, 2, 1, 3)
    k = (_rotary(rk, pos) * HEAD_DIM ** -0.5).transpose(0, 2, 1, 3)
    v = rv.astype(jnp.float32).transpose(0, 2, 1, 3)
    o, s = _retention_scan(q, k, v, s0.astype(jnp.float32))
    mu = jnp.mean(o, axis=-1, keepdims=True)
    var = jnp.mean(jnp.square(o - mu), axis=-1, keepdims=True)
    o = ((o - mu) * lax.rsqrt(var + NORM_EPS)).transpose(0, 2, 1, 3).reshape(B, L, RET_W)
    o = jax.nn.silu(rg.astype(jnp.float32)) * (o * gn_w.astype(jnp.float32))
    return o.astype(rq.dtype), s


def _moba_attend(q, k_sel, v_sel, sel_mask, k_own, v_own, own_mask):
    scale = q.shape[-1] ** -0.5
    f32 = jnp.float32
    l_sel = jnp.einsum('bhqd,bhqkd->bhqk', q, k_sel, preferred_element_type=f32) * scale
    l_own = jnp.einsum('bhqd,bhkd->bhqk', q, k_own, preferred_element_type=f32) * scale
    logits = jnp.concatenate([jnp.where(sel_mask, l_sel, -jnp.inf), jnp.where(own_mask, l_own, -jnp.inf)], axis=-1)
    p = jax.nn.softmax(logits, axis=-1)
    ns = k_sel.shape[3]
    out = (jnp.einsum('bhqk,bhqkd->bhqd', p[..., :ns], v_sel, preferred_element_type=f32)
           + jnp.einsum('bhqk,bhkd->bhqd', p[..., ns:], v_own, preferred_element_type=f32))
    return out.astype(q.dtype)


def _moba_prompt(q, k, v):
    B, S, H, d = q.shape
    nb = -(-S // MOBA_BLOCK)
    pad = nb * MOBA_BLOCK - S
    kh = jnp.pad(k, ((0, 0), (0, pad), (0, 0), (0, 0))).transpose(0, 2, 1, 3).reshape(B, H, nb, MOBA_BLOCK, d)
    vh = jnp.pad(v, ((0, 0), (0, pad), (0, 0), (0, 0))).transpose(0, 2, 1, 3).reshape(B, H, nb, MOBA_BLOCK, d)
    kmean = jnp.mean(kh, axis=3, dtype=jnp.float32)
    kk = min(MOBA_TOPK, nb)
    nq = S // MOBA_Q_CHUNK
    qs = q.reshape(B, nq, MOBA_Q_CHUNK, H, d).transpose(1, 0, 3, 2, 4)
    bi = jnp.arange(B)[:, None, None, None]
    hi = jnp.arange(H)[None, :, None, None]
    blk_ids = jnp.arange(nb)

    def one(args):
        ci, qc = args
        start = ci * MOBA_Q_CHUNK
        blk = start // MOBA_BLOCK
        qpos = start + jnp.arange(MOBA_Q_CHUNK)
        gate = jnp.einsum('bhqd,bhnd->bhqn', qc.astype(jnp.float32), kmean)
        gate = jnp.where(blk_ids < blk, gate, -jnp.inf)
        _, top_i = lax.top_k(gate, kk)
        sel_valid = top_i < blk
        k_sel = kh[bi, hi, top_i].reshape(B, H, MOBA_Q_CHUNK, kk * MOBA_BLOCK, d)
        v_sel = vh[bi, hi, top_i].reshape(B, H, MOBA_Q_CHUNK, kk * MOBA_BLOCK, d)
        sel_mask = jnp.repeat(sel_valid, MOBA_BLOCK, axis=-1)
        k_own = lax.dynamic_index_in_dim(kh, blk, axis=2, keepdims=False)
        v_own = lax.dynamic_index_in_dim(vh, blk, axis=2, keepdims=False)
        own_pos = blk * MOBA_BLOCK + jnp.arange(MOBA_BLOCK)
        own_mask = own_pos[None, :] <= qpos[:, None]
        return _moba_attend(qc, k_sel, v_sel, sel_mask, k_own, v_own, own_mask)

    out = lax.map(one, (jnp.arange(nq), qs))
    return out.transpose(1, 0, 3, 2, 4).reshape(B, S, H * d)


def _moba_sample(q, k_new, v_new, cache_k, cache_v, page_table):
    Bd, T, H, d = q.shape
    ppb = MOBA_BLOCK // PAGE_SIZE
    n_full = PAST_LEN // MOBA_BLOCK
    own_pages = (PAST_LEN % MOBA_BLOCK) // PAGE_SIZE
    qh = q.transpose(0, 2, 1, 3)
    if n_full > 0:
        kk = min(MOBA_TOPK, n_full)
        k_past = cache_k[page_table[:, :n_full * ppb]]
        kmean = jnp.mean(k_past.reshape(Bd, n_full, MOBA_BLOCK, H, d), axis=2, dtype=jnp.float32)
        gate = jnp.einsum('bhtd,bnhd->bhtn', qh.astype(jnp.float32), kmean)
        _, top_i = lax.top_k(gate, kk)
        logical = top_i[..., None] * ppb + jnp.arange(ppb)
        phys = page_table[jnp.arange(Bd)[:, None, None, None, None], logical]
        hi = jnp.arange(H)[None, :, None, None, None]
        k_sel = cache_k[phys, :, hi, :].reshape(Bd, H, T, kk * MOBA_BLOCK, d)
        v_sel = cache_v[phys, :, hi, :].reshape(Bd, H, T, kk * MOBA_BLOCK, d)
    else:
        k_sel = jnp.zeros((Bd, H, T, 0, d), q.dtype)
        v_sel = jnp.zeros((Bd, H, T, 0, d), q.dtype)
    sel_mask = jnp.ones((1, 1, 1, 1), dtype=bool)
    op = page_table[:, n_full * ppb:n_full * ppb + own_pages]
    n_own = own_pages * PAGE_SIZE
    k_own = jnp.concatenate([cache_k[op].reshape(Bd, n_own, H, d).astype(k_new.dtype), k_new], axis=1).transpose(0, 2, 1, 3)
    v_own = jnp.concatenate([cache_v[op].reshape(Bd, n_own, H, d).astype(v_new.dtype), v_new], axis=1).transpose(0, 2, 1, 3)
    kpos = jnp.arange(n_own + T)
    qpos = n_own + jnp.arange(T)
    own_mask = kpos[None, :] <= qpos[:, None]
    out = _moba_attend(qh, k_sel, v_sel, sel_mask, k_own, v_own, own_mask)
    return out.transpose(0, 2, 1, 3).reshape(Bd, T, H * d)


def _decoder_layer(x, c, pos, ret_s0, moba_mix, norm1_w, norm2_w, w_mod, b_mod, w_in,
                   ret_gn_w, moba_norm_w, w_out, w_gate, w_up, w_down):
    B, L, _ = x.shape
    sh1, sc1, g1, sh2, sc2, g2 = _modulation(c, w_mod, b_mod)
    h = _rmsnorm(x, norm1_w) * (1 + sc1) + sh1
    z = h @ w_in
    rq, rk, rv, rg, mq, mk, mv = jnp.split(z, SPLITS, axis=-1)

    def heads(t, n):
        return t.reshape(B, L, n, HEAD_DIM)

    ret_out, ret_s = _retention_branch(heads(rq, RET_HEADS), heads(rk, RET_HEADS), heads(rv, RET_HEADS),
                                       rg, pos, ret_s0, ret_gn_w)
    mk_h = heads(mk, MOBA_HEADS)
    mv_h = heads(mv, MOBA_HEADS)
    moba_out = moba_mix(heads(mq, MOBA_HEADS), mk_h, mv_h)
    moba_out = _rmsnorm(moba_out.reshape(B, L, MOBA_HEADS, HEAD_DIM),
                        moba_norm_w.reshape(MOBA_HEADS, HEAD_DIM)).reshape(B, L, MOBA_W)
    x = x + g1 * (jnp.concatenate([ret_out, moba_out], axis=-1) @ w_out)
    h2 = _rmsnorm(x, norm2_w) * (1 + sc2) + sh2
    x = x + g2 * ((jax.nn.silu(h2 @ w_gate) * (h2 @ w_up)) @ w_down)
    return x, mk_h, mv_h, ret_s


def setup_inputs(seed: int = 0) -> dict:
    key = jax.random.key(seed)
    ks = jax.random.split(key, 24)
    f32 = jnp.float32
    n_pages = PAST_LEN // PAGE_SIZE
    n_pool = (DEC_BATCH * n_pages * 5) // 4

    def nrm(k, shape, s):
        return jax.random.normal(k, shape, f32) * s

    page_table = jax.random.permutation(ks[6], n_pool)[:DEC_BATCH * n_pages].reshape(DEC_BATCH, n_pages).astype(jnp.int32)
    return {
        'x_prompt': nrm(ks[0], (BATCH, SEQ, D_MODEL), 1.0),
        'x_sample': nrm(ks[1], (DEC_BATCH, DEC_SEQ, D_MODEL), 1.0),
        'cache_k': nrm(ks[2], (DEPTH, n_pool, PAGE_SIZE, MOBA_HEADS, HEAD_DIM), 1.0),
        'cache_v': nrm(ks[3], (DEPTH, n_pool, PAGE_SIZE, MOBA_HEADS, HEAD_DIM), 1.0),
        'state_ret': nrm(ks[4], (DEPTH, DEC_BATCH, RET_HEADS, HEAD_DIM, HEAD_DIM), 0.3),
        'page_table': page_table,
        'c_prompt': nrm(ks[5], (BATCH, D_MODEL), 1.0),
        'c_sample': nrm(ks[7], (DEC_BATCH, D_MODEL), 1.0),
        'norm1_w': 1.0 + nrm(ks[8], (DEPTH, D_MODEL), 0.05),
        'norm2_w': 1.0 + nrm(ks[9], (DEPTH, D_MODEL), 0.05),
        'final_w': 1.0 + nrm(ks[10], (D_MODEL,), 0.05),
        'w_mod': nrm(ks[11], (DEPTH, D_MODEL, 6 * D_MODEL), 0.5 * D_MODEL ** -0.5),
        'b_mod': nrm(ks[12], (DEPTH, 6 * D_MODEL), 0.01),
        'w_in': nrm(ks[13], (DEPTH, D_MODEL, IN_COLS), D_MODEL ** -0.5),
        'ret_gn_w': 1.0 + nrm(ks[14], (DEPTH, RET_W), 0.05),
        'moba_norm_w': 1.0 + nrm(ks[15], (DEPTH, MOBA_W), 0.05),
        'w_out': nrm(ks[16], (DEPTH, MIX_W, D_MODEL), MIX_W ** -0.5),
        'w_gate': nrm(ks[17], (DEPTH, D_MODEL, D_FF), D_MODEL ** -0.5),
        'w_up': nrm(ks[18], (DEPTH, D_MODEL, D_FF), D_MODEL ** -0.5),
        'w_down': nrm(ks[19], (DEPTH, D_FF, D_MODEL), D_FF ** -0.5),
    }


def reference(x_prompt, x_sample, cache_k, cache_v, state_ret, page_table, c_prompt, c_sample,
              norm1_w, norm2_w, final_w, w_mod, b_mod, w_in, ret_gn_w, moba_norm_w, w_out,
              w_gate, w_up, w_down):
    Bp, S, _ = x_prompt.shape
    Bd, T, _ = x_sample.shape
    pos_p = jnp.arange(S)
    pos_s = PAST_LEN + jnp.arange(T)
    xp = x_prompt
    xs = x_sample
    kp_l, vp_l, rp_l, ks_l, vs_l, rs_l = [], [], [], [], [], []
    for l in range(DEPTH):
        lw = (norm1_w[l], norm2_w[l], w_mod[l], b_mod[l], w_in[l], ret_gn_w[l], moba_norm_w[l],
              w_out[l], w_gate[l], w_up[l], w_down[l])
        s0_p = jnp.zeros((Bp, RET_HEADS, HEAD_DIM, HEAD_DIM), jnp.float32)
        xp, kp, vp, rp = _decoder_layer(xp, c_prompt, pos_p, s0_p, _moba_prompt, *lw)
        ck = cache_k[l]
        cv = cache_v[l]
        moba_s = lambda q, k, v, ck=ck, cv=cv: _moba_sample(q, k, v, ck, cv, page_table)
        xs, ksm, vsm, rsm = _decoder_layer(xs, c_sample, pos_s, state_ret[l], moba_s, *lw)
        kp_l.append(kp); vp_l.append(vp); rp_l.append(rp)
        ks_l.append(ksm); vs_l.append(vsm); rs_l.append(rsm)
    y_prompt = _rmsnorm(xp, final_w)
    y_sample = _rmsnorm(xs, final_w)
    return (y_prompt, y_sample, jnp.stack(kp_l), jnp.stack(vp_l), jnp.stack(rp_l),
            jnp.stack(ks_l), jnp.stack(vs_l), jnp.stack(rs_l))
```

```python
import functools

import jax
import jax.numpy as jnp
from jax import lax
from jax.experimental import pallas as pl
from jax.experimental.pallas import tpu as pltpu

F32 = jnp.float32
BF16 = jnp.bfloat16
HIGHEST = lax.Precision.HIGHEST

HEAD_DIM = 64
RET_CHUNK = 128
MOBA_BLOCK = 256
MOBA_TOPK = 3
ROPE_BASE = 10000.0
NORM_EPS = 1e-6
LANES = 128
VMEM_LIMIT = 56 * 1024 * 1024
NEG_INF = float("-inf")


def _params(*sem):
    return pltpu.CompilerParams(dimension_semantics=sem, vmem_limit_bytes=VMEM_LIMIT)


def _rms(x, w):
    return x * lax.rsqrt(jnp.mean(x * x, axis=-1, keepdims=True) + NORM_EPS) * w


def _silu(x):
    return x * jax.nn.sigmoid(x)


def _mod_kernel(c_ref, w_ref, b_ref, o_ref):
    s = _silu(c_ref[...])
    o_ref[...] = jnp.dot(s, w_ref[...], preferred_element_type=F32, precision=HIGHEST) + b_ref[...]


def _modulation(c, w_mod, b_mod):
    n, d = c.shape
    cols = w_mod.shape[1]
    tn = 1536
    return pl.pallas_call(
        _mod_kernel,
        out_shape=jax.ShapeDtypeStruct((n, cols), F32),
        grid=(cols // tn,),
        in_specs=[pl.BlockSpec((n, d), lambda j: (0, 0)),
                  pl.BlockSpec((d, tn), lambda j: (0, j)),
                  pl.BlockSpec((1, tn), lambda j: (0, j))],
        out_specs=pl.BlockSpec((n, tn), lambda j: (0, j)),
        compiler_params=_params("arbitrary"),
        name="modulation",
    )(c, w_mod, b_mod.reshape(1, cols))


def _inproj_kernel(*refs, width, transposed_kv):
    if transposed_kv:
        (x_ref, sh_ref, sc_ref, nw_ref, cos_ref, sin_ref, w_ref, wkv_ref,
         rq_ref, rk_ref, rv_ref, rg_ref, mq_ref, mk_ref, mv_ref) = refs
    else:
        (x_ref, sh_ref, sc_ref, nw_ref, cos_ref, sin_ref, w_ref,
         rq_ref, rk_ref, rv_ref, rg_ref, mq_ref, mk_ref, mv_ref) = refs
    h = (_rms(x_ref[...], nw_ref[...]) * (1.0 + sc_ref[...]) + sh_ref[...]).astype(BF16)
    cos = cos_ref[...]
    sin = sin_ref[...]
    lane = lax.broadcasted_iota(jnp.int32, cos.shape, 1)
    first_half = (lane % HEAD_DIM) < (HEAD_DIM // 2)

    def proj(g):
        return jnp.dot(h, w_ref[:, g * width:(g + 1) * width], preferred_element_type=F32)

    def rope_store(z, o_ref, scale):
        for c in range(width // LANES):
            zc = z[:, c * LANES:(c + 1) * LANES]
            partner = jnp.where(first_half, pltpu.roll(zc, LANES - HEAD_DIM // 2, 1),
                                pltpu.roll(zc, HEAD_DIM // 2, 1))
            o_ref[:, c * LANES:(c + 1) * LANES] = ((zc * cos + partner * sin) * scale).astype(o_ref.dtype)

    rope_store(proj(0), rq_ref, 1.0)
    rope_store(proj(1), rk_ref, HEAD_DIM ** -0.5)
    rv_ref[...] = proj(2).astype(rv_ref.dtype)
    rg_ref[...] = _silu(proj(3)).astype(rg_ref.dtype)
    mq_ref[...] = (proj(4) * HEAD_DIM ** -0.5).astype(mq_ref.dtype)
    if transposed_kv:
        nt = (((1,), (1,)), ((), ()))
        mk_ref[...] = lax.dot_general(wkv_ref[0:width, :], h, nt, preferred_element_type=F32)
        mv_ref[...] = lax.dot_general(wkv_ref[width:2 * width, :], h, nt, preferred_element_type=F32)
    else:
        mk_ref[...] = proj(5)
        mv_ref[...] = proj(6)


def _mod_spec(arr, tm):
    if arr.shape[1] == 1:
        return pl.BlockSpec((None, 1, arr.shape[2]), lambda g, i: (g, 0, 0))
    return pl.BlockSpec((None, tm, arr.shape[2]), lambda g, i: (g, i, 0))


def _inproj(x, sh, sc, nw, cos_t, sin_t, w_bf, wkv_t, *, tm, act_dtype):
    G, R, D = x.shape
    width = w_bf.shape[1] // (5 if wkv_t is not None else 7)
    transposed_kv = wkv_t is not None
    row = pl.BlockSpec((None, tm, D), lambda g, i: (g, i, 0))
    const = lambda a: pl.BlockSpec(a.shape, lambda g, i: (0,) * a.ndim, pipeline_mode=pl.Buffered(1))
    tab = pl.BlockSpec((tm, LANES), lambda g, i: (i, 0))
    act = pl.BlockSpec((None, tm, width), lambda g, i: (g, i, 0))
    in_specs = [row, _mod_spec(sh, tm), _mod_spec(sc, tm), const(nw), tab, tab, const(w_bf)]
    args = [x, sh, sc, nw, cos_t, sin_t, w_bf]
    acts = [jax.ShapeDtypeStruct((G, R, width), act_dtype)] * 5
    if transposed_kv:
        in_specs.append(const(wkv_t))
        args.append(wkv_t)
        kv_shape = jax.ShapeDtypeStruct((G, width, R), F32)
        kv_spec = pl.BlockSpec((None, width, tm), lambda g, i: (g, 0, i))
    else:
        kv_shape = jax.ShapeDtypeStruct((G, R, width), F32)
        kv_spec = act
    return pl.pallas_call(
        functools.partial(_inproj_kernel, width=width, transposed_kv=transposed_kv),
        out_shape=acts + [kv_shape, kv_shape],
        grid=(G, R // tm),
        in_specs=in_specs,
        out_specs=[act] * 5 + [kv_spec, kv_spec],
        compiler_params=_params("parallel", "parallel"),
        name="inproj_t" if transposed_kv else "inproj",
    )(*args)


def _retention_kernel(q_ref, k_ref, v_ref, g_ref, s0_ref, dmat_ref, qdec_ref, kdec_ref, cdec_ref, gnw_ref,
                      o_ref, sout_ref, s_scr, *, heads, chunks):
    j = pl.program_id(1)

    @pl.when(j == 0)
    def _():
        s_scr[...] = s0_ref[...]

    qdec = qdec_ref[...]
    kdec = kdec_ref[...]
    gnw = gnw_ref[...]
    nt = (((1,), (1,)), ((), ()))
    tn = (((0,), (0,)), ((), ()))

    def chunk(c, carry):
        r0 = pl.multiple_of(c * RET_CHUNK, RET_CHUNK)
        rows = pl.ds(r0, RET_CHUNK)
        q = q_ref[rows, :].astype(BF16)
        k = k_ref[rows, :]
        v = v_ref[rows, :].astype(BF16)
        kd = (k.astype(F32) * kdec).astype(BF16)
        k = k.astype(BF16)
        outs = []
        for h in range(heads):
            sl = slice(h * HEAD_DIM, (h + 1) * HEAD_DIM)
            qh, kh, vh = q[:, sl], k[:, sl], v[:, sl]
            s = s_scr[h]
            att = lax.dot_general(qh, kh, nt, preferred_element_type=F32) * dmat_ref[h]
            o = (jnp.dot(att.astype(BF16), vh, preferred_element_type=F32)
                 + jnp.dot(qh, s.astype(BF16), preferred_element_type=F32) * qdec[:, sl])
            s_scr[h] = s * cdec_ref[h] + lax.dot_general(kd[:, sl], vh, tn, preferred_element_type=F32)
            mu = jnp.mean(o, axis=-1, keepdims=True)
            d = o - mu
            var = jnp.mean(d * d, axis=-1, keepdims=True)
            outs.append(d * lax.rsqrt(var + NORM_EPS))
        o_all = jnp.concatenate(outs, axis=1)
        o_ref[rows, :] = (g_ref[rows, :].astype(F32) * (o_all * gnw)).astype(o_ref.dtype)
        return carry

    lax.fori_loop(0, chunks, chunk, 0)

    @pl.when(j == pl.num_programs(1) - 1)
    def _():
        sout_ref[...] = s_scr[...]


def _retention(q, k, v, gs, s0, dmat, qdec, kdec, cdec, gnw, *, tl, out_dtype):
    B, L, W = q.shape
    heads = W // HEAD_DIM
    seq = pl.BlockSpec((None, tl, W), lambda b, j: (b, j, 0))
    st = pl.BlockSpec((None, heads, HEAD_DIM, HEAD_DIM), lambda b, j: (b, 0, 0, 0))
    const = lambda a: pl.BlockSpec(a.shape, lambda b, j: (0,) * a.ndim)
    return pl.pallas_call(
        functools.partial(_retention_kernel, heads=heads, chunks=tl // RET_CHUNK),
        out_shape=[jax.ShapeDtypeStruct((B, L, W), out_dtype),
                   jax.ShapeDtypeStruct((B, heads, HEAD_DIM, HEAD_DIM), F32)],
        grid=(B, L // tl),
        in_specs=[seq, seq, seq, seq, st, const(dmat), const(qdec), const(kdec), const(cdec), const(gnw)],
        out_specs=[seq, st],
        scratch_shapes=[pltpu.VMEM((heads, HEAD_DIM, HEAD_DIM), F32)],
        compiler_params=_params("parallel", "arbitrary"),
        name="retention",
    )(q, k, v, gs, s0, dmat, qdec, kdec, cdec, gnw)


def _decay_tables(heads, c_len, pad_len):
    log_g = jnp.log1p(-jnp.exp2(-5.0 - jnp.arange(heads, dtype=F32)))
    idx = jnp.arange(c_len, dtype=F32)
    diff = idx[:, None] - idx[None, :]
    dmat = jnp.where(diff >= 0, jnp.exp(jnp.maximum(diff, 0.0) * log_g[:, None, None]), 0.0)
    q_dec = jnp.exp((idx + 1.0) * log_g[:, None])
    k_dec = jnp.exp((c_len - 1.0 - idx) * log_g[:, None])
    c_dec = jnp.exp(c_len * log_g)
    pad = pad_len - c_len
    dmat = jnp.pad(dmat, ((0, 0), (0, pad), (0, pad)))
    wide = lambda t: jnp.pad(jnp.repeat(t.T, HEAD_DIM, axis=1), ((0, pad), (0, 0)))
    cdec = jnp.broadcast_to(c_dec[:, None, None], (heads, 1, HEAD_DIM))
    return dmat, wide(q_dec), wide(k_dec), cdec


def _moba_prompt_kernel(q_ref, kt_ref, vt_ref, nw_ref, o_ref, *, seq):
    nb = seq // MOBA_BLOCK
    heads = q_ref.shape[1] // HEAD_DIM
    nt = (((1,), (1,)), ((), ()))
    r = lax.broadcasted_iota(jnp.int32, (seq, LANES), 0)
    cidx = lax.broadcasted_iota(jnp.int32, (seq, LANES), 1)
    ind = jnp.where(r // MOBA_BLOCK == cidx, 1.0 / MOBA_BLOCK, 0.0).astype(F32)
    lane = lax.broadcasted_iota(jnp.int32, (MOBA_BLOCK, LANES), 1)
    row_i = lax.broadcasted_iota(jnp.int32, (MOBA_BLOCK, MOBA_BLOCK), 0)
    col_i = lax.broadcasted_iota(jnp.int32, (MOBA_BLOCK, MOBA_BLOCK), 1)
    causal = col_i <= row_i
    kmts = [jnp.dot(kt_ref[hh * HEAD_DIM:(hh + 1) * HEAD_DIM, :], ind, preferred_element_type=F32,
                    precision=HIGHEST) for hh in range(heads)] if nb > MOBA_TOPK + 1 else None

    for i in range(nb):
        rows = slice(i * MOBA_BLOCK, (i + 1) * MOBA_BLOCK)
        nk = (i + 1) * MOBA_BLOCK
        outs = []
        for hh in range(heads):
            hs = slice(hh * HEAD_DIM, (hh + 1) * HEAD_DIM)
            qi = q_ref[rows, hs]
            kt = kt_ref[hs, 0:nk]
            s = jnp.dot(qi.astype(BF16), kt.astype(BF16), preferred_element_type=F32)
            pieces = []
            if i > MOBA_TOPK:
                gate = jnp.dot(qi.astype(F32), kmts[hh], preferred_element_type=F32, precision=HIGHEST)
                valid = lane < i
                for n in range(i):
                    gn = gate[:, n:n + 1]
                    ahead = valid & ((gate > gn) | ((gate == gn) & (lane < n)))
                    rank = jnp.sum(ahead.astype(F32), axis=1, keepdims=True)
                    bias = jnp.where(rank < MOBA_TOPK, 0.0, NEG_INF)
                    pieces.append(s[:, n * MOBA_BLOCK:(n + 1) * MOBA_BLOCK] + bias)
            elif i > 0:
                pieces.append(s[:, 0:i * MOBA_BLOCK])
            pieces.append(jnp.where(causal, s[:, i * MOBA_BLOCK:nk], NEG_INF))
            s = jnp.concatenate(pieces, axis=1) if len(pieces) > 1 else pieces[0]
            m = jnp.max(s, axis=1, keepdims=True)
            p = jnp.exp(s - m)
            l = jnp.sum(p, axis=1, keepdims=True)
            o = lax.dot_general(p.astype(BF16), vt_ref[hs, 0:nk].astype(BF16), nt, preferred_element_type=F32) / l
            outs.append(_rms(o, nw_ref[:, hs]))
        o_ref[rows, :] = jnp.concatenate(outs, axis=1).astype(o_ref.dtype)


def _moba_prompt(q, kt, vt, nw, *, out_dtype):
    B, S, W = q.shape
    pair = 2 * HEAD_DIM
    return pl.pallas_call(
        functools.partial(_moba_prompt_kernel, seq=S),
        out_shape=jax.ShapeDtypeStruct((B, S, W), out_dtype),
        grid=(B, W // pair),
        in_specs=[pl.BlockSpec((None, S, pair), lambda b, h: (b, 0, h)),
                  pl.BlockSpec((None, pair, S), lambda b, h: (b, h, 0)),
                  pl.BlockSpec((None, pair, S), lambda b, h: (b, h, 0)),
                  pl.BlockSpec((1, pair), lambda b, h: (0, h))],
        out_specs=pl.BlockSpec((None, S, pair), lambda b, h: (b, 0, h)),
        compiler_params=_params("parallel", "parallel"),
        name="moba_prompt",
    )(q, kt, vt, nw)


PAGES_PER_STEP = 16


def _kmean_kernel(pt_ref, *refs):
    pages = refs[:PAGES_PER_STEP]
    o_ref = refs[PAGES_PER_STEP]
    g = pl.program_id(1)

    @pl.when(g == 0)
    def _():
        o_ref[...] = jnp.zeros_like(o_ref)

    lane = lax.broadcasted_iota(jnp.int32, o_ref.shape, 1)
    ppb = MOBA_BLOCK // pages[0].shape[1]
    acc = o_ref[...]
    for n in range(PAGES_PER_STEP // ppb):
        tot = pages[n * ppb][...]
        for p in range(1, ppb):
            tot = tot + pages[n * ppb + p][...]
        col = jnp.sum(tot, axis=1, keepdims=True) * (1.0 / MOBA_BLOCK)
        acc = jnp.where(lane == g * (PAGES_PER_STEP // ppb) + n, col, acc)
    o_ref[...] = acc


def _kmean(page_table, ckt, n_full):
    bd = page_table.shape[0]
    _, hd, page = ckt.shape
    ppb = MOBA_BLOCK // page
    steps = n_full * ppb // PAGES_PER_STEP

    def page_spec(p):
        return pl.BlockSpec((None, hd, page), lambda b, g, pt: (pt[b, g * PAGES_PER_STEP + p], 0, 0))

    return pl.pallas_call(
        _kmean_kernel,
        out_shape=jax.ShapeDtypeStruct((bd, hd, n_full), F32),
        grid_spec=pltpu.PrefetchScalarGridSpec(
            num_scalar_prefetch=1,
            grid=(bd, steps),
            in_specs=[page_spec(p) for p in range(PAGES_PER_STEP)],
            out_specs=pl.BlockSpec((None, hd, n_full), lambda b, g, pt: (b, 0, 0)),
        ),
        compiler_params=_params("parallel", "arbitrary"),
        name="moba_kmean",
    )(page_table, *([ckt] * PAGES_PER_STEP))


def _topk_kernel(q_ref, km_ref, o_ref, *, heads):
    n_blk = km_ref.shape[1]
    lane_b = lax.broadcasted_iota(jnp.int32, (q_ref.shape[0], n_blk), 1).astype(F32)
    lane_o = lax.broadcasted_iota(jnp.int32, o_ref.shape, 1)
    out = jnp.zeros(o_ref.shape, F32)
    for h in range(heads):
        hs = slice(h * HEAD_DIM, (h + 1) * HEAD_DIM)
        gate = jnp.dot(q_ref[:, hs], km_ref[hs, :], preferred_element_type=F32, precision=HIGHEST)
        for r in range(MOBA_TOPK):
            m = jnp.max(gate, axis=1, keepdims=True)
            idx = jnp.min(jnp.where(gate == m, lane_b, float(n_blk)), axis=1, keepdims=True)
            out = jnp.where(lane_o == h * MOBA_TOPK + r, idx, out)
            gate = jnp.where(lane_b == idx, NEG_INF, gate)
    o_ref[...] = out.astype(jnp.int32)


def _topk(q8, kmt):
    bd, rows, W = q8.shape
    n_full = kmt.shape[2]
    return pl.pallas_call(
        functools.partial(_topk_kernel, heads=W // HEAD_DIM),
        out_shape=jax.ShapeDtypeStruct((bd, rows, LANES), jnp.int32),
        grid=(bd,),
        in_specs=[pl.BlockSpec((None, rows, W), lambda b: (b, 0, 0)),
                  pl.BlockSpec((None, W, n_full), lambda b: (b, 0, 0))],
        out_specs=pl.BlockSpec((None, rows, LANES), lambda b: (b, 0, 0)),
        compiler_params=_params("parallel"),
        name="moba_topk",
    )(q8, kmt)


def _attend_kernel(top_ref, pt_ref, q_ref, knt_ref, vnt_ref, nw_ref, ck_ref, cv_ref, o_ref,
                   kbuf, vbuf, sem, *, heads, tokens, n_pages, page):
    b = pl.program_id(0)
    ppb = MOBA_BLOCK // page
    n_sel = MOBA_TOPK * MOBA_BLOCK
    nt = (((1,), (1,)), ((), ()))

    def copies(h, t):
        out = []
        for r in range(MOBA_TOPK):
            blk = top_ref[((b * tokens + t) * heads + h) * MOBA_TOPK + r]
            for p in range(ppb):
                phys = pt_ref[b * n_pages + blk * ppb + p]
                dst = pl.ds((r * ppb + p) * page, page)
                out.append(pltpu.make_async_copy(ck_ref.at[phys, h], kbuf.at[h, t, :, dst], sem.at[0]))
                out.append(pltpu.make_async_copy(cv_ref.at[phys, h], vbuf.at[h, t, :, dst], sem.at[1]))
        return out

    all_copies = [c for h in range(heads) for t in range(tokens) for c in copies(h, t)]
    for c in all_copies:
        c.start()
    for h in range(heads):
        for t in range(tokens):
            kbuf[h, t, :, n_sel:n_sel + LANES] = knt_ref[h]
            vbuf[h, t, :, n_sel:n_sel + LANES] = vnt_ref[h]
    for c in all_copies:
        c.wait()

    rows = q_ref.shape[0]
    col = lax.broadcasted_iota(jnp.int32, (rows, n_sel + LANES), 1)
    row = lax.broadcasted_iota(jnp.int32, (rows, HEAD_DIM), 0)
    outs = []
    for h in range(heads):
        hs = slice(h * HEAD_DIM, (h + 1) * HEAD_DIM)
        qh = q_ref[:, hs].astype(BF16)
        acc = jnp.zeros((rows, HEAD_DIM), F32)
        for t in range(tokens):
            s = jnp.dot(qh, kbuf[h, t].astype(BF16), preferred_element_type=F32)
            s = jnp.where(col <= n_sel + t, s, NEG_INF)
            m = jnp.max(s, axis=1, keepdims=True)
            p = jnp.exp(s - m)
            l = jnp.sum(p, axis=1, keepdims=True)
            o = lax.dot_general(p.astype(BF16), vbuf[h, t].astype(BF16), nt, preferred_element_type=F32) / l
            acc = jnp.where(row == t, o, acc)
        outs.append(_rms(acc, nw_ref[:, hs]))
    o_ref[...] = jnp.concatenate(outs, axis=1)


def _attend(top_flat, pt_flat, q8, knt, vnt, nw, ckt4, cvt4, *, tokens, n_pages):
    bd, rows, W = q8.shape
    heads = W // HEAD_DIM
    page = ckt4.shape[3]
    n_keys = MOBA_TOPK * MOBA_BLOCK + LANES
    any_spec = pl.BlockSpec(memory_space=pl.ANY)
    return pl.pallas_call(
        functools.partial(_attend_kernel, heads=heads, tokens=tokens, n_pages=n_pages, page=page),
        out_shape=jax.ShapeDtypeStruct((bd, rows, W), F32),
        grid_spec=pltpu.PrefetchScalarGridSpec(
            num_scalar_prefetch=2,
            grid=(bd,),
            in_specs=[pl.BlockSpec((None, rows, W), lambda b, *_: (b, 0, 0)),
                      pl.BlockSpec((None, heads, HEAD_DIM, LANES), lambda b, *_: (b, 0, 0, 0)),
                      pl.BlockSpec((None, heads, HEAD_DIM, LANES), lambda b, *_: (b, 0, 0, 0)),
                      pl.BlockSpec((1, W), lambda b, *_: (0, 0)),
                      any_spec, any_spec],
            out_specs=pl.BlockSpec((None, rows, W), lambda b, *_: (b, 0, 0)),
            scratch_shapes=[pltpu.VMEM((heads, tokens, HEAD_DIM, n_keys), F32),
                            pltpu.VMEM((heads, tokens, HEAD_DIM, n_keys), F32),
                            pltpu.SemaphoreType.DMA((2,))],
        ),
        compiler_params=_params("arbitrary"),
        name="moba_attend",
    )(top_flat, pt_flat, q8, knt, vnt, nw, ckt4, cvt4)


def _outffn_kernel(x_ref, ro_ref, mo_ref, g1_ref, sh2_ref, sc2_ref, g2_ref, n2w_ref, fw_ref,
                   wo_ref, wg_ref, wu_ref, wd_ref, y_ref, *, ff_chunks):
    rw = ro_ref.shape[1]
    attn = (jnp.dot(ro_ref[...].astype(BF16), wo_ref[0:rw, :], preferred_element_type=F32)
            + jnp.dot(mo_ref[...].astype(BF16), wo_ref[rw:, :], preferred_element_type=F32))
    x1 = x_ref[...] + g1_ref[...] * attn
    h2 = (_rms(x1, n2w_ref[...]) * (1.0 + sc2_ref[...]) + sh2_ref[...]).astype(BF16)
    acc = None
    for lo, hi in ff_chunks:
        gate = jnp.dot(h2, wg_ref[:, lo:hi], preferred_element_type=F32)
        up = jnp.dot(h2, wu_ref[:, lo:hi], preferred_element_type=F32)
        part = jnp.dot((_silu(gate) * up).astype(BF16), wd_ref[lo:hi, :], preferred_element_type=F32)
        acc = part if acc is None else acc + part
    x2 = x1 + g2_ref[...] * acc
    y_ref[...] = _rms(x2, fw_ref[...])


def _outffn(x, ro, mo, g1, sh2, sc2, g2, n2w, fw, wo, wg, wu, wd, *, tm):
    G, R, D = x.shape
    W = ro.shape[2]
    dff = wg.shape[1]
    half = (dff // 2) // 256 * 256
    ff_chunks = ((0, half), (half, dff))
    row = pl.BlockSpec((None, tm, D), lambda g, i: (g, i, 0))
    act = pl.BlockSpec((None, tm, W), lambda g, i: (g, i, 0))
    const = lambda a: pl.BlockSpec(a.shape, lambda g, i: (0,) * a.ndim, pipeline_mode=pl.Buffered(1))
    return pl.pallas_call(
        functools.partial(_outffn_kernel, ff_chunks=ff_chunks),
        out_shape=jax.ShapeDtypeStruct((G, R, D), F32),
        grid=(G, R // tm),
        in_specs=[row, act, act, _mod_spec(g1, tm), _mod_spec(sh2, tm), _mod_spec(sc2, tm), _mod_spec(g2, tm),
                  const(n2w), const(fw), const(wo), const(wg), const(wu), const(wd)],
        out_specs=row,
        compiler_params=_params("parallel", "parallel"),
        name="outproj_ffn",
    )(x, ro, mo, g1, sh2, sc2, g2, n2w, fw, wo, wg, wu, wd)


def _rope_tables(pos):
    half = HEAD_DIM // 2
    inv = ROPE_BASE ** (-jnp.arange(half, dtype=F32) / half)
    ang = pos.astype(F32)[:, None] * inv[None, :]
    cos = jnp.cos(ang)
    sin = jnp.sin(ang)
    reps = LANES // HEAD_DIM
    return jnp.tile(cos, (1, 2 * reps)), jnp.tile(jnp.concatenate([-sin, sin], axis=1), (1, reps))


def kernel(x_prompt, x_sample, cache_k, cache_v, state_ret, page_table, c_prompt, c_sample,
           norm1_w, norm2_w, final_w, w_mod, b_mod, w_in, ret_gn_w, moba_norm_w, w_out,
           w_gate, w_up, w_down):
    Bp, S, D = x_prompt.shape
    Bd, T, _ = x_sample.shape
    depth = w_in.shape[0]
    assert depth == 1, "single decoder layer"
    n_pool, page, m_heads = cache_k.shape[1], cache_k.shape[2], cache_k.shape[3]
    n_pages = page_table.shape[1]
    past_len = n_pages * page
    assert past_len % MOBA_BLOCK == 0 and MOBA_BLOCK % page == 0
    n_full = past_len // MOBA_BLOCK
    moba_w = m_heads * HEAD_DIM
    ret_w = (w_in.shape[2] - 3 * moba_w) // 4
    assert ret_w == moba_w, "the two head groups share one projection width"
    r_heads = ret_w // HEAD_DIM
    fw = final_w.reshape(1, D)
    rows_s = Bd * T
    pad8 = 8

    w_in0 = w_in[0].astype(BF16)
    w_main = w_in0[:, :4 * ret_w + moba_w]
    w_kv_t = w_in0[:, 4 * ret_w + moba_w:].T
    wo, wg, wu, wd = (w[0].astype(BF16) for w in (w_out, w_gate, w_up, w_down))
    n1w, n2w = norm1_w[0].reshape(1, D), norm2_w[0].reshape(1, D)
    gnw, mnw = ret_gn_w[0].reshape(1, ret_w), moba_norm_w[0].reshape(1, moba_w)

    mod = _modulation(jnp.concatenate([c_prompt, c_sample], axis=0), w_mod[0], b_mod[0])
    mod_p = [m[:, None, :] for m in jnp.split(mod[:Bp], 6, axis=-1)]
    mod_s = [jnp.repeat(m, T, axis=0)[None] for m in jnp.split(mod[Bp:], 6, axis=-1)]

    cos_p, sin_p = _rope_tables(jnp.arange(S))
    rq, rk, rv, rg, mq, mkt, mvt = _inproj(x_prompt, mod_p[0], mod_p[1], n1w, cos_p, sin_p, w_main, w_kv_t,
                                           tm=512, act_dtype=BF16)
    tabs_p = _decay_tables(r_heads, RET_CHUNK, RET_CHUNK)
    s0_p = jnp.zeros((Bp, r_heads, HEAD_DIM, HEAD_DIM), F32)
    ret_o, ret_p = _retention(rq, rk, rv, rg, s0_p, *tabs_p, gnw, tl=512, out_dtype=BF16)
    moba_o = _moba_prompt(mq, mkt, mvt, mnw, out_dtype=BF16)
    y_prompt = _outffn(x_prompt, ret_o, moba_o, mod_p[2], mod_p[3], mod_p[4], mod_p[5], n2w, fw,
                       wo, wg, wu, wd, tm=512)
    to_rows = lambda t: t.reshape(1, Bp, m_heads, HEAD_DIM, S).transpose(0, 1, 4, 2, 3)
    k_prompt, v_prompt = to_rows(mkt), to_rows(mvt)

    xs = x_sample.reshape(1, rows_s, D)
    cos_s, sin_s = _rope_tables(past_len + jnp.arange(rows_s) % T)
    sq, sk, sv, sg, smq, smk, smv = _inproj(xs, mod_s[0], mod_s[1], n1w, cos_s, sin_s, w_in0, None,
                                            tm=rows_s, act_dtype=F32)
    tabs_s = _decay_tables(r_heads, T, RET_CHUNK)
    padc = lambda t: jnp.pad(t.reshape(Bd, T, ret_w), ((0, 0), (0, RET_CHUNK - T), (0, 0)))
    ret_os, ret_s = _retention(padc(sq), padc(sk), padc(sv), padc(sg), state_ret[0], *tabs_s, gnw,
                               tl=RET_CHUNK, out_dtype=F32)
    ret_os = ret_os[:, :T].reshape(1, rows_s, ret_w)

    ckt4 = jnp.transpose(cache_k[0], (0, 2, 3, 1))
    cvt4 = jnp.transpose(cache_v[0], (0, 2, 3, 1))
    kmt = _kmean(page_table, ckt4.reshape(n_pool, moba_w, page), n_full)
    q8 = jnp.pad(smq.reshape(Bd, T, moba_w), ((0, 0), (0, pad8 - T), (0, 0)))
    top = _topk(q8, kmt)[:, :T, :m_heads * MOBA_TOPK]
    new_t = lambda t: jnp.pad(t.reshape(Bd, T, m_heads, HEAD_DIM).transpose(0, 2, 3, 1),
                              ((0, 0), (0, 0), (0, 0), (0, LANES - T)))
    moba_os = _attend(top.reshape(-1), page_table.reshape(-1), q8, new_t(smk), new_t(smv), mnw, ckt4, cvt4,
                      tokens=T, n_pages=n_pages)
    moba_os = moba_os[:, :T].reshape(1, rows_s, moba_w)
    y_sample = _outffn(xs, ret_os, moba_os, mod_s[2], mod_s[3], mod_s[4], mod_s[5], n2w, fw,
                       wo, wg, wu, wd, tm=rows_s).reshape(Bd, T, D)
    k_sample = smk.reshape(1, Bd, T, m_heads, HEAD_DIM)
    v_sample = smv.reshape(1, Bd, T, m_heads, HEAD_DIM)

    return (y_prompt, y_sample, k_prompt, v_prompt, ret_p[None], k_sample, v_sample, ret_s[None])
```

```python
import functools

import jax
import jax.numpy as jnp
from jax import lax
from jax.experimental import pallas as pl
from jax.experimental.pallas import tpu as pltpu

F32 = jnp.float32
BF16 = jnp.bfloat16
HIGHEST = lax.Precision.HIGHEST

HEAD_DIM = 64
HALF = HEAD_DIM // 2
RET_CHUNK = 128
MOBA_BLOCK = 256
MOBA_TOPK = 3
ROPE_BASE = 10000.0
NORM_EPS = 1e-6
LANES = 128
SUBLANES = 8
VMEM_LIMIT = 56 * 1024 * 1024
NEG_INF = float("-inf")
NT = (((1,), (1,)), ((), ()))
TN = (((0,), (0,)), ((), ()))


def _params(*sem):
    return pltpu.CompilerParams(dimension_semantics=sem, vmem_limit_bytes=VMEM_LIMIT)


def _rms(x, w):
    return x * lax.rsqrt(jnp.mean(x * x, axis=-1, keepdims=True) + NORM_EPS) * w


def _silu(x):
    return x * jax.nn.sigmoid(x)


def _const_spec(a):
    return pl.BlockSpec(a.shape, lambda *_: (0,) * a.ndim, pipeline_mode=pl.Buffered(1))


def _mod_kernel(c_ref, w_ref, b_ref, o_ref):
    s = _silu(c_ref[...])
    o_ref[...] = jnp.dot(s, w_ref[...], preferred_element_type=F32, precision=HIGHEST) + b_ref[...]


def _modulation(c, w_mod, b_mod):
    n, d = c.shape
    cols = w_mod.shape[1]
    tn = 1536
    return pl.pallas_call(
        _mod_kernel,
        out_shape=jax.ShapeDtypeStruct((n, cols), F32),
        grid=(cols // tn,),
        in_specs=[pl.BlockSpec((n, d), lambda j: (0, 0)),
                  pl.BlockSpec((d, tn), lambda j: (0, j)),
                  pl.BlockSpec((1, tn), lambda j: (0, j))],
        out_specs=pl.BlockSpec((n, tn), lambda j: (0, j)),
        compiler_params=_params("arbitrary"),
        name="modulation",
    )(c, w_mod, b_mod.reshape(1, cols))


def _mod_spec(arr, tm):
    if arr.shape[1] == 1:
        return pl.BlockSpec((None, 1, arr.shape[2]), lambda g, i: (g, 0, 0))
    return pl.BlockSpec((None, tm, arr.shape[2]), lambda g, i: (g, i, 0))


def _normed_input(x_ref, sh_ref, sc_ref, nw_ref):
    return (_rms(x_ref[...], nw_ref[...]) * (1.0 + sc_ref[...]) + sh_ref[...]).astype(BF16)


def _rope_store(z, cos, sin, o_ref, scale):
    lane = lax.broadcasted_iota(jnp.int32, cos.shape, 1)
    first_half = (lane % HEAD_DIM) < HALF
    for c in range(z.shape[1] // LANES):
        zc = z[:, c * LANES:(c + 1) * LANES]
        partner = jnp.where(first_half, pltpu.roll(zc, LANES - HALF, 1), pltpu.roll(zc, HALF, 1))
        o_ref[:, c * LANES:(c + 1) * LANES] = ((zc * cos + partner * sin) * scale).astype(o_ref.dtype)


def _inproj_sample_kernel(x_ref, sh_ref, sc_ref, nw_ref, cos_ref, sin_ref, w_ref,
                          rq_ref, rk_ref, rv_ref, rg_ref, mq_ref, mk_ref, mv_ref, *, width):
    h = _normed_input(x_ref, sh_ref, sc_ref, nw_ref)
    proj = lambda g: jnp.dot(h, w_ref[:, g * width:(g + 1) * width], preferred_element_type=F32)
    _rope_store(proj(0), cos_ref[...], sin_ref[...], rq_ref, 1.0)
    _rope_store(proj(1), cos_ref[...], sin_ref[...], rk_ref, HEAD_DIM ** -0.5)
    rv_ref[...] = proj(2)
    rg_ref[...] = _silu(proj(3))
    mq_ref[...] = proj(4) * HEAD_DIM ** -0.5
    mk_ref[...] = proj(5)
    mv_ref[...] = proj(6)


def _inproj_sample(x, sh, sc, nw, cos_t, sin_t, w_bf):
    G, R, D = x.shape
    width = w_bf.shape[1] // 7
    row = pl.BlockSpec((None, R, D), lambda g, i: (g, i, 0))
    tab = pl.BlockSpec((R, LANES), lambda g, i: (i, 0))
    act = pl.BlockSpec((None, R, width), lambda g, i: (g, i, 0))
    return pl.pallas_call(
        functools.partial(_inproj_sample_kernel, width=width),
        out_shape=[jax.ShapeDtypeStruct((G, R, width), F32)] * 7,
        grid=(G, 1),
        in_specs=[row, _mod_spec(sh, R), _mod_spec(sc, R), _const_spec(nw), tab, tab, _const_spec(w_bf)],
        out_specs=[act] * 7,
        compiler_params=_params("parallel", "parallel"),
        name="inproj_sample",
    )(x, sh, sc, nw, cos_t, sin_t, w_bf)


def _inproj_prompt_kernel(x_ref, sh_ref, sc_ref, nw_ref, cos_ref, sin_ref, cost_ref, sint_ref, w_ref, wt_ref,
                          rq_ref, rkt_ref, rv_ref, rg_ref, mqt_ref, mk_ref, mkt_ref, mvt_ref, *, width):
    h = _normed_input(x_ref, sh_ref, sc_ref, nw_ref)
    proj = lambda g: jnp.dot(h, w_ref[:, g * width:(g + 1) * width], preferred_element_type=F32)
    proj_t = lambda g: lax.dot_general(wt_ref[g * width:(g + 1) * width, :], h, NT, preferred_element_type=F32)
    _rope_store(proj(0), cos_ref[...], sin_ref[...], rq_ref, 1.0)
    rv_ref[...] = proj(1).astype(rv_ref.dtype)
    rg_ref[...] = _silu(proj(2)).astype(rg_ref.dtype)
    mk_ref[...] = proj(3).astype(mk_ref.dtype)
    zt = proj_t(0)
    cost = cost_ref[...]
    sint = sint_ref[...]
    scale = HEAD_DIM ** -0.5
    for hd in range(width // HEAD_DIM):
        lo = slice(hd * HEAD_DIM, hd * HEAD_DIM + HALF)
        hi = slice(hd * HEAD_DIM + HALF, (hd + 1) * HEAD_DIM)
        a, b = zt[lo], zt[hi]
        rkt_ref[lo, :] = ((a * cost - b * sint) * scale).astype(rkt_ref.dtype)
        rkt_ref[hi, :] = ((a * sint + b * cost) * scale).astype(rkt_ref.dtype)
    mqt_ref[...] = (proj_t(1) * scale).astype(mqt_ref.dtype)
    mkt_ref[...] = proj_t(2)
    mvt_ref[...] = proj_t(3)


def _inproj_prompt(x, sh, sc, nw, cos_t, sin_t, cos_tt, sin_tt, w_tok, w_trn, *, tm):
    G, R, D = x.shape
    width = w_tok.shape[1] // 4
    row = pl.BlockSpec((None, tm, D), lambda g, i: (g, i, 0))
    tab = pl.BlockSpec((tm, LANES), lambda g, i: (i, 0))
    tab_t = pl.BlockSpec((HALF, tm), lambda g, i: (0, i))
    act = pl.BlockSpec((None, tm, width), lambda g, i: (g, i, 0))
    act_t = pl.BlockSpec((None, width, tm), lambda g, i: (g, 0, i))
    tok = jax.ShapeDtypeStruct((G, R, width), BF16)
    trn = lambda dt: jax.ShapeDtypeStruct((G, width, R), dt)
    return pl.pallas_call(
        functools.partial(_inproj_prompt_kernel, width=width),
        out_shape=[tok, trn(BF16), tok, tok, trn(BF16), tok, trn(F32), trn(F32)],
        grid=(G, R // tm),
        in_specs=[row, _mod_spec(sh, tm), _mod_spec(sc, tm), _const_spec(nw), tab, tab, tab_t, tab_t,
                  _const_spec(w_tok), _const_spec(w_trn)],
        out_specs=[act, act_t, act, act, act_t, act, act_t, act_t],
        compiler_params=_params("parallel", "parallel"),
        name="inproj_prompt",
    )(x, sh, sc, nw, cos_t, sin_t, cos_tt, sin_tt, w_tok, w_trn)


def _retention_kernel(q_ref, kt_ref, v_ref, g_ref, s0_ref, dmat2_ref, qdec_ref, kdect_ref, cdec_ref, gnw_ref,
                      avg_ref, o_ref, sout_ref, s_scr, o_scr, *, heads, chunks):
    j = pl.program_id(1)
    pairs = heads // 2
    lane = lax.broadcasted_iota(jnp.int32, (LANES, LANES), 1)
    sub = lax.broadcasted_iota(jnp.int32, (LANES, LANES), 0)
    first = lane < HEAD_DIM
    diag = first == (sub < HEAD_DIM)
    keep_a = jnp.where(first, 1.0, 0.0).astype(BF16)
    keep_b = jnp.where(first, 0.0, 1.0).astype(BF16)

    @pl.when(j == 0)
    def _():
        z = jnp.zeros((HEAD_DIM, HEAD_DIM), F32)
        for p in range(pairs):
            s_scr[p] = jnp.concatenate([jnp.concatenate([s0_ref[2 * p], z], axis=1),
                                        jnp.concatenate([z, s0_ref[2 * p + 1]], axis=1)], axis=0)

    gnw = gnw_ref[...]
    avg = avg_ref[...]

    def group_mean(t):
        hi = t.astype(BF16)
        lo = (t - hi.astype(F32)).astype(BF16)
        return jnp.dot(hi, avg, preferred_element_type=F32) + jnp.dot(lo, avg, preferred_element_type=F32)

    for c in range(chunks):
        rows = slice(c * RET_CHUNK, (c + 1) * RET_CHUNK)
        for p in range(pairs):
            cs = slice(p * LANES, (p + 1) * LANES)
            qp = q_ref[rows, cs].astype(BF16)
            vp = v_ref[rows, cs].astype(BF16)
            ktp = kt_ref[cs, rows]
            q2 = jnp.concatenate([qp * keep_a, qp * keep_b], axis=0)
            att2 = jnp.dot(q2, ktp.astype(BF16), preferred_element_type=F32) * dmat2_ref[p]
            o2 = jnp.dot(att2.astype(BF16), vp, preferred_element_type=F32)
            s = s_scr[p]
            o = (jnp.where(first, o2[:RET_CHUNK], o2[RET_CHUNK:])
                 + jnp.dot(qp, s.astype(BF16), preferred_element_type=F32) * qdec_ref[:, cs])
            kdt = (ktp.astype(F32) * kdect_ref[cs, :]).astype(BF16)
            s_scr[p] = s * cdec_ref[p] + jnp.where(diag, jnp.dot(kdt, vp, preferred_element_type=F32), 0.0)
            o_scr[rows, cs] = o

    o_all = o_scr[...]
    d = o_all - group_mean(o_all)
    var = group_mean(d * d)
    o_ref[...] = (g_ref[...].astype(F32) * (d * lax.rsqrt(var + NORM_EPS) * gnw)).astype(o_ref.dtype)

    @pl.when(j == pl.num_programs(1) - 1)
    def _():
        for p in range(pairs):
            s = s_scr[p]
            sout_ref[2 * p] = s[:HEAD_DIM, :HEAD_DIM]
            sout_ref[2 * p + 1] = s[HEAD_DIM:, HEAD_DIM:]


def _retention(q, kt, v, gs, s0, tables, gnw, *, tl, out_dtype):
    B, L, W = q.shape
    heads = W // HEAD_DIM
    seq = pl.BlockSpec((None, tl, W), lambda b, j: (b, j, 0))
    seq_t = pl.BlockSpec((None, W, tl), lambda b, j: (b, 0, j))
    st = pl.BlockSpec((None, heads, HEAD_DIM, HEAD_DIM), lambda b, j: (b, 0, 0, 0))
    return pl.pallas_call(
        functools.partial(_retention_kernel, heads=heads, chunks=tl // RET_CHUNK),
        out_shape=[jax.ShapeDtypeStruct((B, L, W), out_dtype),
                   jax.ShapeDtypeStruct((B, heads, HEAD_DIM, HEAD_DIM), F32)],
        grid=(B, L // tl),
        in_specs=[seq, seq_t, seq, seq, st] + [_const_spec(t) for t in tables] + [_const_spec(gnw),
                                                                                   _const_spec(_head_average(W))],
        out_specs=[seq, st],
        scratch_shapes=[pltpu.VMEM((heads // 2, LANES, LANES), F32), pltpu.VMEM((tl, W), F32)],
        compiler_params=_params("parallel", "arbitrary"),
        name="retention",
    )(q, kt, v, gs, s0, *tables, gnw, _head_average(W))


def _head_average(width):
    r = jnp.arange(width) // HEAD_DIM
    return jnp.where(r[:, None] == r[None, :], 1.0 / HEAD_DIM, 0.0).astype(BF16)


def _decay_tables(heads, c_len, pad_len):
    log_g = jnp.log1p(-jnp.exp2(-5.0 - jnp.arange(heads, dtype=F32)))
    idx = jnp.arange(c_len, dtype=F32)
    diff = idx[:, None] - idx[None, :]
    dmat = jnp.where(diff >= 0, jnp.exp(jnp.maximum(diff, 0.0) * log_g[:, None, None]), 0.0)
    q_dec = jnp.exp((idx + 1.0) * log_g[:, None])
    k_dec = jnp.exp((c_len - 1.0 - idx) * log_g[:, None])
    c_dec = jnp.exp(c_len * log_g)
    pad = pad_len - c_len
    dmat2 = jnp.pad(dmat, ((0, 0), (0, pad), (0, pad))).reshape(heads // 2, 2 * pad_len, pad_len)
    qdec = jnp.pad(jnp.repeat(q_dec.T, HEAD_DIM, axis=1), ((0, pad), (0, 0)))
    kdect = jnp.pad(jnp.repeat(k_dec, HEAD_DIM, axis=0), ((0, 0), (0, pad)))
    cdec = jnp.broadcast_to(jnp.repeat(c_dec, HEAD_DIM).reshape(heads // 2, 2 * HEAD_DIM, 1),
                            (heads // 2, 2 * HEAD_DIM, 2 * HEAD_DIM))
    return dmat2, qdec, kdect, cdec


def _col_reduce(x, op, final):
    while x.shape[0] % (2 * SUBLANES) == 0:
        half = x.shape[0] // 2
        x = op(x[:half], x[half:])
    return final(x, axis=0, keepdims=True)


def _moba_prompt_kernel(qt_ref, k_ref, kt_ref, vt_ref, nwt_ref, o_ref, *, seq):
    nb = seq // MOBA_BLOCK
    nbp = -(-nb // SUBLANES) * SUBLANES
    qb = MOBA_BLOCK
    pair = 2 * HEAD_DIM
    if nb > MOBA_TOPK + 1:
        lane_n = lax.broadcasted_iota(jnp.int32, (pair, LANES), 1)
        kmt = jnp.zeros((pair, LANES), F32)
        for n in range(nb):
            tot = kt_ref[:, n * MOBA_BLOCK:n * MOBA_BLOCK + LANES]
            for c in range(1, MOBA_BLOCK // LANES):
                tot = tot + kt_ref[:, n * MOBA_BLOCK + c * LANES:n * MOBA_BLOCK + (c + 1) * LANES]
            kmt = jnp.where(lane_n == n, jnp.sum(tot, axis=1, keepdims=True) * (1.0 / MOBA_BLOCK), kmt)
        km = kmt.T[0:nbp]
        lane_k = lax.broadcasted_iota(jnp.int32, km.shape, 1)
        km_heads = [jnp.where(lane_k < HEAD_DIM, km, 0.0), jnp.where(lane_k < HEAD_DIM, 0.0, km)]
        blk = lax.broadcasted_iota(jnp.int32, (nbp, qb), 0)
    key_i = lax.broadcasted_iota(jnp.int32, (qb, qb), 0)
    qry_i = lax.broadcasted_iota(jnp.int32, (qb, qb), 1)
    causal = key_i <= qry_i
    zeros = jnp.zeros((HEAD_DIM, qb), BF16)
    vtb = vt_ref[...].astype(BF16)

    def scores(i, hh):
        qt = qt_ref[:, i * qb:(i + 1) * qb]
        hr = slice(hh * HEAD_DIM, (hh + 1) * HEAD_DIM)
        qm = jnp.concatenate([qt[hr], zeros] if hh == 0 else [zeros, qt[hr]], axis=0)
        return jnp.dot(k_ref[0:(i + 1) * MOBA_BLOCK, :], qm, preferred_element_type=F32)

    def softmax(i, hh, st):
        biases = [None] * i
        if i > MOBA_TOPK:
            qt = qt_ref[:, i * qb:(i + 1) * qb].astype(F32)
            gate = jnp.dot(km_heads[hh], qt, preferred_element_type=F32, precision=HIGHEST)
            valid = blk < i
            for n in range(i):
                gn = gate[n:n + 1, :]
                ahead = valid & ((gate > gn) | ((gate == gn) & (blk < n)))
                rank = jnp.sum(ahead.astype(F32), axis=0, keepdims=True)
                biases[n] = jnp.where(rank < MOBA_TOPK, 0.0, NEG_INF)
        past = [st[n * MOBA_BLOCK:(n + 1) * MOBA_BLOCK] for n in range(i)]
        own = jnp.where(causal, st[i * MOBA_BLOCK:(i + 1) * MOBA_BLOCK], NEG_INF)
        m = _col_reduce(own, jnp.maximum, jnp.max)
        for sb, bias in zip(past, biases):
            mb = _col_reduce(sb, jnp.maximum, jnp.max)
            m = jnp.maximum(m, mb if bias is None else mb + bias)
        p = jnp.exp(own - m)
        l = _col_reduce(p, jnp.add, jnp.sum)
        pieces = []
        for sb, bias in zip(past, biases):
            pb = jnp.exp(sb + (-m if bias is None else bias - m))
            l = l + _col_reduce(pb, jnp.add, jnp.sum)
            pieces.append(pb.astype(BF16))
        pieces.append(p.astype(BF16))
        return (jnp.concatenate(pieces, axis=0) if i > 0 else pieces[0]), l

    def output(i, hh, p, l):
        hr = slice(hh * HEAD_DIM, (hh + 1) * HEAD_DIM)
        ot = jnp.dot(vtb[hr, 0:(i + 1) * MOBA_BLOCK], p, preferred_element_type=F32) / l
        ms = jnp.mean(ot * ot, axis=0, keepdims=True)
        o_ref[hr, i * qb:(i + 1) * qb] = (ot * lax.rsqrt(ms + NORM_EPS) * nwt_ref[hr, :]).astype(o_ref.dtype)

    bodies = [(i, hh) for i in range(nb) for hh in range(2)]
    st_next = scores(*bodies[0])
    pending = None
    for idx, body in enumerate(bodies):
        st = st_next
        if idx + 1 < len(bodies):
            st_next = scores(*bodies[idx + 1])
        p, l = softmax(*body, st)
        if pending is not None:
            output(*pending)
        pending = (*body, p, l)
    output(*pending)


def _moba_prompt(qt, k, kt, vt, nwt, *, out_dtype):
    B, W, S = qt.shape
    pair = 2 * HEAD_DIM
    trn = pl.BlockSpec((None, pair, S), lambda b, h: (b, h, 0))
    return pl.pallas_call(
        functools.partial(_moba_prompt_kernel, seq=S),
        out_shape=jax.ShapeDtypeStruct((B, W, S), out_dtype),
        grid=(B, W // pair),
        in_specs=[trn, pl.BlockSpec((None, S, pair), lambda b, h: (b, 0, h)), trn, trn,
                  pl.BlockSpec((pair, MOBA_BLOCK), lambda b, h: (h, 0))],
        out_specs=trn,
        compiler_params=_params("parallel", "parallel"),
        name="moba_prompt",
    )(qt, k, kt, vt, nwt)


PAGES_PER_STEP = 32


def _kmean_kernel(pt_ref, *refs):
    pages = refs[:PAGES_PER_STEP]
    o_ref = refs[PAGES_PER_STEP]
    g = pl.program_id(1)

    @pl.when(g == 0)
    def _():
        o_ref[...] = jnp.zeros_like(o_ref)

    lane = lax.broadcasted_iota(jnp.int32, o_ref.shape, 1)
    ppb = MOBA_BLOCK // pages[0].shape[1]
    acc = o_ref[...]
    for n in range(PAGES_PER_STEP // ppb):
        tot = pages[n * ppb][...]
        for p in range(1, ppb):
            tot = tot + pages[n * ppb + p][...]
        col = jnp.sum(tot, axis=1, keepdims=True) * (1.0 / MOBA_BLOCK)
        acc = jnp.where(lane == g * (PAGES_PER_STEP // ppb) + n, col, acc)
    o_ref[...] = acc


def _kmean(page_table, ckt, n_full):
    bd = page_table.shape[0]
    _, hd, page = ckt.shape
    ppb = MOBA_BLOCK // page
    steps = n_full * ppb // PAGES_PER_STEP

    def page_spec(p):
        return pl.BlockSpec((None, hd, page), lambda b, g, pt: (pt[b, g * PAGES_PER_STEP + p], 0, 0))

    return pl.pallas_call(
        _kmean_kernel,
        out_shape=jax.ShapeDtypeStruct((bd, hd, n_full), F32),
        grid_spec=pltpu.PrefetchScalarGridSpec(
            num_scalar_prefetch=1,
            grid=(bd, steps),
            in_specs=[page_spec(p) for p in range(PAGES_PER_STEP)],
            out_specs=pl.BlockSpec((None, hd, n_full), lambda b, g, pt: (b, 0, 0)),
        ),
        compiler_params=_params("parallel", "arbitrary"),
        name="moba_kmean",
    )(page_table, *([ckt] * PAGES_PER_STEP))


def _topk_kernel(q_ref, km_ref, o_ref, *, heads):
    n_blk = km_ref.shape[1]
    lane_b = lax.broadcasted_iota(jnp.int32, (q_ref.shape[0], n_blk), 1).astype(F32)
    lane_o = lax.broadcasted_iota(jnp.int32, o_ref.shape, 1)
    out = jnp.zeros(o_ref.shape, F32)
    for h in range(heads):
        hs = slice(h * HEAD_DIM, (h + 1) * HEAD_DIM)
        gate = jnp.dot(q_ref[:, hs], km_ref[hs, :], preferred_element_type=F32, precision=HIGHEST)
        for r in range(MOBA_TOPK):
            m = jnp.max(gate, axis=1, keepdims=True)
            idx = jnp.min(jnp.where(gate == m, lane_b, float(n_blk)), axis=1, keepdims=True)
            out = jnp.where(lane_o == h * MOBA_TOPK + r, idx, out)
            gate = jnp.where(lane_b == idx, NEG_INF, gate)
    o_ref[...] = out.astype(jnp.int32)


def _topk(q8, kmt):
    bd, rows, W = q8.shape
    n_full = kmt.shape[2]
    return pl.pallas_call(
        functools.partial(_topk_kernel, heads=W // HEAD_DIM),
        out_shape=jax.ShapeDtypeStruct((bd, rows, LANES), jnp.int32),
        grid=(bd,),
        in_specs=[pl.BlockSpec((None, rows, W), lambda b: (b, 0, 0)),
                  pl.BlockSpec((None, W, n_full), lambda b: (b, 0, 0))],
        out_specs=pl.BlockSpec((None, rows, LANES), lambda b: (b, 0, 0)),
        compiler_params=_params("parallel"),
        name="moba_topk",
    )(q8, kmt)


def _attend_kernel(top_ref, pt_ref, q_ref, knt_ref, vnt_ref, nw_ref, ck_ref, cv_ref, o_ref,
                   kbuf, vbuf, sem, *, heads, tokens, n_pages, page):
    b = pl.program_id(0)
    nb = pl.num_programs(0)
    ppb = MOBA_BLOCK // page
    n_sel = MOBA_TOPK * MOBA_BLOCK
    slot = b % 2

    def page_copies(seq, buf_slot):
        out = []
        for h in range(heads):
            for t in range(tokens):
                for r in range(MOBA_TOPK):
                    blk = top_ref[((seq * tokens + t) * heads + h) * MOBA_TOPK + r]
                    for p in range(ppb):
                        phys = pt_ref[seq * n_pages + blk * ppb + p]
                        dst = pl.ds((r * ppb + p) * page, page)
                        out.append(pltpu.make_async_copy(ck_ref.at[phys, h], kbuf.at[buf_slot, h, t, :, dst],
                                                         sem.at[buf_slot, 0]))
                        out.append(pltpu.make_async_copy(cv_ref.at[phys, h], vbuf.at[buf_slot, h, t, :, dst],
                                                         sem.at[buf_slot, 1]))
        return out

    @pl.when(b == 0)
    def _():
        for c in page_copies(b, slot):
            c.start()

    @pl.when(b + 1 < nb)
    def _():
        for c in page_copies(b + 1, 1 - slot):
            c.start()

    for c in page_copies(b, slot):
        c.wait()
    kb = kbuf.at[slot]
    vb = vbuf.at[slot]
    for h in range(heads):
        for t in range(tokens):
            kb[h, t, :, n_sel:n_sel + LANES] = knt_ref[h]
            vb[h, t, :, n_sel:n_sel + LANES] = vnt_ref[h]

    rows = q_ref.shape[0]
    col = lax.broadcasted_iota(jnp.int32, (rows, n_sel + LANES), 1)
    row = lax.broadcasted_iota(jnp.int32, (rows, HEAD_DIM), 0)
    outs = []
    for h in range(heads):
        hs = slice(h * HEAD_DIM, (h + 1) * HEAD_DIM)
        qh = q_ref[:, hs].astype(BF16)
        acc = jnp.zeros((rows, HEAD_DIM), F32)
        for t in range(tokens):
            s = jnp.dot(qh, kb[h, t].astype(BF16), preferred_element_type=F32)
            s = jnp.where(col <= n_sel + t, s, NEG_INF)
            m = jnp.max(s, axis=1, keepdims=True)
            p = jnp.exp(s - m)
            l = jnp.sum(p, axis=1, keepdims=True)
            o = lax.dot_general(p.astype(BF16), vb[h, t].astype(BF16), NT, preferred_element_type=F32) / l
            acc = jnp.where(row == t, o, acc)
        outs.append(_rms(acc, nw_ref[:, hs]))
    o_ref[...] = jnp.concatenate(outs, axis=1)


def _attend(top_flat, pt_flat, q8, knt, vnt, nw, ckt4, cvt4, *, tokens, n_pages):
    bd, rows, W = q8.shape
    heads = W // HEAD_DIM
    page = ckt4.shape[3]
    n_keys = MOBA_TOPK * MOBA_BLOCK + LANES
    any_spec = pl.BlockSpec(memory_space=pl.ANY)
    return pl.pallas_call(
        functools.partial(_attend_kernel, heads=heads, tokens=tokens, n_pages=n_pages, page=page),
        out_shape=jax.ShapeDtypeStruct((bd, rows, W), F32),
        grid_spec=pltpu.PrefetchScalarGridSpec(
            num_scalar_prefetch=2,
            grid=(bd,),
            in_specs=[pl.BlockSpec((None, rows, W), lambda b, *_: (b, 0, 0)),
                      pl.BlockSpec((None, heads, HEAD_DIM, LANES), lambda b, *_: (b, 0, 0, 0)),
                      pl.BlockSpec((None, heads, HEAD_DIM, LANES), lambda b, *_: (b, 0, 0, 0)),
                      pl.BlockSpec((1, W), lambda b, *_: (0, 0)),
                      any_spec, any_spec],
            out_specs=pl.BlockSpec((None, rows, W), lambda b, *_: (b, 0, 0)),
            scratch_shapes=[pltpu.VMEM((2, heads, tokens, HEAD_DIM, n_keys), F32),
                            pltpu.VMEM((2, heads, tokens, HEAD_DIM, n_keys), F32),
                            pltpu.SemaphoreType.DMA((2, 2))],
        ),
        compiler_params=_params("arbitrary"),
        name="moba_attend",
    )(top_flat, pt_flat, q8, knt, vnt, nw, ckt4, cvt4)


def _outffn_kernel(x_ref, ro_ref, mo_ref, g1_ref, sh2_ref, sc2_ref, g2_ref, n2w_ref, fw_ref,
                   wo_ref, wg_ref, wu_ref, wd_ref, y_ref, *, ff_chunks, moba_transposed):
    rw = ro_ref.shape[1]
    mo = mo_ref[...].astype(BF16)
    attn = (jnp.dot(ro_ref[...].astype(BF16), wo_ref[0:rw, :], preferred_element_type=F32)
            + lax.dot_general(mo, wo_ref[rw:, :], TN if moba_transposed else (((1,), (0,)), ((), ())),
                              preferred_element_type=F32))
    x1 = x_ref[...] + g1_ref[...] * attn
    h2 = (_rms(x1, n2w_ref[...]) * (1.0 + sc2_ref[...]) + sh2_ref[...]).astype(BF16)
    acc = None
    for lo, hi in ff_chunks:
        gate = jnp.dot(h2, wg_ref[:, lo:hi], preferred_element_type=F32)
        up = jnp.dot(h2, wu_ref[:, lo:hi], preferred_element_type=F32)
        part = jnp.dot((_silu(gate) * up).astype(BF16), wd_ref[lo:hi, :], preferred_element_type=F32)
        acc = part if acc is None else acc + part
    x2 = x1 + g2_ref[...] * acc
    y_ref[...] = _rms(x2, fw_ref[...])


def _outffn(x, ro, mo, g1, sh2, sc2, g2, n2w, fw, wo, wg, wu, wd, *, tm, moba_transposed):
    G, R, D = x.shape
    W = ro.shape[2]
    dff = wg.shape[1]
    half = (dff // 2) // 256 * 256
    ff_chunks = ((0, half), (half, dff))
    row = pl.BlockSpec((None, tm, D), lambda g, i: (g, i, 0))
    act = pl.BlockSpec((None, tm, W), lambda g, i: (g, i, 0))
    act_t = pl.BlockSpec((None, W, tm), lambda g, i: (g, 0, i))
    return pl.pallas_call(
        functools.partial(_outffn_kernel, ff_chunks=ff_chunks, moba_transposed=moba_transposed),
        out_shape=jax.ShapeDtypeStruct((G, R, D), F32),
        grid=(G, R // tm),
        in_specs=[row, act, act_t if moba_transposed else act,
                  _mod_spec(g1, tm), _mod_spec(sh2, tm), _mod_spec(sc2, tm), _mod_spec(g2, tm),
                  _const_spec(n2w), _const_spec(fw), _const_spec(wo), _const_spec(wg), _const_spec(wu),
                  _const_spec(wd)],
        out_specs=row,
        compiler_params=_params("parallel", "parallel"),
        name="outproj_ffn",
    )(x, ro, mo, g1, sh2, sc2, g2, n2w, fw, wo, wg, wu, wd)


def _rope_angles(pos):
    inv = ROPE_BASE ** (-jnp.arange(HALF, dtype=F32) / HALF)
    ang = pos.astype(F32)[:, None] * inv[None, :]
    return jnp.cos(ang), jnp.sin(ang)


def _rope_tables(pos):
    cos, sin = _rope_angles(pos)
    reps = LANES // HEAD_DIM
    return jnp.tile(cos, (1, 2 * reps)), jnp.tile(jnp.concatenate([-sin, sin], axis=1), (1, reps))


def kernel(x_prompt, x_sample, cache_k, cache_v, state_ret, page_table, c_prompt, c_sample,
           norm1_w, norm2_w, final_w, w_mod, b_mod, w_in, ret_gn_w, moba_norm_w, w_out,
           w_gate, w_up, w_down):
    Bp, S, D = x_prompt.shape
    Bd, T, _ = x_sample.shape
    depth = w_in.shape[0]
    assert depth == 1, "single decoder layer"
    n_pool, page, m_heads = cache_k.shape[1], cache_k.shape[2], cache_k.shape[3]
    n_pages = page_table.shape[1]
    past_len = n_pages * page
    assert past_len % MOBA_BLOCK == 0 and MOBA_BLOCK % page == 0
    n_full = past_len // MOBA_BLOCK
    moba_w = m_heads * HEAD_DIM
    ret_w = (w_in.shape[2] - 3 * moba_w) // 4
    assert ret_w == moba_w, "the two head groups share one projection width"
    W = ret_w
    r_heads = ret_w // HEAD_DIM
    fw = final_w.reshape(1, D)
    rows_s = Bd * T
    pad8 = SUBLANES

    w_in0 = w_in[0].astype(BF16)
    col = lambda g: w_in0[:, g * W:(g + 1) * W]
    w_tok = jnp.concatenate([col(0), col(2), col(3), col(5)], axis=1)
    w_trn = jnp.concatenate([col(1), col(4), col(5), col(6)], axis=1).T
    wo, wg, wu, wd = (w[0].astype(BF16) for w in (w_out, w_gate, w_up, w_down))
    n1w, n2w = norm1_w[0].reshape(1, D), norm2_w[0].reshape(1, D)
    gnw, mnw = ret_gn_w[0].reshape(1, ret_w), moba_norm_w[0].reshape(1, moba_w)

    mod = _modulation(jnp.concatenate([c_prompt, c_sample], axis=0), w_mod[0], b_mod[0])
    mod_p = [m[:, None, :] for m in jnp.split(mod[:Bp], 6, axis=-1)]
    mod_s = [jnp.repeat(m, T, axis=0)[None] for m in jnp.split(mod[Bp:], 6, axis=-1)]

    cos_a, sin_a = _rope_angles(jnp.arange(S))
    cos_p, sin_p = _rope_tables(jnp.arange(S))
    rq, rkt, rv, rg, mqt, mk, mkt, mvt = _inproj_prompt(x_prompt, mod_p[0], mod_p[1], n1w, cos_p, sin_p,
                                                        cos_a.T, sin_a.T, w_tok, w_trn, tm=512)
    s0_p = jnp.zeros((Bp, r_heads, HEAD_DIM, HEAD_DIM), F32)
    ret_o, ret_p = _retention(rq, rkt, rv, rg, s0_p, _decay_tables(r_heads, RET_CHUNK, RET_CHUNK), gnw,
                              tl=512, out_dtype=BF16)
    nwt = jnp.broadcast_to(mnw.reshape(moba_w, 1), (moba_w, MOBA_BLOCK))
    moba_ot = _moba_prompt(mqt, mk, mkt, mvt, nwt, out_dtype=BF16)
    y_prompt = _outffn(x_prompt, ret_o, moba_ot, mod_p[2], mod_p[3], mod_p[4], mod_p[5], n2w, fw,
                       wo, wg, wu, wd, tm=512, moba_transposed=True)
    to_rows = lambda t: t.reshape(1, Bp, m_heads, HEAD_DIM, S).transpose(0, 1, 4, 2, 3)
    k_prompt, v_prompt = to_rows(mkt), to_rows(mvt)

    xs = x_sample.reshape(1, rows_s, D)
    cos_s, sin_s = _rope_tables(past_len + jnp.arange(rows_s) % T)
    sq, sk, sv, sg, smq, smk, smv = _inproj_sample(xs, mod_s[0], mod_s[1], n1w, cos_s, sin_s, w_in0)
    padc = lambda t: jnp.pad(t.reshape(Bd, T, ret_w), ((0, 0), (0, RET_CHUNK - T), (0, 0)))
    skt = jnp.pad(sk.reshape(Bd, T, ret_w).transpose(0, 2, 1), ((0, 0), (0, 0), (0, RET_CHUNK - T)))
    ret_os, ret_s = _retention(padc(sq), skt, padc(sv), padc(sg), state_ret[0],
                               _decay_tables(r_heads, T, RET_CHUNK), gnw, tl=RET_CHUNK, out_dtype=F32)
    ret_os = ret_os[:, :T].reshape(1, rows_s, ret_w)

    ckt4 = jnp.transpose(cache_k[0], (0, 2, 3, 1))
    cvt4 = jnp.transpose(cache_v[0], (0, 2, 3, 1))
    kmt = _kmean(page_table, ckt4.reshape(n_pool, moba_w, page), n_full)
    q8 = jnp.pad(smq.reshape(Bd, T, moba_w), ((0, 0), (0, pad8 - T), (0, 0)))
    top = _topk(q8, kmt)[:, :T, :m_heads * MOBA_TOPK]
    new_t = lambda t: jnp.pad(t.reshape(Bd, T, m_heads, HEAD_DIM).transpose(0, 2, 3, 1),
                              ((0, 0), (0, 0), (0, 0), (0, LANES - T)))
    moba_os = _attend(top.reshape(-1), page_table.reshape(-1), q8, new_t(smk), new_t(smv), mnw, ckt4, cvt4,
                      tokens=T, n_pages=n_pages)
    moba_os = moba_os[:, :T].reshape(1, rows_s, moba_w)
    y_sample = _outffn(xs, ret_os, moba_os, mod_s[2], mod_s[3], mod_s[4], mod_s[5], n2w, fw,
                       wo, wg, wu, wd, tm=rows_s, moba_transposed=False).reshape(Bd, T, D)
    k_sample = smk.reshape(1, Bd, T, m_heads, HEAD_DIM)
    v_sample = smv.reshape(1, Bd, T, m_heads, HEAD_DIM)

    return (y_prompt, y_sample, k_prompt, v_prompt, ret_p[None], k_sample, v_sample, ret_s[None])
```

```python
import functools
import math

import jax
import jax.numpy as jnp
from jax import lax
from jax.experimental import pallas as pl
from jax.experimental.pallas import tpu as pltpu

F32 = jnp.float32
BF16 = jnp.bfloat16
HIGHEST = lax.Precision.HIGHEST

HEAD_DIM = 64
HALF = HEAD_DIM // 2
RET_CHUNK = 128
MOBA_BLOCK = 256
MOBA_TOPK = 3
ROPE_BASE = 10000.0
NORM_EPS = 1e-6
LANES = 128
SUBLANES = 8
VMEM_LIMIT = 56 * 1024 * 1024
NEG_INF = float("-inf")
MOBA_QSCALE = HEAD_DIM ** -0.5 * math.log2(math.e)
NT = (((1,), (1,)), ((), ()))
TN = (((0,), (0,)), ((), ()))


def _params(*sem):
    return pltpu.CompilerParams(dimension_semantics=sem, vmem_limit_bytes=VMEM_LIMIT)


def _rms(x, w):
    return x * lax.rsqrt(jnp.mean(x * x, axis=-1, keepdims=True) + NORM_EPS) * w


def _silu(x):
    return x * jax.nn.sigmoid(x)


def _const_spec(a):
    return pl.BlockSpec(a.shape, lambda *_: (0,) * a.ndim, pipeline_mode=pl.Buffered(1))


def _mod_kernel(c_ref, w_ref, b_ref, o_ref):
    s = _silu(c_ref[...])
    o_ref[...] = jnp.dot(s, w_ref[...], preferred_element_type=F32, precision=HIGHEST) + b_ref[...]


def _modulation(c, w_mod, b_mod):
    n, d = c.shape
    cols = w_mod.shape[1]
    tn = 1536
    return pl.pallas_call(
        _mod_kernel,
        out_shape=jax.ShapeDtypeStruct((n, cols), F32),
        grid=(cols // tn,),
        in_specs=[pl.BlockSpec((n, d), lambda j: (0, 0)),
                  pl.BlockSpec((d, tn), lambda j: (0, j)),
                  pl.BlockSpec((1, tn), lambda j: (0, j))],
        out_specs=pl.BlockSpec((n, tn), lambda j: (0, j)),
        compiler_params=_params("arbitrary"),
        name="modulation",
    )(c, w_mod, b_mod.reshape(1, cols))


def _mod_spec(arr, tm):
    if arr.shape[1] == 1:
        return pl.BlockSpec((None, 1, arr.shape[2]), lambda g, i, *_: (g, 0, 0))
    return pl.BlockSpec((None, tm, arr.shape[2]), lambda g, i, *_: (g, i, 0))


def _normed_input(x_ref, sh_ref, sc_ref, nw_ref):
    return (_rms(x_ref[...], nw_ref[...]) * (1.0 + sc_ref[...]) + sh_ref[...]).astype(BF16)


def _rope_store(z, cos, sin, o_ref, scale):
    lane = lax.broadcasted_iota(jnp.int32, cos.shape, 1)
    first_half = (lane % HEAD_DIM) < HALF
    for c in range(z.shape[1] // LANES):
        zc = z[:, c * LANES:(c + 1) * LANES]
        partner = jnp.where(first_half, pltpu.roll(zc, LANES - HALF, 1), pltpu.roll(zc, HALF, 1))
        o_ref[:, c * LANES:(c + 1) * LANES] = ((zc * cos + partner * sin) * scale).astype(o_ref.dtype)


def _inproj_sample_kernel(x_ref, sh_ref, sc_ref, nw_ref, cos_ref, sin_ref, w_ref,
                          rq_ref, rk_ref, rv_ref, rg_ref, mq_ref, mk_ref, mv_ref, *, width):
    h = _normed_input(x_ref, sh_ref, sc_ref, nw_ref)
    proj = lambda g: jnp.dot(h, w_ref[:, g * width:(g + 1) * width], preferred_element_type=F32)
    _rope_store(proj(0), cos_ref[...], sin_ref[...], rq_ref, 1.0)
    _rope_store(proj(1), cos_ref[...], sin_ref[...], rk_ref, HEAD_DIM ** -0.5)
    rv_ref[...] = proj(2)
    rg_ref[...] = _silu(proj(3))
    mq_ref[...] = proj(4) * MOBA_QSCALE
    mk_ref[...] = proj(5)
    mv_ref[...] = proj(6)


def _inproj_sample(x, sh, sc, nw, cos_t, sin_t, w_bf):
    G, R, D = x.shape
    width = w_bf.shape[1] // 7
    row = pl.BlockSpec((None, R, D), lambda g, i: (g, i, 0))
    tab = pl.BlockSpec((R, LANES), lambda g, i: (i, 0))
    act = pl.BlockSpec((None, R, width), lambda g, i: (g, i, 0))
    return pl.pallas_call(
        functools.partial(_inproj_sample_kernel, width=width),
        out_shape=[jax.ShapeDtypeStruct((G, R, width), F32)] * 7,
        grid=(G, 1),
        in_specs=[row, _mod_spec(sh, R), _mod_spec(sc, R), _const_spec(nw), tab, tab, _const_spec(w_bf)],
        out_specs=[act] * 7,
        compiler_params=_params("parallel", "parallel"),
        name="inproj_sample",
    )(x, sh, sc, nw, cos_t, sin_t, w_bf)


def _inproj_prompt_kernel(x_ref, sh_ref, sc_ref, nw_ref, cos_ref, sin_ref, cost_ref, sint_ref, w_ref, wt_ref,
                          rq_ref, rkt_ref, rv_ref, rg_ref, mqt_ref, mk_ref, mkt_ref, mvt_ref, *, width):
    h = _normed_input(x_ref, sh_ref, sc_ref, nw_ref)
    proj = lambda g: jnp.dot(h, w_ref[:, g * width:(g + 1) * width], preferred_element_type=F32)
    proj_t = lambda g: lax.dot_general(wt_ref[g * width:(g + 1) * width, :], h, NT, preferred_element_type=F32)
    _rope_store(proj(0), cos_ref[...], sin_ref[...], rq_ref, 1.0)
    rv_ref[...] = proj(1).astype(rv_ref.dtype)
    rg_ref[...] = _silu(proj(2)).astype(rg_ref.dtype)
    mk_ref[...] = proj(3).astype(mk_ref.dtype)
    zt = proj_t(0)
    cost = cost_ref[...]
    sint = sint_ref[...]
    scale = HEAD_DIM ** -0.5
    for hd in range(width // HEAD_DIM):
        lo = slice(hd * HEAD_DIM, hd * HEAD_DIM + HALF)
        hi = slice(hd * HEAD_DIM + HALF, (hd + 1) * HEAD_DIM)
        a, b = zt[lo], zt[hi]
        rkt_ref[lo, :] = ((a * cost - b * sint) * scale).astype(rkt_ref.dtype)
        rkt_ref[hi, :] = ((a * sint + b * cost) * scale).astype(rkt_ref.dtype)
    mqt_ref[...] = (proj_t(1) * MOBA_QSCALE).astype(mqt_ref.dtype)
    mkt_ref[...] = proj_t(2)
    mvt_ref[...] = proj_t(3)


def _inproj_prompt(x, sh, sc, nw, cos_t, sin_t, cos_tt, sin_tt, w_tok, w_trn, *, tm):
    G, R, D = x.shape
    width = w_tok.shape[1] // 4
    row = pl.BlockSpec((None, tm, D), lambda g, i: (g, i, 0))
    tab = pl.BlockSpec((tm, LANES), lambda g, i: (i, 0))
    tab_t = pl.BlockSpec((HALF, tm), lambda g, i: (0, i))
    act = pl.BlockSpec((None, tm, width), lambda g, i: (g, i, 0))
    act_t = pl.BlockSpec((None, width, tm), lambda g, i: (g, 0, i))
    tok = jax.ShapeDtypeStruct((G, R, width), BF16)
    trn = lambda dt: jax.ShapeDtypeStruct((G, width, R), dt)
    return pl.pallas_call(
        functools.partial(_inproj_prompt_kernel, width=width),
        out_shape=[tok, trn(BF16), tok, tok, trn(BF16), tok, trn(F32), trn(F32)],
        grid=(G, R // tm),
        in_specs=[row, _mod_spec(sh, tm), _mod_spec(sc, tm), _const_spec(nw), tab, tab, tab_t, tab_t,
                  _const_spec(w_tok), _const_spec(w_trn)],
        out_specs=[act, act_t, act, act, act_t, act, act_t, act_t],
        compiler_params=_params("parallel", "parallel"),
        name="inproj_prompt",
    )(x, sh, sc, nw, cos_t, sin_t, cos_tt, sin_tt, w_tok, w_trn)


def _retention_kernel(q_ref, kt_ref, v_ref, g_ref, s0_ref, dmat2_ref, qdec_ref, kdect_ref, cdec_ref, gnw_ref,
                      avg_ref, o_ref, sout_ref, s_scr, o_scr, *, heads, chunks):
    j = pl.program_id(1)
    pairs = heads // 2
    lane = lax.broadcasted_iota(jnp.int32, (LANES, LANES), 1)
    sub = lax.broadcasted_iota(jnp.int32, (LANES, LANES), 0)
    first = lane < HEAD_DIM
    diag = first == (sub < HEAD_DIM)
    keep_a = jnp.where(first, 1.0, 0.0).astype(BF16)
    keep_b = jnp.where(first, 0.0, 1.0).astype(BF16)

    @pl.when(j == 0)
    def _():
        z = jnp.zeros((HEAD_DIM, HEAD_DIM), F32)
        for p in range(pairs):
            s_scr[p] = jnp.concatenate([jnp.concatenate([s0_ref[2 * p], z], axis=1),
                                        jnp.concatenate([z, s0_ref[2 * p + 1]], axis=1)], axis=0)

    gnw = gnw_ref[...]
    avg = avg_ref[...]

    def group_mean(t):
        hi = t.astype(BF16)
        lo = (t - hi.astype(F32)).astype(BF16)
        return jnp.dot(hi, avg, preferred_element_type=F32) + jnp.dot(lo, avg, preferred_element_type=F32)

    for c in range(chunks):
        rows = slice(c * RET_CHUNK, (c + 1) * RET_CHUNK)
        for p in range(pairs):
            cs = slice(p * LANES, (p + 1) * LANES)
            qp = q_ref[rows, cs].astype(BF16)
            vp = v_ref[rows, cs].astype(BF16)
            ktp = kt_ref[cs, rows]
            q2 = jnp.concatenate([qp * keep_a, qp * keep_b], axis=0)
            att2 = jnp.dot(q2, ktp.astype(BF16), preferred_element_type=F32) * dmat2_ref[p]
            o2 = jnp.dot(att2.astype(BF16), vp, preferred_element_type=F32)
            s = s_scr[p]
            o = (jnp.where(first, o2[:RET_CHUNK], o2[RET_CHUNK:])
                 + jnp.dot(qp, s.astype(BF16), preferred_element_type=F32) * qdec_ref[:, cs])
            kdt = (ktp.astype(F32) * kdect_ref[cs, :]).astype(BF16)
            s_scr[p] = s * cdec_ref[p] + jnp.where(diag, jnp.dot(kdt, vp, preferred_element_type=F32), 0.0)
            o_scr[rows, cs] = o

    o_all = o_scr[...]
    d = o_all - group_mean(o_all)
    var = group_mean(d * d)
    o_ref[...] = (g_ref[...].astype(F32) * (d * lax.rsqrt(var + NORM_EPS) * gnw)).astype(o_ref.dtype)

    @pl.when(j == pl.num_programs(1) - 1)
    def _():
        for p in range(pairs):
            s = s_scr[p]
            sout_ref[2 * p] = s[:HEAD_DIM, :HEAD_DIM]
            sout_ref[2 * p + 1] = s[HEAD_DIM:, HEAD_DIM:]


def _retention(q, kt, v, gs, s0, tables, gnw, *, tl, out_dtype):
    B, L, W = q.shape
    heads = W // HEAD_DIM
    seq = pl.BlockSpec((None, tl, W), lambda b, j: (b, j, 0))
    seq_t = pl.BlockSpec((None, W, tl), lambda b, j: (b, 0, j))
    st = pl.BlockSpec((None, heads, HEAD_DIM, HEAD_DIM), lambda b, j: (b, 0, 0, 0))
    return pl.pallas_call(
        functools.partial(_retention_kernel, heads=heads, chunks=tl // RET_CHUNK),
        out_shape=[jax.ShapeDtypeStruct((B, L, W), out_dtype),
                   jax.ShapeDtypeStruct((B, heads, HEAD_DIM, HEAD_DIM), F32)],
        grid=(B, L // tl),
        in_specs=[seq, seq_t, seq, seq, st] + [_const_spec(t) for t in tables] + [_const_spec(gnw),
                                                                                   _const_spec(_head_average(W))],
        out_specs=[seq, st],
        scratch_shapes=[pltpu.VMEM((heads // 2, LANES, LANES), F32), pltpu.VMEM((tl, W), F32)],
        compiler_params=_params("parallel", "arbitrary"),
        name="retention",
    )(q, kt, v, gs, s0, *tables, gnw, _head_average(W))


def _head_average(width):
    r = jnp.arange(width) // HEAD_DIM
    return jnp.where(r[:, None] == r[None, :], 1.0 / HEAD_DIM, 0.0).astype(BF16)


def _decay_tables(heads, c_len, pad_len):
    log_g = jnp.log1p(-jnp.exp2(-5.0 - jnp.arange(heads, dtype=F32)))
    idx = jnp.arange(c_len, dtype=F32)
    diff = idx[:, None] - idx[None, :]
    dmat = jnp.where(diff >= 0, jnp.exp(jnp.maximum(diff, 0.0) * log_g[:, None, None]), 0.0)
    q_dec = jnp.exp((idx + 1.0) * log_g[:, None])
    k_dec = jnp.exp((c_len - 1.0 - idx) * log_g[:, None])
    c_dec = jnp.exp(c_len * log_g)
    pad = pad_len - c_len
    dmat2 = jnp.pad(dmat, ((0, 0), (0, pad), (0, pad))).reshape(heads // 2, 2 * pad_len, pad_len)
    qdec = jnp.pad(jnp.repeat(q_dec.T, HEAD_DIM, axis=1), ((0, pad), (0, 0)))
    kdect = jnp.pad(jnp.repeat(k_dec, HEAD_DIM, axis=0), ((0, 0), (0, pad)))
    cdec = jnp.broadcast_to(jnp.repeat(c_dec, HEAD_DIM).reshape(heads // 2, 2 * HEAD_DIM, 1),
                            (heads // 2, 2 * HEAD_DIM, 2 * HEAD_DIM))
    return dmat2, qdec, kdect, cdec


def _col_reduce(x, op, final):
    while x.shape[0] % (2 * SUBLANES) == 0:
        half = x.shape[0] // 2
        x = op(x[:half], x[half:])
    return final(x, axis=0, keepdims=True)


def _moba_prompt_kernel(qt_ref, k_ref, kt_ref, vt_ref, nwt_ref, o_ref, *, seq):
    nb = seq // MOBA_BLOCK
    nbp = -(-nb // SUBLANES) * SUBLANES
    qb = MOBA_BLOCK
    pair = 2 * HEAD_DIM
    if nb > MOBA_TOPK + 1:
        lane_n = lax.broadcasted_iota(jnp.int32, (pair, LANES), 1)
        kmt = jnp.zeros((pair, LANES), F32)
        for n in range(nb):
            tot = kt_ref[:, n * MOBA_BLOCK:n * MOBA_BLOCK + LANES]
            for c in range(1, MOBA_BLOCK // LANES):
                tot = tot + kt_ref[:, n * MOBA_BLOCK + c * LANES:n * MOBA_BLOCK + (c + 1) * LANES]
            kmt = jnp.where(lane_n == n, jnp.sum(tot, axis=1, keepdims=True) * (1.0 / MOBA_BLOCK), kmt)
        km = kmt.T[0:nbp]
        lane_k = lax.broadcasted_iota(jnp.int32, km.shape, 1)
        km_heads = [jnp.where(lane_k < HEAD_DIM, km, 0.0), jnp.where(lane_k < HEAD_DIM, 0.0, km)]
        blk = lax.broadcasted_iota(jnp.int32, (nbp, qb), 0)
    key_i = lax.broadcasted_iota(jnp.int32, (qb, qb), 0)
    qry_i = lax.broadcasted_iota(jnp.int32, (qb, qb), 1)
    causal = key_i <= qry_i
    zeros = jnp.zeros((HEAD_DIM, qb), BF16)
    vtb = vt_ref[...].astype(BF16)

    def scores(i, hh):
        qt = qt_ref[:, i * qb:(i + 1) * qb]
        hr = slice(hh * HEAD_DIM, (hh + 1) * HEAD_DIM)
        qm = jnp.concatenate([qt[hr], zeros] if hh == 0 else [zeros, qt[hr]], axis=0)
        return jnp.dot(k_ref[0:(i + 1) * MOBA_BLOCK, :], qm, preferred_element_type=F32)

    def softmax(i, hh, st):
        biases = [None] * i
        if i > MOBA_TOPK:
            qt = qt_ref[:, i * qb:(i + 1) * qb].astype(F32)
            gate = jnp.dot(km_heads[hh], qt, preferred_element_type=F32, precision=HIGHEST)
            valid = blk < i
            for n in range(i):
                gn = gate[n:n + 1, :]
                ahead = valid & ((gate > gn) | ((gate == gn) & (blk < n)))
                rank = jnp.sum(ahead.astype(F32), axis=0, keepdims=True)
                biases[n] = jnp.where(rank < MOBA_TOPK, 0.0, NEG_INF)
        past = [st[n * MOBA_BLOCK:(n + 1) * MOBA_BLOCK] for n in range(i)]
        own = jnp.where(causal, st[i * MOBA_BLOCK:(i + 1) * MOBA_BLOCK], NEG_INF)
        m = _col_reduce(own, jnp.maximum, jnp.max)
        for sb, bias in zip(past, biases):
            mb = _col_reduce(sb, jnp.maximum, jnp.max)
            m = jnp.maximum(m, mb if bias is None else mb + bias)
        p = jnp.exp2(own - m)
        l = _col_reduce(p, jnp.add, jnp.sum)
        pieces = []
        for sb, bias in zip(past, biases):
            pb = jnp.exp2(sb + (-m if bias is None else bias - m))
            l = l + _col_reduce(pb, jnp.add, jnp.sum)
            pieces.append(pb.astype(BF16))
        pieces.append(p.astype(BF16))
        return (jnp.concatenate(pieces, axis=0) if i > 0 else pieces[0]), l

    def output(i, hh, p, l):
        hr = slice(hh * HEAD_DIM, (hh + 1) * HEAD_DIM)
        ot = jnp.dot(vtb[hr, 0:(i + 1) * MOBA_BLOCK], p, preferred_element_type=F32) / l
        ms = jnp.mean(ot * ot, axis=0, keepdims=True)
        o_ref[hr, i * qb:(i + 1) * qb] = (ot * lax.rsqrt(ms + NORM_EPS) * nwt_ref[hr, :]).astype(o_ref.dtype)

    bodies = [(i, hh) for i in range(nb) for hh in range(2)]
    st_next = scores(*bodies[0])
    pending = None
    for idx, body in enumerate(bodies):
        st = st_next
        if idx + 1 < len(bodies):
            st_next = scores(*bodies[idx + 1])
        p, l = softmax(*body, st)
        if pending is not None:
            output(*pending)
        pending = (*body, p, l)
    output(*pending)


def _moba_prompt(qt, k, kt, vt, nwt, *, out_dtype):
    B, W, S = qt.shape
    pair = 2 * HEAD_DIM
    trn = pl.BlockSpec((None, pair, S), lambda b, h: (b, h, 0))
    return pl.pallas_call(
        functools.partial(_moba_prompt_kernel, seq=S),
        out_shape=jax.ShapeDtypeStruct((B, W, S), out_dtype),
        grid=(B, W // pair),
        in_specs=[trn, pl.BlockSpec((None, S, pair), lambda b, h: (b, 0, h)), trn, trn,
                  pl.BlockSpec((pair, MOBA_BLOCK), lambda b, h: (h, 0))],
        out_specs=trn,
        compiler_params=_params("parallel", "parallel"),
        name="moba_prompt",
    )(qt, k, kt, vt, nwt)


def _topk_kernel(q_ref, km_ref, o_ref, *, heads):
    n_blk = km_ref.shape[1]
    lane_b = lax.broadcasted_iota(jnp.int32, (q_ref.shape[0], n_blk), 1).astype(F32)
    lane_o = lax.broadcasted_iota(jnp.int32, o_ref.shape, 1)
    out = jnp.zeros(o_ref.shape, F32)
    for h in range(heads):
        hs = slice(h * HEAD_DIM, (h + 1) * HEAD_DIM)
        gate = jnp.dot(q_ref[:, hs], km_ref[hs, :], preferred_element_type=F32, precision=HIGHEST)
        for r in range(MOBA_TOPK):
            m = jnp.max(gate, axis=1, keepdims=True)
            idx = jnp.min(jnp.where(gate == m, lane_b, float(n_blk)), axis=1, keepdims=True)
            out = jnp.where(lane_o == h * MOBA_TOPK + r, idx, out)
            gate = jnp.where(lane_b == idx, NEG_INF, gate)
    o_ref[...] = out.astype(jnp.int32)


def _topk(q8, kmt):
    bd, rows, W = q8.shape
    n_full = kmt.shape[2]
    return pl.pallas_call(
        functools.partial(_topk_kernel, heads=W // HEAD_DIM),
        out_shape=jax.ShapeDtypeStruct((bd, rows, LANES), jnp.int32),
        grid=(bd,),
        in_specs=[pl.BlockSpec((None, rows, W), lambda b: (b, 0, 0)),
                  pl.BlockSpec((None, W, n_full), lambda b: (b, 0, 0))],
        out_specs=pl.BlockSpec((None, rows, LANES), lambda b: (b, 0, 0)),
        compiler_params=_params("parallel"),
        name="moba_topk",
    )(q8, kmt)


def _attend_kernel(top_ref, pt_ref, q_ref, knt_ref, vnt_ref, nw_ref, ck_ref, cv_ref, o_ref,
                   kbuf, vbuf, sem, *, heads, tokens, n_pages, page):
    b = pl.program_id(0)
    nb = pl.num_programs(0)
    ppb = MOBA_BLOCK // page
    n_sel = MOBA_TOPK * MOBA_BLOCK
    slot = b % 2

    def page_copies(seq, buf_slot):
        out = []
        for h in range(heads):
            for t in range(tokens):
                for r in range(MOBA_TOPK):
                    blk = top_ref[((seq * tokens + t) * heads + h) * MOBA_TOPK + r]
                    for p in range(ppb):
                        phys = pt_ref[seq * n_pages + blk * ppb + p]
                        dst = pl.ds((r * ppb + p) * page, page)
                        out.append(pltpu.make_async_copy(ck_ref.at[phys, h], kbuf.at[buf_slot, h, t, :, dst],
                                                         sem.at[buf_slot, 0]))
                        out.append(pltpu.make_async_copy(cv_ref.at[phys, h], vbuf.at[buf_slot, h, t, :, dst],
                                                         sem.at[buf_slot, 1]))
        return out

    @pl.when(b == 0)
    def _():
        for c in page_copies(b, slot):
            c.start()

    @pl.when(b + 1 < nb)
    def _():
        for c in page_copies(b + 1, 1 - slot):
            c.start()

    for c in page_copies(b, slot):
        c.wait()
    kb = kbuf.at[slot]
    vb = vbuf.at[slot]
    for h in range(heads):
        for t in range(tokens):
            kb[h, t, :, n_sel:n_sel + LANES] = knt_ref[h]
            vb[h, t, :, n_sel:n_sel + LANES] = vnt_ref[h]

    rows = q_ref.shape[0]
    col = lax.broadcasted_iota(jnp.int32, (rows, n_sel + LANES), 1)
    row = lax.broadcasted_iota(jnp.int32, (rows, HEAD_DIM), 0)
    pairs = [(h, t) for h in range(heads) for t in range(tokens)]
    qs = [q_ref[:, h * HEAD_DIM:(h + 1) * HEAD_DIM].astype(BF16) for h in range(heads)]
    scores = [jnp.where(col <= n_sel + t,
                        jnp.dot(qs[h], kb[h, t].astype(BF16), preferred_element_type=F32), NEG_INF)
              for h, t in pairs]
    probs = []
    for s in scores:
        p = jnp.exp2(s - jnp.max(s, axis=1, keepdims=True))
        probs.append((p.astype(BF16), jnp.sum(p, axis=1, keepdims=True)))
    outs = [jnp.zeros((rows, HEAD_DIM), F32)] * heads
    for (h, t), (p, l) in zip(pairs, probs):
        o = lax.dot_general(p, vb[h, t].astype(BF16), NT, preferred_element_type=F32) / l
        outs[h] = jnp.where(row == t, o, outs[h])
    o_ref[...] = jnp.concatenate([_rms(outs[h], nw_ref[:, h * HEAD_DIM:(h + 1) * HEAD_DIM])
                                  for h in range(heads)], axis=1)


def _attend(top_flat, pt_flat, q8, knt, vnt, nw, ckt4, cvt4, *, tokens, n_pages):
    bd, rows, W = q8.shape
    heads = W // HEAD_DIM
    page = ckt4.shape[3]
    n_keys = MOBA_TOPK * MOBA_BLOCK + LANES
    any_spec = pl.BlockSpec(memory_space=pl.ANY)
    return pl.pallas_call(
        functools.partial(_attend_kernel, heads=heads, tokens=tokens, n_pages=n_pages, page=page),
        out_shape=jax.ShapeDtypeStruct((bd, rows, W), F32),
        grid_spec=pltpu.PrefetchScalarGridSpec(
            num_scalar_prefetch=2,
            grid=(bd,),
            in_specs=[pl.BlockSpec((None, rows, W), lambda b, *_: (b, 0, 0)),
                      pl.BlockSpec((None, heads, HEAD_DIM, LANES), lambda b, *_: (b, 0, 0, 0)),
                      pl.BlockSpec((None, heads, HEAD_DIM, LANES), lambda b, *_: (b, 0, 0, 0)),
                      pl.BlockSpec((1, W), lambda b, *_: (0, 0)),
                      any_spec, any_spec],
            out_specs=pl.BlockSpec((None, rows, W), lambda b, *_: (b, 0, 0)),
            scratch_shapes=[pltpu.VMEM((2, heads, tokens, HEAD_DIM, n_keys), F32),
                            pltpu.VMEM((2, heads, tokens, HEAD_DIM, n_keys), F32),
                            pltpu.SemaphoreType.DMA((2, 2))],
        ),
        compiler_params=_params("arbitrary"),
        name="moba_attend",
    )(top_flat, pt_flat, q8, knt, vnt, nw, ckt4, cvt4)


FF_CHUNKS = 2
SIDE_SLOTS = FF_CHUNKS + 2


def _outffn_compute(x_ref, ro_ref, mo_ref, g1_ref, sh2_ref, sc2_ref, g2_ref, n2w_ref, fw_ref,
                    wo_ref, wg_ref, wu_ref, wd_ref, y_ref, *, ff_chunks, moba_transposed, side_work):
    side_work(0)
    rw = ro_ref.shape[1]
    mo = mo_ref[...].astype(BF16)
    attn = (jnp.dot(ro_ref[...].astype(BF16), wo_ref[0:rw, :], preferred_element_type=F32)
            + lax.dot_general(mo, wo_ref[rw:, :], TN if moba_transposed else (((1,), (0,)), ((), ())),
                              preferred_element_type=F32))
    x1 = x_ref[...] + g1_ref[...] * attn
    h2 = (_rms(x1, n2w_ref[...]) * (1.0 + sc2_ref[...]) + sh2_ref[...]).astype(BF16)
    acc = None
    for c, (lo, hi) in enumerate(ff_chunks):
        side_work(1 + c)
        gate = jnp.dot(h2, wg_ref[:, lo:hi], preferred_element_type=F32)
        up = jnp.dot(h2, wu_ref[:, lo:hi], preferred_element_type=F32)
        part = jnp.dot((_silu(gate) * up).astype(BF16), wd_ref[lo:hi, :], preferred_element_type=F32)
        acc = part if acc is None else acc + part
    side_work(1 + len(ff_chunks))
    x2 = x1 + g2_ref[...] * acc
    y_ref[...] = _rms(x2, fw_ref[...])


def _outffn_kernel(*refs, ff_chunks, moba_transposed):
    _outffn_compute(*refs, ff_chunks=ff_chunks, moba_transposed=moba_transposed, side_work=lambda c: None)


def _outffn_pool_kernel(pt_ref, *refs, ff_chunks, moba_transposed, inner_steps, pages_per_step):
    ck_ref, y_ref, km_ref, pbuf, sem = refs[-5:]
    step = pl.program_id(0) * inner_steps + pl.program_id(1)
    n_steps = pl.num_programs(0) * inner_steps
    n_pages = pt_ref.shape[1]
    steps_per_seq = n_pages // pages_per_step
    chunk_pages = pages_per_step // SIDE_SLOTS
    ppb = MOBA_BLOCK // pbuf.shape[3]
    chunk_blocks = chunk_pages // ppb
    lane = lax.broadcasted_iota(jnp.int32, km_ref.shape, 1)

    def chunk_copies(st, c, slot):
        seq = st // steps_per_seq
        base = (st % steps_per_seq) * pages_per_step + c * chunk_pages
        return [pltpu.make_async_copy(ck_ref.at[pt_ref[seq, base + p]], pbuf.at[slot, p], sem.at[slot])
                for p in range(chunk_pages)]

    def side_work(c):
        slot = c % 2
        if c == 0:
            @pl.when(step == 0)
            def _():
                for cp in chunk_copies(step, 0, 0):
                    cp.start()

            @pl.when(step % steps_per_seq == 0)
            def _():
                km_ref[...] = jnp.zeros_like(km_ref)
        if c + 1 < SIDE_SLOTS:
            for cp in chunk_copies(step, c + 1, 1 - slot):
                cp.start()
        else:
            @pl.when(step + 1 < n_steps)
            def _():
                for cp in chunk_copies(step + 1, 0, 1 - slot):
                    cp.start()
        for cp in chunk_copies(step, c, slot):
            cp.wait()
        first_block = (step % steps_per_seq) * (pages_per_step // ppb) + c * chunk_blocks
        acc = km_ref[...]
        for n in range(chunk_blocks):
            tot = pbuf[slot, n * ppb]
            for p in range(1, ppb):
                tot = tot + pbuf[slot, n * ppb + p]
            col = jnp.sum(tot, axis=1, keepdims=True) * (1.0 / MOBA_BLOCK)
            acc = jnp.where(lane == first_block + n, col, acc)
        km_ref[...] = acc

    _outffn_compute(*refs[:-5], y_ref, ff_chunks=ff_chunks, moba_transposed=moba_transposed, side_work=side_work)


def _outffn(x, ro, mo, g1, sh2, sc2, g2, n2w, fw, wo, wg, wu, wd, *, tm, moba_transposed, pool=None):
    G, R, D = x.shape
    W = ro.shape[2]
    dff = wg.shape[1]
    cuts = [round(dff * c / FF_CHUNKS / 256) * 256 for c in range(FF_CHUNKS)] + [dff]
    ff_chunks = tuple(zip(cuts[:-1], cuts[1:]))
    inner = R // tm
    row = pl.BlockSpec((None, tm, D), lambda g, i, *_: (g, i, 0))
    act = pl.BlockSpec((None, tm, W), lambda g, i, *_: (g, i, 0))
    act_t = pl.BlockSpec((None, W, tm), lambda g, i, *_: (g, 0, i))
    in_specs = [row, act, act_t if moba_transposed else act,
                _mod_spec(g1, tm), _mod_spec(sh2, tm), _mod_spec(sc2, tm), _mod_spec(g2, tm),
                _const_spec(n2w), _const_spec(fw), _const_spec(wo), _const_spec(wg), _const_spec(wu),
                _const_spec(wd)]
    args = (x, ro, mo, g1, sh2, sc2, g2, n2w, fw, wo, wg, wu, wd)
    y_shape = jax.ShapeDtypeStruct((G, R, D), F32)
    if pool is None:
        return pl.pallas_call(
            functools.partial(_outffn_kernel, ff_chunks=ff_chunks, moba_transposed=moba_transposed),
            out_shape=y_shape, grid=(G, inner), in_specs=in_specs, out_specs=row,
            compiler_params=_params("parallel", "parallel"),
            name="outproj_ffn",
        )(*args)
    page_table, ckt = pool
    bd, n_pages = page_table.shape
    _, hd, page = ckt.shape
    ppb = MOBA_BLOCK // page
    n_steps = G * inner
    pages_per_step = bd * n_pages // n_steps
    assert pages_per_step * n_steps == bd * n_pages and n_pages % pages_per_step == 0
    assert pages_per_step % (SIDE_SLOTS * ppb) == 0 and SIDE_SLOTS % 2 == 0
    steps_per_seq = n_pages // pages_per_step
    n_blocks = n_pages // ppb
    km_spec = pl.BlockSpec((None, hd, n_blocks), lambda g, i, *_: ((g * inner + i) // steps_per_seq, 0, 0))
    return pl.pallas_call(
        functools.partial(_outffn_pool_kernel, ff_chunks=ff_chunks, moba_transposed=moba_transposed,
                          inner_steps=inner, pages_per_step=pages_per_step),
        out_shape=[y_shape, jax.ShapeDtypeStruct((bd, hd, n_blocks), F32)],
        grid_spec=pltpu.PrefetchScalarGridSpec(
            num_scalar_prefetch=1,
            grid=(G, inner),
            in_specs=in_specs + [pl.BlockSpec(memory_space=pl.ANY)],
            out_specs=[row, km_spec],
            scratch_shapes=[pltpu.VMEM((2, pages_per_step // SIDE_SLOTS, hd, page), F32),
                            pltpu.SemaphoreType.DMA((2,))],
        ),
        compiler_params=_params("arbitrary", "arbitrary"),
        name="outproj_ffn_pool",
    )(page_table, *args, ckt)


def _rope_angles(pos):
    inv = ROPE_BASE ** (-jnp.arange(HALF, dtype=F32) / HALF)
    ang = pos.astype(F32)[:, None] * inv[None, :]
    return jnp.cos(ang), jnp.sin(ang)


def _rope_tables(pos):
    cos, sin = _rope_angles(pos)
    reps = LANES // HEAD_DIM
    return jnp.tile(cos, (1, 2 * reps)), jnp.tile(jnp.concatenate([-sin, sin], axis=1), (1, reps))


def kernel(x_prompt, x_sample, cache_k, cache_v, state_ret, page_table, c_prompt, c_sample,
           norm1_w, norm2_w, final_w, w_mod, b_mod, w_in, ret_gn_w, moba_norm_w, w_out,
           w_gate, w_up, w_down):
    Bp, S, D = x_prompt.shape
    Bd, T, _ = x_sample.shape
    depth = w_in.shape[0]
    assert depth == 1, "single decoder layer"
    n_pool, page, m_heads = cache_k.shape[1], cache_k.shape[2], cache_k.shape[3]
    n_pages = page_table.shape[1]
    past_len = n_pages * page
    assert past_len % MOBA_BLOCK == 0 and MOBA_BLOCK % page == 0
    n_full = past_len // MOBA_BLOCK
    moba_w = m_heads * HEAD_DIM
    ret_w = (w_in.shape[2] - 3 * moba_w) // 4
    assert ret_w == moba_w, "the two head groups share one projection width"
    W = ret_w
    r_heads = ret_w // HEAD_DIM
    fw = final_w.reshape(1, D)
    rows_s = Bd * T
    pad8 = SUBLANES

    w_in0 = w_in[0].astype(BF16)
    col = lambda g: w_in0[:, g * W:(g + 1) * W]
    w_tok = jnp.concatenate([col(0), col(2), col(3), col(5)], axis=1)
    w_trn = jnp.concatenate([col(1), col(4), col(5), col(6)], axis=1).T
    wo, wg, wu, wd = (w[0].astype(BF16) for w in (w_out, w_gate, w_up, w_down))
    n1w, n2w = norm1_w[0].reshape(1, D), norm2_w[0].reshape(1, D)
    gnw, mnw = ret_gn_w[0].reshape(1, ret_w), moba_norm_w[0].reshape(1, moba_w)

    mod = _modulation(jnp.concatenate([c_prompt, c_sample], axis=0), w_mod[0], b_mod[0])
    mod_p = [m[:, None, :] for m in jnp.split(mod[:Bp], 6, axis=-1)]
    mod_s = [jnp.repeat(m, T, axis=0)[None] for m in jnp.split(mod[Bp:], 6, axis=-1)]

    cos_a, sin_a = _rope_angles(jnp.arange(S))
    cos_p, sin_p = _rope_tables(jnp.arange(S))
    rq, rkt, rv, rg, mqt, mk, mkt, mvt = _inproj_prompt(x_prompt, mod_p[0], mod_p[1], n1w, cos_p, sin_p,
                                                        cos_a.T, sin_a.T, w_tok, w_trn, tm=512)
    s0_p = jnp.zeros((Bp, r_heads, HEAD_DIM, HEAD_DIM), F32)
    ret_o, ret_p = _retention(rq, rkt, rv, rg, s0_p, _decay_tables(r_heads, RET_CHUNK, RET_CHUNK), gnw,
                              tl=512, out_dtype=BF16)
    nwt = jnp.broadcast_to(mnw.reshape(moba_w, 1), (moba_w, MOBA_BLOCK))
    moba_ot = _moba_prompt(mqt, mk, mkt, mvt, nwt, out_dtype=BF16)
    ckt4 = jnp.transpose(cache_k[0], (0, 2, 3, 1))
    cvt4 = jnp.transpose(cache_v[0], (0, 2, 3, 1))
    y_prompt, kmt = _outffn(x_prompt, ret_o, moba_ot, mod_p[2], mod_p[3], mod_p[4], mod_p[5], n2w, fw,
                            wo, wg, wu, wd, tm=512, moba_transposed=True,
                            pool=(page_table, ckt4.reshape(n_pool, moba_w, page)))
    to_rows = lambda t: t.reshape(1, Bp, m_heads, HEAD_DIM, S).transpose(0, 1, 4, 2, 3)
    k_prompt, v_prompt = to_rows(mkt), to_rows(mvt)

    xs = x_sample.reshape(1, rows_s, D)
    cos_s, sin_s = _rope_tables(past_len + jnp.arange(rows_s) % T)
    sq, sk, sv, sg, smq, smk, smv = _inproj_sample(xs, mod_s[0], mod_s[1], n1w, cos_s, sin_s, w_in0)
    padc = lambda t: jnp.pad(t.reshape(Bd, T, ret_w), ((0, 0), (0, RET_CHUNK - T), (0, 0)))
    skt = jnp.pad(sk.reshape(Bd, T, ret_w).transpose(0, 2, 1), ((0, 0), (0, 0), (0, RET_CHUNK - T)))
    ret_os, ret_s = _retention(padc(sq), skt, padc(sv), padc(sg), state_ret[0],
                               _decay_tables(r_heads, T, RET_CHUNK), gnw, tl=RET_CHUNK, out_dtype=F32)
    ret_os = ret_os[:, :T].reshape(1, rows_s, ret_w)

    assert kmt.shape[2] == n_full
    q8 =jnp.pad(smq.reshape(Bd, T, moba_w), ((0, 0), (0, pad8 - T), (0, 0)))
    top = _topk(q8, kmt)[:, :T, :m_heads * MOBA_TOPK]
    new_t = lambda t: jnp.pad(t.reshape(Bd, T, m_heads, HEAD_DIM).transpose(0, 2, 3, 1),
                              ((0, 0), (0, 0), (0, 0), (0, LANES - T)))
    moba_os = _attend(top.reshape(-1), page_table.reshape(-1), q8, new_t(smk), new_t(smv), mnw, ckt4, cvt4,
                      tokens=T, n_pages=n_pages)
    moba_os = moba_os[:, :T].reshape(1, rows_s, moba_w)
    y_sample = _outffn(xs, ret_os, moba_os, mod_s[2], mod_s[3], mod_s[4], mod_s[5], n2w, fw,
                       wo, wg, wu, wd, tm=rows_s, moba_transposed=False).reshape(Bd, T, D)
    k_sample = smk.reshape(1, Bd, T, m_heads, HEAD_DIM)
    v_sample = smv.reshape(1, Bd, T, m_heads, HEAD_DIM)

    return (y_prompt, y_sample, k_prompt, v_prompt, ret_p[None], k_sample, v_sample, ret_s[None])
```

```python
import functools
import math

import jax
import jax.numpy as jnp
from jax import lax
from jax.experimental import pallas as pl
from jax.experimental.pallas import tpu as pltpu

F32 = jnp.float32
BF16 = jnp.bfloat16
HIGHEST = lax.Precision.HIGHEST

HEAD_DIM = 64
HALF = HEAD_DIM // 2
RET_CHUNK = 128
MOBA_BLOCK = 256
MOBA_TOPK = 3
ROPE_BASE = 10000.0
NORM_EPS = 1e-6
LANES = 128
SUBLANES = 8
VMEM_LIMIT = 56 * 1024 * 1024
NEG_INF = float("-inf")
MOBA_QSCALE = HEAD_DIM ** -0.5 * math.log2(math.e)
NT = (((1,), (1,)), ((), ()))
TN = (((0,), (0,)), ((), ()))


def _params(*sem):
    return pltpu.CompilerParams(dimension_semantics=sem, vmem_limit_bytes=VMEM_LIMIT)


def _rms(x, w):
    return x * lax.rsqrt(jnp.mean(x * x, axis=-1, keepdims=True) + NORM_EPS) * w


def _silu(x):
    return x * jax.nn.sigmoid(x)


def _const_spec(a):
    return pl.BlockSpec(a.shape, lambda *_: (0,) * a.ndim, pipeline_mode=pl.Buffered(1))


def _mod_kernel(c_ref, w_ref, b_ref, o_ref):
    s = _silu(c_ref[...])
    o_ref[...] = jnp.dot(s, w_ref[...], preferred_element_type=F32, precision=HIGHEST) + b_ref[...]


def _modulation(c, w_mod, b_mod):
    n, d = c.shape
    cols = w_mod.shape[1]
    tn = 1536
    return pl.pallas_call(
        _mod_kernel,
        out_shape=jax.ShapeDtypeStruct((n, cols), F32),
        grid=(cols // tn,),
        in_specs=[pl.BlockSpec((n, d), lambda j: (0, 0)),
                  pl.BlockSpec((d, tn), lambda j: (0, j)),
                  pl.BlockSpec((1, tn), lambda j: (0, j))],
        out_specs=pl.BlockSpec((n, tn), lambda j: (0, j)),
        compiler_params=_params("arbitrary"),
        name="modulation",
    )(c, w_mod, b_mod.reshape(1, cols))


def _mod_spec(arr, tm):
    if arr.shape[1] == 1:
        return pl.BlockSpec((None, 1, arr.shape[2]), lambda g, i, *_: (g, 0, 0))
    return pl.BlockSpec((None, tm, arr.shape[2]), lambda g, i, *_: (g, i, 0))


def _normed_input(x_ref, sh_ref, sc_ref, nw_ref):
    return (_rms(x_ref[...], nw_ref[...]) * (1.0 + sc_ref[...]) + sh_ref[...]).astype(BF16)


def _rope_store(z, cos, sin, o_ref, scale):
    lane = lax.broadcasted_iota(jnp.int32, cos.shape, 1)
    first_half = (lane % HEAD_DIM) < HALF
    for c in range(z.shape[1] // LANES):
        zc = z[:, c * LANES:(c + 1) * LANES]
        partner = jnp.where(first_half, pltpu.roll(zc, LANES - HALF, 1), pltpu.roll(zc, HALF, 1))
        o_ref[:, c * LANES:(c + 1) * LANES] = ((zc * cos + partner * sin) * scale).astype(o_ref.dtype)


def _inproj_sample_kernel(x_ref, sh_ref, sc_ref, nw_ref, cos_ref, sin_ref, w_ref,
                          rq_ref, rk_ref, rv_ref, rg_ref, mq_ref, mk_ref, mv_ref, *, width):
    h = _normed_input(x_ref, sh_ref, sc_ref, nw_ref)
    proj = lambda g: jnp.dot(h, w_ref[:, g * width:(g + 1) * width], preferred_element_type=F32)
    _rope_store(proj(0), cos_ref[...], sin_ref[...], rq_ref, 1.0)
    _rope_store(proj(1), cos_ref[...], sin_ref[...], rk_ref, HEAD_DIM ** -0.5)
    rv_ref[...] = proj(2)
    rg_ref[...] = _silu(proj(3))
    mq_ref[...] = proj(4) * MOBA_QSCALE
    mk_ref[...] = proj(5)
    mv_ref[...] = proj(6)


def _inproj_sample(x, sh, sc, nw, cos_t, sin_t, w_bf):
    G, R, D = x.shape
    width = w_bf.shape[1] // 7
    row = pl.BlockSpec((None, R, D), lambda g, i: (g, i, 0))
    tab = pl.BlockSpec((R, LANES), lambda g, i: (i, 0))
    act = pl.BlockSpec((None, R, width), lambda g, i: (g, i, 0))
    return pl.pallas_call(
        functools.partial(_inproj_sample_kernel, width=width),
        out_shape=[jax.ShapeDtypeStruct((G, R, width), F32)] * 7,
        grid=(G, 1),
        in_specs=[row, _mod_spec(sh, R), _mod_spec(sc, R), _const_spec(nw), tab, tab, _const_spec(w_bf)],
        out_specs=[act] * 7,
        compiler_params=_params("parallel", "parallel"),
        name="inproj_sample",
    )(x, sh, sc, nw, cos_t, sin_t, w_bf)


def _inproj_prompt_kernel(x_ref, sh_ref, sc_ref, nw_ref, cos_ref, sin_ref, cost_ref, sint_ref, w_ref, wt_ref,
                          rq_ref, rkt_ref, rv_ref, rg_ref, mqt_ref, mk_ref, mkt_ref, mvt_ref, *, width):
    h = _normed_input(x_ref, sh_ref, sc_ref, nw_ref)
    proj = lambda g: jnp.dot(h, w_ref[:, g * width:(g + 1) * width], preferred_element_type=F32)
    proj_t = lambda g: lax.dot_general(wt_ref[g * width:(g + 1) * width, :], h, NT, preferred_element_type=F32)
    _rope_store(proj(0), cos_ref[...], sin_ref[...], rq_ref, 1.0)
    rv_ref[...] = proj(1).astype(rv_ref.dtype)
    rg_ref[...] = _silu(proj(2)).astype(rg_ref.dtype)
    zt = proj_t(0)
    cost = cost_ref[...]
    sint = sint_ref[...]
    scale = HEAD_DIM ** -0.5
    for hd in range(width // HEAD_DIM):
        lo = slice(hd * HEAD_DIM, hd * HEAD_DIM + HALF)
        hi = slice(hd * HEAD_DIM + HALF, (hd + 1) * HEAD_DIM)
        a, b = zt[lo], zt[hi]
        rkt_ref[lo, :] = ((a * cost - b * sint) * scale).astype(rkt_ref.dtype)
        rkt_ref[hi, :] = ((a * sint + b * cost) * scale).astype(rkt_ref.dtype)
    mqt_ref[...] = (proj_t(1) * MOBA_QSCALE).astype(mqt_ref.dtype)
    mkt = proj_t(2)
    mkt_ref[...] = mkt
    mk_ref[...] = mkt.T.astype(mk_ref.dtype)
    mvt_ref[...] = proj_t(3)


def _inproj_prompt(x, sh, sc, nw, cos_t, sin_t, cos_tt, sin_tt, w_tok, w_trn, *, tm):
    G, R, D = x.shape
    width = w_tok.shape[1] // 3
    row = pl.BlockSpec((None, tm, D), lambda g, i: (g, i, 0))
    tab = pl.BlockSpec((tm, LANES), lambda g, i: (i, 0))
    tab_t = pl.BlockSpec((HALF, tm), lambda g, i: (0, i))
    act = pl.BlockSpec((None, tm, width), lambda g, i: (g, i, 0))
    act_t = pl.BlockSpec((None, width, tm), lambda g, i: (g, 0, i))
    tok = jax.ShapeDtypeStruct((G, R, width), BF16)
    trn = lambda dt: jax.ShapeDtypeStruct((G, width, R), dt)
    return pl.pallas_call(
        functools.partial(_inproj_prompt_kernel, width=width),
        out_shape=[tok, trn(BF16), tok, tok, trn(BF16), tok, trn(F32), trn(F32)],
        grid=(G, R // tm),
        in_specs=[row, _mod_spec(sh, tm), _mod_spec(sc, tm), _const_spec(nw), tab, tab, tab_t, tab_t,
                  _const_spec(w_tok), _const_spec(w_trn)],
        out_specs=[act, act_t, act, act, act_t, act, act_t, act_t],
        compiler_params=_params("parallel", "parallel"),
        name="inproj_prompt",
    )(x, sh, sc, nw, cos_t, sin_t, cos_tt, sin_tt, w_tok, w_trn)


def _retention_kernel(q_ref, kt_ref, v_ref, g_ref, s0_ref, dmat2_ref, qdec_ref, kdect_ref, cdec_ref, gnw_ref,
                      avg_ref, o_ref, sout_ref, s_scr, o_scr, *, heads, chunks):
    j = pl.program_id(1)
    pairs = heads // 2
    lane = lax.broadcasted_iota(jnp.int32, (LANES, LANES), 1)
    sub = lax.broadcasted_iota(jnp.int32, (LANES, LANES), 0)
    first = lane < HEAD_DIM
    diag = first == (sub < HEAD_DIM)
    keep_a = jnp.where(first, 1.0, 0.0).astype(BF16)
    keep_b = jnp.where(first, 0.0, 1.0).astype(BF16)

    @pl.when(j == 0)
    def _():
        z = jnp.zeros((HEAD_DIM, HEAD_DIM), F32)
        for p in range(pairs):
            s_scr[p] = jnp.concatenate([jnp.concatenate([s0_ref[2 * p], z], axis=1),
                                        jnp.concatenate([z, s0_ref[2 * p + 1]], axis=1)], axis=0)

    gnw = gnw_ref[...]
    avg = avg_ref[...]

    def group_mean(t):
        hi = t.astype(BF16)
        lo = (t - hi.astype(F32)).astype(BF16)
        return jnp.dot(hi, avg, preferred_element_type=F32) + jnp.dot(lo, avg, preferred_element_type=F32)

    for c in range(chunks):
        rows = slice(c * RET_CHUNK, (c + 1) * RET_CHUNK)
        for p in range(pairs):
            cs = slice(p * LANES, (p + 1) * LANES)
            qp = q_ref[rows, cs].astype(BF16)
            vp = v_ref[rows, cs].astype(BF16)
            ktp = kt_ref[cs, rows]
            q2 = jnp.concatenate([qp * keep_a, qp * keep_b], axis=0)
            att2 = jnp.dot(q2, ktp.astype(BF16), preferred_element_type=F32) * dmat2_ref[p]
            o2 = jnp.dot(att2.astype(BF16), vp, preferred_element_type=F32)
            s = s_scr[p]
            o = (jnp.where(first, o2[:RET_CHUNK], o2[RET_CHUNK:])
                 + jnp.dot(qp, s.astype(BF16), preferred_element_type=F32) * qdec_ref[:, cs])
            kdt = (ktp.astype(F32) * kdect_ref[cs, :]).astype(BF16)
            s_scr[p] = s * cdec_ref[p] + jnp.where(diag, jnp.dot(kdt, vp, preferred_element_type=F32), 0.0)
            o_scr[rows, cs] = o

    o_all = o_scr[...]
    d = o_all - group_mean(o_all)
    var = group_mean(d * d)
    o_ref[...] = (g_ref[...].astype(F32) * (d * lax.rsqrt(var + NORM_EPS) * gnw)).astype(o_ref.dtype)

    @pl.when(j == pl.num_programs(1) - 1)
    def _():
        for p in range(pairs):
            s = s_scr[p]
            sout_ref[2 * p] = s[:HEAD_DIM, :HEAD_DIM]
            sout_ref[2 * p + 1] = s[HEAD_DIM:, HEAD_DIM:]


def _retention(q, kt, v, gs, s0, tables, gnw, *, tl, out_dtype):
    B, L, W = q.shape
    heads = W // HEAD_DIM
    seq = pl.BlockSpec((None, tl, W), lambda b, j: (b, j, 0))
    seq_t = pl.BlockSpec((None, W, tl), lambda b, j: (b, 0, j))
    st = pl.BlockSpec((None, heads, HEAD_DIM, HEAD_DIM), lambda b, j: (b, 0, 0, 0))
    return pl.pallas_call(
        functools.partial(_retention_kernel, heads=heads, chunks=tl // RET_CHUNK),
        out_shape=[jax.ShapeDtypeStruct((B, L, W), out_dtype),
                   jax.ShapeDtypeStruct((B, heads, HEAD_DIM, HEAD_DIM), F32)],
        grid=(B, L // tl),
        in_specs=[seq, seq_t, seq, seq, st] + [_const_spec(t) for t in tables] + [_const_spec(gnw),
                                                                                   _const_spec(_head_average(W))],
        out_specs=[seq, st],
        scratch_shapes=[pltpu.VMEM((heads // 2, LANES, LANES), F32), pltpu.VMEM((tl, W), F32)],
        compiler_params=_params("parallel", "arbitrary"),
        name="retention",
    )(q, kt, v, gs, s0, *tables, gnw, _head_average(W))


def _head_average(width):
    r = jnp.arange(width) // HEAD_DIM
    return jnp.where(r[:, None] == r[None, :], 1.0 / HEAD_DIM, 0.0).astype(BF16)


def _decay_tables(heads, c_len, pad_len):
    log_g = jnp.log1p(-jnp.exp2(-5.0 - jnp.arange(heads, dtype=F32)))
    idx = jnp.arange(c_len, dtype=F32)
    diff = idx[:, None] - idx[None, :]
    dmat = jnp.where(diff >= 0, jnp.exp(jnp.maximum(diff, 0.0) * log_g[:, None, None]), 0.0)
    q_dec = jnp.exp((idx + 1.0) * log_g[:, None])
    k_dec = jnp.exp((c_len - 1.0 - idx) * log_g[:, None])
    c_dec = jnp.exp(c_len * log_g)
    pad = pad_len - c_len
    dmat2 = jnp.pad(dmat, ((0, 0), (0, pad), (0, pad))).reshape(heads // 2, 2 * pad_len, pad_len)
    qdec = jnp.pad(jnp.repeat(q_dec.T, HEAD_DIM, axis=1), ((0, pad), (0, 0)))
    kdect = jnp.pad(jnp.repeat(k_dec, HEAD_DIM, axis=0), ((0, 0), (0, pad)))
    cdec = jnp.broadcast_to(jnp.repeat(c_dec, HEAD_DIM).reshape(heads // 2, 2 * HEAD_DIM, 1),
                            (heads // 2, 2 * HEAD_DIM, 2 * HEAD_DIM))
    return dmat2, qdec, kdect, cdec


def _col_reduce(x, op, final):
    while x.shape[0] % (2 * SUBLANES) == 0:
        half = x.shape[0] // 2
        x = op(x[:half], x[half:])
    return final(x, axis=0, keepdims=True)


def _moba_prompt_kernel(qt_ref, k_ref, kt_ref, vt_ref, nwt_ref, o_ref, *, seq):
    nb = seq // MOBA_BLOCK
    nbp = -(-nb // SUBLANES) * SUBLANES
    qb = MOBA_BLOCK
    pair = 2 * HEAD_DIM
    if nb > MOBA_TOPK + 1:
        lane_n = lax.broadcasted_iota(jnp.int32, (pair, LANES), 1)
        kmt = jnp.zeros((pair, LANES), F32)
        for n in range(nb):
            tot = kt_ref[:, n * MOBA_BLOCK:n * MOBA_BLOCK + LANES]
            for c in range(1, MOBA_BLOCK // LANES):
                tot = tot + kt_ref[:, n * MOBA_BLOCK + c * LANES:n * MOBA_BLOCK + (c + 1) * LANES]
            kmt = jnp.where(lane_n == n, jnp.sum(tot, axis=1, keepdims=True) * (1.0 / MOBA_BLOCK), kmt)
        km = kmt.T[0:nbp]
        lane_k = lax.broadcasted_iota(jnp.int32, km.shape, 1)
        km_heads = [jnp.where(lane_k < HEAD_DIM, km, 0.0), jnp.where(lane_k < HEAD_DIM, 0.0, km)]
        blk = lax.broadcasted_iota(jnp.int32, (nbp, qb), 0)
    key_i = lax.broadcasted_iota(jnp.int32, (qb, qb), 0)
    qry_i = lax.broadcasted_iota(jnp.int32, (qb, qb), 1)
    causal = key_i <= qry_i
    zeros = jnp.zeros((HEAD_DIM, qb), BF16)
    ones = jnp.ones((2 * SUBLANES, seq), BF16)
    vt_ones = [jnp.concatenate([vt_ref[hh * HEAD_DIM:(hh + 1) * HEAD_DIM, :].astype(BF16), ones], axis=0)
               for hh in range(2)]

    def scores(i, hh):
        qt = qt_ref[:, i * qb:(i + 1) * qb]
        hr = slice(hh * HEAD_DIM, (hh + 1) * HEAD_DIM)
        qm = jnp.concatenate([qt[hr], zeros] if hh == 0 else [zeros, qt[hr]], axis=0)
        return jnp.dot(k_ref[0:(i + 1) * MOBA_BLOCK, :], qm, preferred_element_type=F32)

    def softmax(i, hh, st):
        biases = [None] * i
        if i > MOBA_TOPK:
            qt = qt_ref[:, i * qb:(i + 1) * qb].astype(F32)
            gate = jnp.dot(km_heads[hh], qt, preferred_element_type=F32, precision=HIGHEST)
            valid = blk < i
            for n in range(i):
                gn = gate[n:n + 1, :]
                ahead = valid & ((gate > gn) | ((gate == gn) & (blk < n)))
                rank = jnp.sum(ahead.astype(F32), axis=0, keepdims=True)
                biases[n] = jnp.where(rank < MOBA_TOPK, 0.0, NEG_INF)
        past = [st[n * MOBA_BLOCK:(n + 1) * MOBA_BLOCK] for n in range(i)]
        own = jnp.where(causal, st[i * MOBA_BLOCK:(i + 1) * MOBA_BLOCK], NEG_INF)
        m = _col_reduce(own, jnp.maximum, jnp.max)
        for sb, bias in zip(past, biases):
            mb = _col_reduce(sb, jnp.maximum, jnp.max)
            m = jnp.maximum(m, mb if bias is None else mb + bias)
        pieces = [jnp.exp2(sb + (-m if bias is None else bias - m)).astype(BF16) for sb, bias in zip(past, biases)]
        pieces.append(jnp.exp2(own - m).astype(BF16))
        return jnp.concatenate(pieces, axis=0) if i > 0 else pieces[0]

    def output(i, hh, p):
        hr = slice(hh * HEAD_DIM, (hh + 1) * HEAD_DIM)
        ot = jnp.dot(vt_ones[hh][:, 0:(i + 1) * MOBA_BLOCK], p, preferred_element_type=F32)
        ot = ot[0:HEAD_DIM] / ot[HEAD_DIM:HEAD_DIM + 1]
        ms = jnp.mean(ot * ot, axis=0, keepdims=True)
        o_ref[hr, i * qb:(i + 1) * qb] = (ot * lax.rsqrt(ms + NORM_EPS) * nwt_ref[hr, :]).astype(o_ref.dtype)

    bodies = [(i, hh) for i in range(nb) for hh in range(2)]
    st_next = scores(*bodies[0])
    pending = None
    for idx, body in enumerate(bodies):
        st = st_next
        if idx + 1 < len(bodies):
            st_next = scores(*bodies[idx + 1])
        p = softmax(*body, st)
        if pending is not None:
            output(*pending)
        pending = (*body, p)
    output(*pending)


def _moba_prompt(qt, k, kt, vt, nwt, *, out_dtype):
    B, W, S = qt.shape
    pair = 2 * HEAD_DIM
    trn = pl.BlockSpec((None, pair, S), lambda b, h: (b, h, 0))
    return pl.pallas_call(
        functools.partial(_moba_prompt_kernel, seq=S),
        out_shape=jax.ShapeDtypeStruct((B, W, S), out_dtype),
        grid=(B, W // pair),
        in_specs=[trn, pl.BlockSpec((None, S, pair), lambda b, h: (b, 0, h)), trn, trn,
                  pl.BlockSpec((pair, MOBA_BLOCK), lambda b, h: (h, 0))],
        out_specs=trn,
        compiler_params=_params("parallel", "parallel"),
        name="moba_prompt",
    )(qt, k, kt, vt, nwt)


def _topk_kernel(q_ref, km_ref, o_ref, *, heads):
    n_blk = km_ref.shape[1]
    lane_b = lax.broadcasted_iota(jnp.int32, (q_ref.shape[0], n_blk), 1).astype(F32)
    lane_o = lax.broadcasted_iota(jnp.int32, o_ref.shape, 1)
    out = jnp.zeros(o_ref.shape, F32)
    for h in range(heads):
        hs = slice(h * HEAD_DIM, (h + 1) * HEAD_DIM)
        gate = jnp.dot(q_ref[:, hs], km_ref[hs, :], preferred_element_type=F32, precision=HIGHEST)
        for r in range(MOBA_TOPK):
            m = jnp.max(gate, axis=1, keepdims=True)
            idx = jnp.min(jnp.where(gate == m, lane_b, float(n_blk)), axis=1, keepdims=True)
            out = jnp.where(lane_o == h * MOBA_TOPK + r, idx, out)
            gate = jnp.where(lane_b == idx, NEG_INF, gate)
    o_ref[...] = out.astype(jnp.int32)


def _topk(q8, kmt):
    bd, rows, W = q8.shape
    n_full = kmt.shape[2]
    return pl.pallas_call(
        functools.partial(_topk_kernel, heads=W // HEAD_DIM),
        out_shape=jax.ShapeDtypeStruct((bd, rows, LANES), jnp.int32),
        grid=(bd,),
        in_specs=[pl.BlockSpec((None, rows, W), lambda b: (b, 0, 0)),
                  pl.BlockSpec((None, W, n_full), lambda b: (b, 0, 0))],
        out_specs=pl.BlockSpec((None, rows, LANES), lambda b: (b, 0, 0)),
        compiler_params=_params("parallel"),
        name="moba_topk",
    )(q8, kmt)


def _attend_kernel(top_ref, pt_ref, q_ref, knt_ref, vnt_ref, nw_ref, ck_ref, cv_ref, o_ref,
                   kbuf, vbuf, sem, *, heads, tokens, n_pages, page):
    b = pl.program_id(0)
    nb = pl.num_programs(0)
    ppb = MOBA_BLOCK // page
    n_sel = MOBA_TOPK * MOBA_BLOCK
    slot = b % 2

    def page_copies(seq, buf_slot):
        out = []
        for h in range(heads):
            for t in range(tokens):
                for r in range(MOBA_TOPK):
                    blk = top_ref[((seq * tokens + t) * heads + h) * MOBA_TOPK + r]
                    for p in range(ppb):
                        phys = pt_ref[seq * n_pages + blk * ppb + p]
                        dst = pl.ds((r * ppb + p) * page, page)
                        out.append(pltpu.make_async_copy(ck_ref.at[phys, h], kbuf.at[buf_slot, h, t, :, dst],
                                                         sem.at[buf_slot, 0]))
                        out.append(pltpu.make_async_copy(cv_ref.at[phys, h], vbuf.at[buf_slot, h, t, :, dst],
                                                         sem.at[buf_slot, 1]))
        return out

    @pl.when(b == 0)
    def _():
        for c in page_copies(b, slot):
            c.start()

    @pl.when(b + 1 < nb)
    def _():
        for c in page_copies(b + 1, 1 - slot):
            c.start()

    for c in page_copies(b, slot):
        c.wait()
    kb = kbuf.at[slot]
    vb = vbuf.at[slot]
    for h in range(heads):
        for t in range(tokens):
            kb[h, t, :, n_sel:n_sel + LANES] = knt_ref[h]
            vb[h, t, :, n_sel:n_sel + LANES] = vnt_ref[h]

    rows = q_ref.shape[0]
    col = lax.broadcasted_iota(jnp.int32, (rows, n_sel + LANES), 1)
    row = lax.broadcasted_iota(jnp.int32, (rows, HEAD_DIM), 0)
    pairs = [(h, t) for h in range(heads) for t in range(tokens)]
    qs = [q_ref[:, h * HEAD_DIM:(h + 1) * HEAD_DIM].astype(BF16) for h in range(heads)]
    scores = [jnp.where(col <= n_sel + t,
                        jnp.dot(qs[h], kb[h, t].astype(BF16), preferred_element_type=F32), NEG_INF)
              for h, t in pairs]
    probs = []
    for s in scores:
        p = jnp.exp2(s - jnp.max(s, axis=1, keepdims=True))
        probs.append((p.astype(BF16), jnp.sum(p, axis=1, keepdims=True)))
    outs = [jnp.zeros((rows, HEAD_DIM), F32)] * heads
    for (h, t), (p, l) in zip(pairs, probs):
        o = lax.dot_general(p, vb[h, t].astype(BF16), NT, preferred_element_type=F32) / l
        outs[h] = jnp.where(row == t, o, outs[h])
    o_ref[...] = jnp.concatenate([_rms(outs[h], nw_ref[:, h * HEAD_DIM:(h + 1) * HEAD_DIM])
                                  for h in range(heads)], axis=1)


def _attend(top_flat, pt_flat, q8, knt, vnt, nw, ckt4, cvt4, *, tokens, n_pages):
    bd, rows, W = q8.shape
    heads = W // HEAD_DIM
    page = ckt4.shape[3]
    n_keys = MOBA_TOPK * MOBA_BLOCK + LANES
    any_spec = pl.BlockSpec(memory_space=pl.ANY)
    return pl.pallas_call(
        functools.partial(_attend_kernel, heads=heads, tokens=tokens, n_pages=n_pages, page=page),
        out_shape=jax.ShapeDtypeStruct((bd, rows, W), F32),
        grid_spec=pltpu.PrefetchScalarGridSpec(
            num_scalar_prefetch=2,
            grid=(bd,),
            in_specs=[pl.BlockSpec((None, rows, W), lambda b, *_: (b, 0, 0)),
                      pl.BlockSpec((None, heads, HEAD_DIM, LANES), lambda b, *_: (b, 0, 0, 0)),
                      pl.BlockSpec((None, heads, HEAD_DIM, LANES), lambda b, *_: (b, 0, 0, 0)),
                      pl.BlockSpec((1, W), lambda b, *_: (0, 0)),
                      any_spec, any_spec],
            out_specs=pl.BlockSpec((None, rows, W), lambda b, *_: (b, 0, 0)),
            scratch_shapes=[pltpu.VMEM((2, heads, tokens, HEAD_DIM, n_keys), F32),
                            pltpu.VMEM((2, heads, tokens, HEAD_DIM, n_keys), F32),
                            pltpu.SemaphoreType.DMA((2, 2))],
        ),
        compiler_params=_params("arbitrary"),
        name="moba_attend",
    )(top_flat, pt_flat, q8, knt, vnt, nw, ckt4, cvt4)


FF_CHUNKS = 2
SIDE_SLOTS = FF_CHUNKS + 2
POOL_SLOTS = 3


def _outffn_compute(x_ref, ro_ref, mo_ref, g1_ref, sh2_ref, sc2_ref, g2_ref, n2w_ref, fw_ref,
                    wo_ref, wg_ref, wu_ref, wd_ref, y_ref, *, ff_chunks, moba_transposed, side_work):
    side_work(0)
    rw = ro_ref.shape[1]
    mo = mo_ref[...].astype(BF16)
    attn = (jnp.dot(ro_ref[...].astype(BF16), wo_ref[0:rw, :], preferred_element_type=F32)
            + lax.dot_general(mo, wo_ref[rw:, :], TN if moba_transposed else (((1,), (0,)), ((), ())),
                              preferred_element_type=F32))
    x1 = x_ref[...] + g1_ref[...] * attn
    h2 = (_rms(x1, n2w_ref[...]) * (1.0 + sc2_ref[...]) + sh2_ref[...]).astype(BF16)
    acc = None
    for c, (lo, hi) in enumerate(ff_chunks):
        side_work(1 + c)
        gate = jnp.dot(h2, wg_ref[:, lo:hi], preferred_element_type=F32)
        up = jnp.dot(h2, wu_ref[:, lo:hi], preferred_element_type=F32)
        part = jnp.dot((_silu(gate) * up).astype(BF16), wd_ref[lo:hi, :], preferred_element_type=F32)
        acc = part if acc is None else acc + part
    side_work(1 + len(ff_chunks))
    x2 = x1 + g2_ref[...] * acc
    y_ref[...] = _rms(x2, fw_ref[...])


def _outffn_kernel(*refs, ff_chunks, moba_transposed):
    _outffn_compute(*refs, ff_chunks=ff_chunks, moba_transposed=moba_transposed, side_work=lambda c: None)


def _outffn_pool_kernel(pt_ref, *refs, ff_chunks, moba_transposed, inner_steps, pages_per_step):
    ck_ref, y_ref, km_ref, pbuf, sem = refs[-5:]
    step = pl.program_id(0) * inner_steps + pl.program_id(1)
    n_steps = pl.num_programs(0) * inner_steps
    n_pages = pt_ref.shape[1]
    steps_per_seq = n_pages // pages_per_step
    chunk_pages = pages_per_step // SIDE_SLOTS
    ppb = MOBA_BLOCK // pbuf.shape[3]
    chunk_blocks = chunk_pages // ppb
    lane = lax.broadcasted_iota(jnp.int32, km_ref.shape, 1)

    def chunk_slot(st, c):
        return lax.rem(st * SIDE_SLOTS + c, POOL_SLOTS)

    def chunk_copies(st, c):
        seq = st // steps_per_seq
        base = (st % steps_per_seq) * pages_per_step + c * chunk_pages
        slot = chunk_slot(st, c)
        return [pltpu.make_async_copy(ck_ref.at[pt_ref[seq, base + p]], pbuf.at[slot, p], sem.at[slot])
                for p in range(chunk_pages)]

    def side_work(c):
        if c == 0:
            @pl.when(step == 0)
            def _():
                for a in range(POOL_SLOTS - 1):
                    for cp in chunk_copies(step, a):
                        cp.start()

            @pl.when(step % steps_per_seq == 0)
            def _():
                km_ref[...] = jnp.zeros_like(km_ref)
        ahead = c + POOL_SLOTS - 1
        if ahead < SIDE_SLOTS:
            for cp in chunk_copies(step, ahead):
                cp.start()
        else:
            @pl.when(step + 1 < n_steps)
            def _():
                for cp in chunk_copies(step + 1, ahead - SIDE_SLOTS):
                    cp.start()
        for cp in chunk_copies(step, c):
            cp.wait()
        slot = chunk_slot(step, c)
        first_block = (step % steps_per_seq) * (pages_per_step // ppb) + c * chunk_blocks
        acc = km_ref[...]
        for n in range(chunk_blocks):
            tot = pbuf[slot, n * ppb]
            for p in range(1, ppb):
                tot = tot + pbuf[slot, n * ppb + p]
            col = jnp.sum(tot, axis=1, keepdims=True) * (1.0 / MOBA_BLOCK)
            acc = jnp.where(lane == first_block + n, col, acc)
        km_ref[...] = acc

    _outffn_compute(*refs[:-5], y_ref, ff_chunks=ff_chunks, moba_transposed=moba_transposed, side_work=side_work)


def _outffn(x, ro, mo, g1, sh2, sc2, g2, n2w, fw, wo, wg, wu, wd, *, tm, moba_transposed, pool=None):
    G, R, D = x.shape
    W = ro.shape[2]
    dff = wg.shape[1]
    cuts = [round(dff * c / FF_CHUNKS / 256) * 256 for c in range(FF_CHUNKS)] + [dff]
    ff_chunks = tuple(zip(cuts[:-1], cuts[1:]))
    inner = R // tm
    row = pl.BlockSpec((None, tm, D), lambda g, i, *_: (g, i, 0))
    act = pl.BlockSpec((None, tm, W), lambda g, i, *_: (g, i, 0))
    act_t = pl.BlockSpec((None, W, tm), lambda g, i, *_: (g, 0, i))
    in_specs = [row, act, act_t if moba_transposed else act,
                _mod_spec(g1, tm), _mod_spec(sh2, tm), _mod_spec(sc2, tm), _mod_spec(g2, tm),
                _const_spec(n2w), _const_spec(fw), _const_spec(wo), _const_spec(wg), _const_spec(wu),
                _const_spec(wd)]
    args = (x, ro, mo, g1, sh2, sc2, g2, n2w, fw, wo, wg, wu, wd)
    y_shape = jax.ShapeDtypeStruct((G, R, D), F32)
    if pool is None:
        return pl.pallas_call(
            functools.partial(_outffn_kernel, ff_chunks=ff_chunks, moba_transposed=moba_transposed),
            out_shape=y_shape, grid=(G, inner), in_specs=in_specs, out_specs=row,
            compiler_params=_params("parallel", "parallel"),
            name="outproj_ffn",
        )(*args)
    page_table, ckt = pool
    bd, n_pages = page_table.shape
    _, hd, page = ckt.shape
    ppb = MOBA_BLOCK // page
    n_steps = G * inner
    pages_per_step = bd * n_pages // n_steps
    assert pages_per_step * n_steps == bd * n_pages and n_pages % pages_per_step == 0
    assert pages_per_step % (SIDE_SLOTS * ppb) == 0 and POOL_SLOTS - 1 <= SIDE_SLOTS
    steps_per_seq = n_pages // pages_per_step
    n_blocks = n_pages // ppb
    km_spec = pl.BlockSpec((None, hd, n_blocks), lambda g, i, *_: ((g * inner + i) // steps_per_seq, 0, 0))
    return pl.pallas_call(
        functools.partial(_outffn_pool_kernel, ff_chunks=ff_chunks, moba_transposed=moba_transposed,
                          inner_steps=inner, pages_per_step=pages_per_step),
        out_shape=[y_shape, jax.ShapeDtypeStruct((bd, hd, n_blocks), F32)],
        grid_spec=pltpu.PrefetchScalarGridSpec(
            num_scalar_prefetch=1,
            grid=(G, inner),
            in_specs=in_specs + [pl.BlockSpec(memory_space=pl.ANY)],
            out_specs=[row, km_spec],
            scratch_shapes=[pltpu.VMEM((POOL_SLOTS, pages_per_step // SIDE_SLOTS, hd, page), F32),
                            pltpu.SemaphoreType.DMA((POOL_SLOTS,))],
        ),
        compiler_params=_params("arbitrary", "arbitrary"),
        name="outproj_ffn_pool",
    )(page_table, *args, ckt)


def _rope_angles(pos):
    inv = ROPE_BASE ** (-jnp.arange(HALF, dtype=F32) / HALF)
    ang = pos.astype(F32)[:, None] * inv[None, :]
    return jnp.cos(ang), jnp.sin(ang)


def _rope_tables(pos):
    cos, sin = _rope_angles(pos)
    reps = LANES // HEAD_DIM
    return jnp.tile(cos, (1, 2 * reps)), jnp.tile(jnp.concatenate([-sin, sin], axis=1), (1, reps))


def kernel(x_prompt, x_sample, cache_k, cache_v, state_ret, page_table, c_prompt, c_sample,
           norm1_w, norm2_w, final_w, w_mod, b_mod, w_in, ret_gn_w, moba_norm_w, w_out,
           w_gate, w_up, w_down):
    Bp, S, D = x_prompt.shape
    Bd, T, _ = x_sample.shape
    depth = w_in.shape[0]
    assert depth == 1, "single decoder layer"
    n_pool, page, m_heads = cache_k.shape[1], cache_k.shape[2], cache_k.shape[3]
    n_pages = page_table.shape[1]
    past_len = n_pages * page
    assert past_len % MOBA_BLOCK == 0 and MOBA_BLOCK % page == 0
    n_full = past_len // MOBA_BLOCK
    moba_w = m_heads * HEAD_DIM
    ret_w = (w_in.shape[2] - 3 * moba_w) // 4
    assert ret_w == moba_w, "the two head groups share one projection width"
    W = ret_w
    r_heads = ret_w // HEAD_DIM
    fw = final_w.reshape(1, D)
    rows_s = Bd * T
    pad8 = SUBLANES

    w_in0 = w_in[0].astype(BF16)
    col = lambda g: w_in0[:, g * W:(g + 1) * W]
    w_tok = jnp.concatenate([col(0), col(2), col(3)], axis=1)
    w_trn = jnp.concatenate([col(1), col(4), col(5), col(6)], axis=1).T
    wo, wg, wu, wd = (w[0].astype(BF16) for w in (w_out, w_gate, w_up, w_down))
    n1w, n2w = norm1_w[0].reshape(1, D), norm2_w[0].reshape(1, D)
    gnw, mnw = ret_gn_w[0].reshape(1, ret_w), moba_norm_w[0].reshape(1, moba_w)

    mod = _modulation(jnp.concatenate([c_prompt, c_sample], axis=0), w_mod[0], b_mod[0])
    mod_p = [m[:, None, :] for m in jnp.split(mod[:Bp], 6, axis=-1)]
    mod_s = [jnp.repeat(m, T, axis=0)[None] for m in jnp.split(mod[Bp:], 6, axis=-1)]

    cos_a, sin_a = _rope_angles(jnp.arange(S))
    cos_p, sin_p = _rope_tables(jnp.arange(S))
    rq, rkt, rv, rg, mqt, mk, mkt, mvt = _inproj_prompt(x_prompt, mod_p[0], mod_p[1], n1w, cos_p, sin_p,
                                                        cos_a.T, sin_a.T, w_tok, w_trn, tm=512)
    s0_p = jnp.zeros((Bp, r_heads, HEAD_DIM, HEAD_DIM), F32)
    ret_o, ret_p = _retention(rq, rkt, rv, rg, s0_p, _decay_tables(r_heads, RET_CHUNK, RET_CHUNK), gnw,
                              tl=512, out_dtype=BF16)
    nwt = jnp.broadcast_to(mnw.reshape(moba_w, 1), (moba_w, MOBA_BLOCK))
    moba_ot = _moba_prompt(mqt, mk, mkt, mvt, nwt, out_dtype=BF16)
    ckt4 = jnp.transpose(cache_k[0], (0, 2, 3, 1))
    cvt4 = jnp.transpose(cache_v[0], (0, 2, 3, 1))
    y_prompt, kmt = _outffn(x_prompt, ret_o, moba_ot, mod_p[2], mod_p[3], mod_p[4], mod_p[5], n2w, fw,
                            wo, wg, wu, wd, tm=512, moba_transposed=True,
                            pool=(page_table, ckt4.reshape(n_pool, moba_w, page)))
    to_rows = lambda t: t.reshape(1, Bp, m_heads, HEAD_DIM, S).transpose(0, 1, 4, 2, 3)
    k_prompt, v_prompt = to_rows(mkt), to_rows(mvt)

    xs = x_sample.reshape(1, rows_s, D)
    cos_s, sin_s = _rope_tables(past_len + jnp.arange(rows_s) % T)
    sq, sk, sv, sg, smq, smk, smv = _inproj_sample(xs, mod_s[0], mod_s[1], n1w, cos_s, sin_s, w_in0)
    padc = lambda t: jnp.pad(t.reshape(Bd, T, ret_w), ((0, 0), (0, RET_CHUNK - T), (0, 0)))
    skt = jnp.pad(sk.reshape(Bd, T, ret_w).transpose(0, 2, 1), ((0, 0), (0, 0), (0, RET_CHUNK - T)))
    ret_os, ret_s = _retention(padc(sq), skt, padc(sv), padc(sg), state_ret[0],
                               _decay_tables(r_heads, T, RET_CHUNK), gnw, tl=RET_CHUNK, out_dtype=F32)
    ret_os = ret_os[:, :T].reshape(1, rows_s, ret_w)

    assert kmt.shape[2] == n_full
    q8 =jnp.pad(smq.reshape(Bd, T, moba_w), ((0, 0), (0, pad8 - T), (0, 0)))
    top = _topk(q8, kmt)[:, :T, :m_heads * MOBA_TOPK]
    new_t = lambda t: jnp.pad(t.reshape(Bd, T, m_heads, HEAD_DIM).transpose(0, 2, 3, 1),
                              ((0, 0), (0, 0), (0, 0), (0, LANES - T)))
    moba_os = _attend(top.reshape(-1), page_table.reshape(-1), q8, new_t(smk), new_t(smv), mnw, ckt4, cvt4,
                      tokens=T, n_pages=n_pages)
    moba_os = moba_os[:, :T].reshape(1, rows_s, moba_w)
    y_sample = _outffn(xs, ret_os, moba_os, mod_s[2], mod_s[3], mod_s[4], mod_s[5], n2w, fw,
                       wo, wg, wu, wd, tm=rows_s, moba_transposed=False).reshape(Bd, T, D)
    k_sample = smk.reshape(1, Bd, T, m_heads, HEAD_DIM)
    v_sample = smv.reshape(1, Bd, T, m_heads, HEAD_DIM)

    return (y_prompt, y_sample, k_prompt, v_prompt, ret_p[None], k_sample, v_sample, ret_s[None])
```

```python
import functools
import math

import jax
import jax.numpy as jnp
from jax import lax
from jax.experimental import pallas as pl
from jax.experimental.pallas import tpu as pltpu

F32 = jnp.float32
BF16 = jnp.bfloat16
HIGHEST = lax.Precision.HIGHEST

HEAD_DIM = 64
HALF = HEAD_DIM // 2
RET_CHUNK = 128
MOBA_BLOCK = 256
MOBA_TOPK = 3
ROPE_BASE = 10000.0
NORM_EPS = 1e-6
LANES = 128
SUBLANES = 8
VMEM_LIMIT = 56 * 1024 * 1024
NEG_INF = float("-inf")
MOBA_QSCALE = HEAD_DIM ** -0.5 * math.log2(math.e)
NT = (((1,), (1,)), ((), ()))
TN = (((0,), (0,)), ((), ()))


def _params(*sem):
    return pltpu.CompilerParams(dimension_semantics=sem, vmem_limit_bytes=VMEM_LIMIT)


def _rms(x, w):
    return x * lax.rsqrt(jnp.mean(x * x, axis=-1, keepdims=True) + NORM_EPS) * w


def _silu(x):
    return x * jax.nn.sigmoid(x)


def _const_spec(a):
    return pl.BlockSpec(a.shape, lambda *_: (0,) * a.ndim, pipeline_mode=pl.Buffered(1))


def _mod_kernel(c_ref, w_ref, b_ref, o_ref):
    s = _silu(c_ref[...])
    o_ref[...] = jnp.dot(s, w_ref[...], preferred_element_type=F32, precision=HIGHEST) + b_ref[...]


def _modulation(c, w_mod, b_mod):
    n, d = c.shape
    cols = w_mod.shape[1]
    tn = 1536
    return pl.pallas_call(
        _mod_kernel,
        out_shape=jax.ShapeDtypeStruct((n, cols), F32),
        grid=(cols // tn,),
        in_specs=[pl.BlockSpec((n, d), lambda j: (0, 0)),
                  pl.BlockSpec((d, tn), lambda j: (0, j)),
                  pl.BlockSpec((1, tn), lambda j: (0, j))],
        out_specs=pl.BlockSpec((n, tn), lambda j: (0, j)),
        compiler_params=_params("arbitrary"),
        name="modulation",
    )(c, w_mod, b_mod.reshape(1, cols))


def _mod_spec(arr, tm):
    if arr.shape[1] == 1:
        return pl.BlockSpec((None, 1, arr.shape[2]), lambda g, i, *_: (g, 0, 0))
    return pl.BlockSpec((None, tm, arr.shape[2]), lambda g, i, *_: (g, i, 0))


def _normed_input(x_ref, sh_ref, sc_ref, nw_ref):
    return (_rms(x_ref[...], nw_ref[...]) * (1.0 + sc_ref[...]) + sh_ref[...]).astype(BF16)


def _rope_store(z, cos, sin, o_ref, scale):
    lane = lax.broadcasted_iota(jnp.int32, cos.shape, 1)
    first_half = (lane % HEAD_DIM) < HALF
    for c in range(z.shape[1] // LANES):
        zc = z[:, c * LANES:(c + 1) * LANES]
        partner = jnp.where(first_half, pltpu.roll(zc, LANES - HALF, 1), pltpu.roll(zc, HALF, 1))
        o_ref[:, c * LANES:(c + 1) * LANES] = ((zc * cos + partner * sin) * scale).astype(o_ref.dtype)


def _inproj_sample_kernel(x_ref, sh_ref, sc_ref, nw_ref, cos_ref, sin_ref, w_ref,
                          rq_ref, rk_ref, rv_ref, rg_ref, mq_ref, mk_ref, mv_ref, *, width):
    h = _normed_input(x_ref, sh_ref, sc_ref, nw_ref)
    proj = lambda g: jnp.dot(h, w_ref[:, g * width:(g + 1) * width], preferred_element_type=F32)
    _rope_store(proj(0), cos_ref[...], sin_ref[...], rq_ref, 1.0)
    _rope_store(proj(1), cos_ref[...], sin_ref[...], rk_ref, HEAD_DIM ** -0.5)
    rv_ref[...] = proj(2)
    rg_ref[...] = _silu(proj(3))
    mq_ref[...] = proj(4) * MOBA_QSCALE
    mk_ref[...] = proj(5)
    mv_ref[...] = proj(6)


def _inproj_sample(x, sh, sc, nw, cos_t, sin_t, w_bf):
    G, R, D = x.shape
    width = w_bf.shape[1] // 7
    row = pl.BlockSpec((None, R, D), lambda g, i: (g, i, 0))
    tab = pl.BlockSpec((R, LANES), lambda g, i: (i, 0))
    act = pl.BlockSpec((None, R, width), lambda g, i: (g, i, 0))
    return pl.pallas_call(
        functools.partial(_inproj_sample_kernel, width=width),
        out_shape=[jax.ShapeDtypeStruct((G, R, width), F32)] * 7,
        grid=(G, 1),
        in_specs=[row, _mod_spec(sh, R), _mod_spec(sc, R), _const_spec(nw), tab, tab, _const_spec(w_bf)],
        out_specs=[act] * 7,
        compiler_params=_params("parallel", "parallel"),
        name="inproj_sample",
    )(x, sh, sc, nw, cos_t, sin_t, w_bf)


def _inproj_prompt_kernel(x_ref, sh_ref, sc_ref, nw_ref, cos_ref, sin_ref, cost_ref, sint_ref, w_ref, wt_ref,
                          rq_ref, rkt_ref, rv_ref, rg_ref, mqt_ref, mk_ref, mkt_ref, mvt_ref, *, width):
    h = _normed_input(x_ref, sh_ref, sc_ref, nw_ref)
    proj = lambda g: jnp.dot(h, w_ref[:, g * width:(g + 1) * width], preferred_element_type=F32)
    proj_t = lambda g: lax.dot_general(wt_ref[g * width:(g + 1) * width, :], h, NT, preferred_element_type=F32)
    _rope_store(proj(0), cos_ref[...], sin_ref[...], rq_ref, 1.0)
    rv_ref[...] = proj(1).astype(rv_ref.dtype)
    rg_ref[...] = _silu(proj(2)).astype(rg_ref.dtype)
    zt = proj_t(0)
    cost = cost_ref[...]
    sint = sint_ref[...]
    scale = HEAD_DIM ** -0.5
    for hd in range(width // HEAD_DIM):
        lo = slice(hd * HEAD_DIM, hd * HEAD_DIM + HALF)
        hi = slice(hd * HEAD_DIM + HALF, (hd + 1) * HEAD_DIM)
        a, b = zt[lo], zt[hi]
        rkt_ref[lo, :] = ((a * cost - b * sint) * scale).astype(rkt_ref.dtype)
        rkt_ref[hi, :] = ((a * sint + b * cost) * scale).astype(rkt_ref.dtype)
    mqt_ref[...] = (proj_t(1) * MOBA_QSCALE).astype(mqt_ref.dtype)
    mkt = proj_t(2)
    mkt_ref[...] = mkt
    mk_ref[...] = mkt.T.astype(mk_ref.dtype)
    mvt_ref[...] = proj_t(3)


def _inproj_prompt(x, sh, sc, nw, cos_t, sin_t, cos_tt, sin_tt, w_tok, w_trn, *, tm):
    G, R, D = x.shape
    width = w_tok.shape[1] // 3
    row = pl.BlockSpec((None, tm, D), lambda g, i: (g, i, 0))
    tab = pl.BlockSpec((tm, LANES), lambda g, i: (i, 0))
    tab_t = pl.BlockSpec((HALF, tm), lambda g, i: (0, i))
    act = pl.BlockSpec((None, tm, width), lambda g, i: (g, i, 0))
    act_t = pl.BlockSpec((None, width, tm), lambda g, i: (g, 0, i))
    tok = jax.ShapeDtypeStruct((G, R, width), BF16)
    trn = lambda dt: jax.ShapeDtypeStruct((G, width, R), dt)
    return pl.pallas_call(
        functools.partial(_inproj_prompt_kernel, width=width),
        out_shape=[tok, trn(BF16), tok, tok, trn(BF16), tok, trn(F32), trn(F32)],
        grid=(G, R // tm),
        in_specs=[row, _mod_spec(sh, tm), _mod_spec(sc, tm), _const_spec(nw), tab, tab, tab_t, tab_t,
                  _const_spec(w_tok), _const_spec(w_trn)],
        out_specs=[act, act_t, act, act, act_t, act, act_t, act_t],
        compiler_params=_params("parallel", "parallel"),
        name="inproj_prompt",
    )(x, sh, sc, nw, cos_t, sin_t, cos_tt, sin_tt, w_tok, w_trn)


def _retention_kernel(q_ref, kt_ref, v_ref, g_ref, s0_ref, dmat2_ref, qdec_ref, kdect_ref, cdec_ref, gnw_ref,
                      avg_ref, o_ref, sout_ref, s_scr, o_scr, *, heads, chunks):
    j = pl.program_id(1)
    pairs = heads // 2
    lane = lax.broadcasted_iota(jnp.int32, (LANES, LANES), 1)
    sub = lax.broadcasted_iota(jnp.int32, (LANES, LANES), 0)
    first = lane < HEAD_DIM
    diag = first == (sub < HEAD_DIM)
    keep_a = jnp.where(first, 1.0, 0.0).astype(BF16)
    keep_b = jnp.where(first, 0.0, 1.0).astype(BF16)

    @pl.when(j == 0)
    def _():
        z = jnp.zeros((HEAD_DIM, HEAD_DIM), F32)
        for p in range(pairs):
            s_scr[p] = jnp.concatenate([jnp.concatenate([s0_ref[2 * p], z], axis=1),
                                        jnp.concatenate([z, s0_ref[2 * p + 1]], axis=1)], axis=0)

    gnw = gnw_ref[...]
    avg = avg_ref[...]

    def group_mean(t):
        hi = t.astype(BF16)
        lo = (t - hi.astype(F32)).astype(BF16)
        return jnp.dot(hi, avg, preferred_element_type=F32) + jnp.dot(lo, avg, preferred_element_type=F32)

    for c in range(chunks):
        rows = slice(c * RET_CHUNK, (c + 1) * RET_CHUNK)
        for p in range(pairs):
            cs = slice(p * LANES, (p + 1) * LANES)
            qp = q_ref[rows, cs].astype(BF16)
            vp = v_ref[rows, cs].astype(BF16)
            ktp = kt_ref[cs, rows]
            q2 = jnp.concatenate([qp * keep_a, qp * keep_b], axis=0)
            att2 = jnp.dot(q2, ktp.astype(BF16), preferred_element_type=F32) * dmat2_ref[p]
            o2 = jnp.dot(att2.astype(BF16), vp, preferred_element_type=F32)
            s = s_scr[p]
            o = (jnp.where(first, o2[:RET_CHUNK], o2[RET_CHUNK:])
                 + jnp.dot(qp, s.astype(BF16), preferred_element_type=F32) * qdec_ref[:, cs])
            kdt = (ktp.astype(F32) * kdect_ref[cs, :]).astype(BF16)
            s_scr[p] = s * cdec_ref[p] + jnp.where(diag, jnp.dot(kdt, vp, preferred_element_type=F32), 0.0)
            o_scr[rows, cs] = o

    o_all = o_scr[...]
    d = o_all - group_mean(o_all)
    var = group_mean(d * d)
    o_ref[...] = (g_ref[...].astype(F32) * (d * lax.rsqrt(var + NORM_EPS) * gnw)).astype(o_ref.dtype)

    @pl.when(j == pl.num_programs(1) - 1)
    def _():
        for p in range(pairs):
            s = s_scr[p]
            sout_ref[2 * p] = s[:HEAD_DIM, :HEAD_DIM]
            sout_ref[2 * p + 1] = s[HEAD_DIM:, HEAD_DIM:]


def _retention(q, kt, v, gs, s0, tables, gnw, *, tl, out_dtype):
    B, L, W = q.shape
    heads = W // HEAD_DIM
    seq = pl.BlockSpec((None, tl, W), lambda b, j: (b, j, 0))
    seq_t = pl.BlockSpec((None, W, tl), lambda b, j: (b, 0, j))
    st = pl.BlockSpec((None, heads, HEAD_DIM, HEAD_DIM), lambda b, j: (b, 0, 0, 0))
    return pl.pallas_call(
        functools.partial(_retention_kernel, heads=heads, chunks=tl // RET_CHUNK),
        out_shape=[jax.ShapeDtypeStruct((B, L, W), out_dtype),
                   jax.ShapeDtypeStruct((B, heads, HEAD_DIM, HEAD_DIM), F32)],
        grid=(B, L // tl),
        in_specs=[seq, seq_t, seq, seq, st] + [_const_spec(t) for t in tables] + [_const_spec(gnw),
                                                                                   _const_spec(_head_average(W))],
        out_specs=[seq, st],
        scratch_shapes=[pltpu.VMEM((heads // 2, LANES, LANES), F32), pltpu.VMEM((tl, W), F32)],
        compiler_params=_params("parallel", "arbitrary"),
        name="retention",
    )(q, kt, v, gs, s0, *tables, gnw, _head_average(W))


def _head_average(width):
    r = jnp.arange(width) // HEAD_DIM
    return jnp.where(r[:, None] == r[None, :], 1.0 / HEAD_DIM, 0.0).astype(BF16)


def _decay_tables(heads, c_len, pad_len):
    log_g = jnp.log1p(-jnp.exp2(-5.0 - jnp.arange(heads, dtype=F32)))
    idx = jnp.arange(c_len, dtype=F32)
    diff = idx[:, None] - idx[None, :]
    dmat = jnp.where(diff >= 0, jnp.exp(jnp.maximum(diff, 0.0) * log_g[:, None, None]), 0.0)
    q_dec = jnp.exp((idx + 1.0) * log_g[:, None])
    k_dec = jnp.exp((c_len - 1.0 - idx) * log_g[:, None])
    c_dec = jnp.exp(c_len * log_g)
    pad = pad_len - c_len
    dmat2 = jnp.pad(dmat, ((0, 0), (0, pad), (0, pad))).reshape(heads // 2, 2 * pad_len, pad_len)
    qdec = jnp.pad(jnp.repeat(q_dec.T, HEAD_DIM, axis=1), ((0, pad), (0, 0)))
    kdect = jnp.pad(jnp.repeat(k_dec, HEAD_DIM, axis=0), ((0, 0), (0, pad)))
    cdec = jnp.broadcast_to(jnp.repeat(c_dec, HEAD_DIM).reshape(heads // 2, 2 * HEAD_DIM, 1),
                            (heads // 2, 2 * HEAD_DIM, 2 * HEAD_DIM))
    return dmat2, qdec, kdect, cdec


def _col_reduce(x, op, final):
    while x.shape[0] % (2 * SUBLANES) == 0:
        half = x.shape[0] // 2
        x = op(x[:half], x[half:])
    return final(x, axis=0, keepdims=True)


def _moba_prompt_kernel(qt_ref, k_ref, kt_ref, vt_ref, nwt_ref, o_ref, *, seq):
    nb = seq // MOBA_BLOCK
    nbp = -(-nb // SUBLANES) * SUBLANES
    qb = MOBA_BLOCK
    pair = 2 * HEAD_DIM
    if nb > MOBA_TOPK + 1:
        lane_n = lax.broadcasted_iota(jnp.int32, (pair, LANES), 1)
        kmt = jnp.zeros((pair, LANES), F32)
        for n in range(nb):
            tot = kt_ref[:, n * MOBA_BLOCK:n * MOBA_BLOCK + LANES]
            for c in range(1, MOBA_BLOCK // LANES):
                tot = tot + kt_ref[:, n * MOBA_BLOCK + c * LANES:n * MOBA_BLOCK + (c + 1) * LANES]
            kmt = jnp.where(lane_n == n, jnp.sum(tot, axis=1, keepdims=True) * (1.0 / MOBA_BLOCK), kmt)
        km = kmt.T[0:nbp]
        lane_k = lax.broadcasted_iota(jnp.int32, km.shape, 1)
        km_heads = [jnp.where(lane_k < HEAD_DIM, km, 0.0), jnp.where(lane_k < HEAD_DIM, 0.0, km)]
        blk = lax.broadcasted_iota(jnp.int32, (nbp, qb), 0)
    key_i = lax.broadcasted_iota(jnp.int32, (qb, qb), 0)
    qry_i = lax.broadcasted_iota(jnp.int32, (qb, qb), 1)
    causal = key_i <= qry_i
    zeros = jnp.zeros((HEAD_DIM, qb), BF16)
    ones = jnp.ones((2 * SUBLANES, seq), BF16)
    vt_ones = [jnp.concatenate([vt_ref[hh * HEAD_DIM:(hh + 1) * HEAD_DIM, :].astype(BF16), ones], axis=0)
               for hh in range(2)]

    def scores(i, hh):
        qt = qt_ref[:, i * qb:(i + 1) * qb]
        hr = slice(hh * HEAD_DIM, (hh + 1) * HEAD_DIM)
        qm = jnp.concatenate([qt[hr], zeros] if hh == 0 else [zeros, qt[hr]], axis=0)
        return jnp.dot(k_ref[0:(i + 1) * MOBA_BLOCK, :], qm, preferred_element_type=F32)

    def softmax(i, hh, st):
        biases = [None] * i
        if i > MOBA_TOPK:
            qt = qt_ref[:, i * qb:(i + 1) * qb].astype(F32)
            gate = jnp.dot(km_heads[hh], qt, preferred_element_type=F32, precision=HIGHEST)
            valid = blk < i
            for n in range(i):
                gn = gate[n:n + 1, :]
                ahead = valid & ((gate > gn) | ((gate == gn) & (blk < n)))
                rank = jnp.sum(ahead.astype(F32), axis=0, keepdims=True)
                biases[n] = jnp.where(rank < MOBA_TOPK, 0.0, NEG_INF)
        past = [st[n * MOBA_BLOCK:(n + 1) * MOBA_BLOCK] for n in range(i)]
        own = jnp.where(causal, st[i * MOBA_BLOCK:(i + 1) * MOBA_BLOCK], NEG_INF)
        m = _col_reduce(own, jnp.maximum, jnp.max)
        for sb, bias in zip(past, biases):
            mb = _col_reduce(sb, jnp.maximum, jnp.max)
            m = jnp.maximum(m, mb if bias is None else mb + bias)
        pieces = [jnp.exp2(sb + (-m if bias is None else bias - m)).astype(BF16) for sb, bias in zip(past, biases)]
        pieces.append(jnp.exp2(own - m).astype(BF16))
        return jnp.concatenate(pieces, axis=0) if i > 0 else pieces[0]

    def output(i, hh, p):
        hr = slice(hh * HEAD_DIM, (hh + 1) * HEAD_DIM)
        ot = jnp.dot(vt_ones[hh][:, 0:(i + 1) * MOBA_BLOCK], p, preferred_element_type=F32)
        ot = ot[0:HEAD_DIM] / ot[HEAD_DIM:HEAD_DIM + 1]
        ms = jnp.mean(ot * ot, axis=0, keepdims=True)
        o_ref[hr, i * qb:(i + 1) * qb] = (ot * lax.rsqrt(ms + NORM_EPS) * nwt_ref[hr, :]).astype(o_ref.dtype)

    bodies = [(i, hh) for i in range(nb) for hh in range(2)]
    st_next = scores(*bodies[0])
    pending = None
    for idx, body in enumerate(bodies):
        st = st_next
        if idx + 1 < len(bodies):
            st_next = scores(*bodies[idx + 1])
        p = softmax(*body, st)
        if pending is not None:
            output(*pending)
        pending = (*body, p)
    output(*pending)


def _moba_prompt(qt, k, kt, vt, nwt, *, out_dtype):
    B, W, S = qt.shape
    pair = 2 * HEAD_DIM
    trn = pl.BlockSpec((None, pair, S), lambda b, h: (b, h, 0))
    return pl.pallas_call(
        functools.partial(_moba_prompt_kernel, seq=S),
        out_shape=jax.ShapeDtypeStruct((B, W, S), out_dtype),
        grid=(B, W // pair),
        in_specs=[trn, pl.BlockSpec((None, S, pair), lambda b, h: (b, 0, h)), trn, trn,
                  pl.BlockSpec((pair, MOBA_BLOCK), lambda b, h: (h, 0))],
        out_specs=trn,
        compiler_params=_params("parallel", "parallel"),
        name="moba_prompt",
    )(qt, k, kt, vt, nwt)


def _topk_kernel(q_ref, km_ref, o_ref, *, heads):
    seqs, rows, _ = q_ref.shape
    n_blk = km_ref.shape[2]
    lane_b = lax.broadcasted_iota(jnp.int32, (seqs * heads * rows, n_blk), 1).astype(F32)
    lane_o = lax.broadcasted_iota(jnp.int32, (rows, LANES), 1)
    gate = jnp.concatenate(
        [jnp.dot(q_ref[s, :, h * HEAD_DIM:(h + 1) * HEAD_DIM], km_ref[s, h * HEAD_DIM:(h + 1) * HEAD_DIM, :],
                 preferred_element_type=F32, precision=HIGHEST) for s in range(seqs) for h in range(heads)], axis=0)
    picks = []
    for r in range(MOBA_TOPK):
        m = jnp.max(gate, axis=1, keepdims=True)
        idx = jnp.min(jnp.where(gate == m, lane_b, float(n_blk)), axis=1, keepdims=True)
        gate = jnp.where(lane_b == idx, NEG_INF, gate)
        picks.append(idx)
    for s in range(seqs):
        out = jnp.zeros((rows, LANES), F32)
        for h in range(heads):
            base = (s * heads + h) * rows
            for r in range(MOBA_TOPK):
                out = jnp.where(lane_o == h * MOBA_TOPK + r, picks[r][base:base + rows], out)
        o_ref[s] = out.astype(jnp.int32)


def _topk(q8, kmt):
    bd, rows, W = q8.shape
    n_full = kmt.shape[2]
    seqs = max(s for s in (8, 4, 2, 1) if bd % s == 0)
    return pl.pallas_call(
        functools.partial(_topk_kernel, heads=W // HEAD_DIM),
        out_shape=jax.ShapeDtypeStruct((bd, rows, LANES), jnp.int32),
        grid=(bd // seqs,),
        in_specs=[pl.BlockSpec((seqs, rows, W), lambda b: (b, 0, 0)),
                  pl.BlockSpec((seqs, W, n_full), lambda b: (b, 0, 0))],
        out_specs=pl.BlockSpec((seqs, rows, LANES), lambda b: (b, 0, 0)),
        compiler_params=_params("parallel"),
        name="moba_topk",
    )(q8, kmt)


def _attend_kernel(top_ref, pt_ref, q_ref, knt_ref, vnt_ref, nw_ref, ck_ref, cv_ref, o_ref,
                   kbuf, vbuf, sem, *, heads, tokens, n_pages, page):
    b = pl.program_id(0)
    nb = pl.num_programs(0)
    ppb = MOBA_BLOCK // page
    n_sel = MOBA_TOPK * MOBA_BLOCK
    slot = b % 2

    def page_copies(seq, buf_slot):
        out = []
        for h in range(heads):
            for t in range(tokens):
                for r in range(MOBA_TOPK):
                    blk = top_ref[((seq * tokens + t) * heads + h) * MOBA_TOPK + r]
                    for p in range(ppb):
                        phys = pt_ref[seq * n_pages + blk * ppb + p]
                        dst = pl.ds((r * ppb + p) * page, page)
                        out.append(pltpu.make_async_copy(ck_ref.at[phys, h], kbuf.at[buf_slot, h, t, :, dst],
                                                         sem.at[buf_slot, 0]))
                        out.append(pltpu.make_async_copy(cv_ref.at[phys, h], vbuf.at[buf_slot, h, t, :, dst],
                                                         sem.at[buf_slot, 1]))
        return out

    @pl.when(b == 0)
    def _():
        for c in page_copies(b, slot):
            c.start()
        tail = (2, heads, tokens, HEAD_DIM, LANES)
        kbuf[:, :, :, :, n_sel:n_sel + LANES] = jnp.zeros(tail, F32)
        vbuf[:, :, :, :, n_sel:n_sel + LANES] = jnp.zeros(tail, F32)

    @pl.when(b + 1 < nb)
    def _():
        for c in page_copies(b + 1, 1 - slot):
            c.start()

    for c in page_copies(b, slot):
        c.wait()
    kb = kbuf.at[slot]
    vb = vbuf.at[slot]
    for h in range(heads):
        for t in range(tokens):
            kb[h, t, :, n_sel:n_sel + knt_ref.shape[2]] = knt_ref[h]
            vb[h, t, :, n_sel:n_sel + vnt_ref.shape[2]] = vnt_ref[h]

    rows = q_ref.shape[0]
    col = lax.broadcasted_iota(jnp.int32, (rows, n_sel + LANES), 1)
    row = lax.broadcasted_iota(jnp.int32, (rows, HEAD_DIM), 0)
    pairs = [(h, t) for h in range(heads) for t in range(tokens)]
    qs = [q_ref[:, h * HEAD_DIM:(h + 1) * HEAD_DIM].astype(BF16) for h in range(heads)]
    scores = [jnp.where(col <= n_sel + t,
                        jnp.dot(qs[h], kb[h, t].astype(BF16), preferred_element_type=F32), NEG_INF)
              for h, t in pairs]
    probs = []
    for s in scores:
        p = jnp.exp2(s - jnp.max(s, axis=1, keepdims=True))
        probs.append((p.astype(BF16), jnp.sum(p, axis=1, keepdims=True)))
    outs = [jnp.zeros((rows, HEAD_DIM), F32)] * heads
    for (h, t), (p, l) in zip(pairs, probs):
        o = lax.dot_general(p, vb[h, t].astype(BF16), NT, preferred_element_type=F32) / l
        outs[h] = jnp.where(row == t, o, outs[h])
    o_ref[...] = jnp.concatenate([_rms(outs[h], nw_ref[:, h * HEAD_DIM:(h + 1) * HEAD_DIM])
                                  for h in range(heads)], axis=1)


def _attend(top_flat, pt_flat, q8, knt, vnt, nw, ckt4, cvt4, *, tokens, n_pages):
    bd, rows, W = q8.shape
    heads = W // HEAD_DIM
    page = ckt4.shape[3]
    n_keys = MOBA_TOPK * MOBA_BLOCK + LANES
    any_spec = pl.BlockSpec(memory_space=pl.ANY)
    return pl.pallas_call(
        functools.partial(_attend_kernel, heads=heads, tokens=tokens, n_pages=n_pages, page=page),
        out_shape=jax.ShapeDtypeStruct((bd, rows, W), F32),
        grid_spec=pltpu.PrefetchScalarGridSpec(
            num_scalar_prefetch=2,
            grid=(bd,),
            in_specs=[pl.BlockSpec((None, rows, W), lambda b, *_: (b, 0, 0)),
                      pl.BlockSpec((None,) + knt.shape[1:], lambda b, *_: (b, 0, 0, 0)),
                      pl.BlockSpec((None,) + vnt.shape[1:], lambda b, *_: (b, 0, 0, 0)),
                      pl.BlockSpec((1, W), lambda b, *_: (0, 0)),
                      any_spec, any_spec],
            out_specs=pl.BlockSpec((None, rows, W), lambda b, *_: (b, 0, 0)),
            scratch_shapes=[pltpu.VMEM((2, heads, tokens, HEAD_DIM, n_keys), F32),
                            pltpu.VMEM((2, heads, tokens, HEAD_DIM, n_keys), F32),
                            pltpu.SemaphoreType.DMA((2, 2))],
        ),
        compiler_params=_params("arbitrary"),
        name="moba_attend",
    )(top_flat, pt_flat, q8, knt, vnt, nw, ckt4, cvt4)


FF_CHUNKS = 3
SIDE_SLOTS = FF_CHUNKS + 1
POOL_SLOTS = 3


def _outffn_compute(x_ref, ro_ref, mo_ref, g1_ref, sh2_ref, sc2_ref, g2_ref, n2w_ref, fw_ref,
                    wo_ref, wg_ref, wu_ref, wd_ref, y_ref, *, ff_chunks, moba_transposed, side_work):
    side_work(0)
    rw = ro_ref.shape[1]
    mo = mo_ref[...].astype(BF16)
    attn = (jnp.dot(ro_ref[...].astype(BF16), wo_ref[0:rw, :], preferred_element_type=F32)
            + lax.dot_general(mo, wo_ref[rw:, :], TN if moba_transposed else (((1,), (0,)), ((), ())),
                              preferred_element_type=F32))
    x1 = x_ref[...] + g1_ref[...] * attn
    h2 = (_rms(x1, n2w_ref[...]) * (1.0 + sc2_ref[...]) + sh2_ref[...]).astype(BF16)
    acc = None
    for c, (lo, hi) in enumerate(ff_chunks):
        side_work(1 + c)
        gate = jnp.dot(h2, wg_ref[:, lo:hi], preferred_element_type=F32)
        up = jnp.dot(h2, wu_ref[:, lo:hi], preferred_element_type=F32)
        part = jnp.dot((_silu(gate) * up).astype(BF16), wd_ref[lo:hi, :], preferred_element_type=F32)
        acc = part if acc is None else acc + part
    x2 = x1 + g2_ref[...] * acc
    y_ref[...] = _rms(x2, fw_ref[...])


def _outffn_kernel(*refs, ff_chunks, moba_transposed):
    _outffn_compute(*refs, ff_chunks=ff_chunks, moba_transposed=moba_transposed, side_work=lambda c: None)


def _outffn_pool_kernel(pt_ref, *refs, ff_chunks, moba_transposed, inner_steps, pages_per_step):
    ck_ref, y_ref, km_ref, pbuf, sem = refs[-5:]
    step = pl.program_id(0) * inner_steps + pl.program_id(1)
    n_steps = pl.num_programs(0) * inner_steps
    n_pages = pt_ref.shape[1]
    steps_per_seq = n_pages // pages_per_step
    chunk_pages = pages_per_step // SIDE_SLOTS
    ppb = MOBA_BLOCK // pbuf.shape[3]
    chunk_blocks = chunk_pages // ppb
    lane = lax.broadcasted_iota(jnp.int32, km_ref.shape, 1)

    def chunk_slot(st, c):
        return lax.rem(st * SIDE_SLOTS + c, POOL_SLOTS)

    def chunk_copies(st, c):
        seq = st // steps_per_seq
        base = (st % steps_per_seq) * pages_per_step + c * chunk_pages
        slot = chunk_slot(st, c)
        return [pltpu.make_async_copy(ck_ref.at[pt_ref[seq, base + p]], pbuf.at[slot, p], sem.at[slot])
                for p in range(chunk_pages)]

    def side_work(c):
        if c == 0:
            @pl.when(step == 0)
            def _():
                for a in range(POOL_SLOTS - 1):
                    for cp in chunk_copies(step, a):
                        cp.start()

            @pl.when(step % steps_per_seq == 0)
            def _():
                km_ref[...] = jnp.zeros_like(km_ref)
        ahead = c + POOL_SLOTS - 1
        if ahead < SIDE_SLOTS:
            for cp in chunk_copies(step, ahead):
                cp.start()
        else:
            @pl.when(step + 1 < n_steps)
            def _():
                for cp in chunk_copies(step + 1, ahead - SIDE_SLOTS):
                    cp.start()
        for cp in chunk_copies(step, c):
            cp.wait()
        slot = chunk_slot(step, c)
        first_block = (step % steps_per_seq) * (pages_per_step // ppb) + c * chunk_blocks
        acc = km_ref[...]
        for n in range(chunk_blocks):
            tot = pbuf[slot, n * ppb]
            for p in range(1, ppb):
                tot = tot + pbuf[slot, n * ppb + p]
            col = jnp.sum(tot, axis=1, keepdims=True) * (1.0 / MOBA_BLOCK)
            acc = jnp.where(lane == first_block + n, col, acc)
        km_ref[...] = acc

    _outffn_compute(*refs[:-5], y_ref, ff_chunks=ff_chunks, moba_transposed=moba_transposed, side_work=side_work)


def _outffn(x, ro, mo, g1, sh2, sc2, g2, n2w, fw, wo, wg, wu, wd, *, tm, moba_transposed, pool=None):
    G, R, D = x.shape
    W = ro.shape[2]
    dff = wg.shape[1]
    cuts = [round(dff * c / FF_CHUNKS / 256) * 256 for c in range(FF_CHUNKS)] + [dff]
    ff_chunks = tuple(zip(cuts[:-1], cuts[1:]))
    inner = R // tm
    row = pl.BlockSpec((None, tm, D), lambda g, i, *_: (g, i, 0))
    act = pl.BlockSpec((None, tm, W), lambda g, i, *_: (g, i, 0))
    act_t = pl.BlockSpec((None, W, tm), lambda g, i, *_: (g, 0, i))
    in_specs = [row, act, act_t if moba_transposed else act,
                _mod_spec(g1, tm), _mod_spec(sh2, tm), _mod_spec(sc2, tm), _mod_spec(g2, tm),
                _const_spec(n2w), _const_spec(fw), _const_spec(wo), _const_spec(wg), _const_spec(wu),
                _const_spec(wd)]
    args = (x, ro, mo, g1, sh2, sc2, g2, n2w, fw, wo, wg, wu, wd)
    y_shape = jax.ShapeDtypeStruct((G, R, D), F32)
    if pool is None:
        return pl.pallas_call(
            functools.partial(_outffn_kernel, ff_chunks=ff_chunks, moba_transposed=moba_transposed),
            out_shape=y_shape, grid=(G, inner), in_specs=in_specs, out_specs=row,
            compiler_params=_params("parallel", "parallel"),
            name="outproj_ffn",
        )(*args)
    page_table, ckt = pool
    bd, n_pages = page_table.shape
    _, hd, page = ckt.shape
    ppb = MOBA_BLOCK // page
    n_steps = G * inner
    pages_per_step = bd * n_pages // n_steps
    assert pages_per_step * n_steps == bd * n_pages and n_pages % pages_per_step == 0
    assert pages_per_step % (SIDE_SLOTS * ppb) == 0 and POOL_SLOTS - 1 <= SIDE_SLOTS
    steps_per_seq = n_pages // pages_per_step
    n_blocks = n_pages // ppb
    km_spec = pl.BlockSpec((None, hd, n_blocks), lambda g, i, *_: ((g * inner + i) // steps_per_seq, 0, 0))
    return pl.pallas_call(
        functools.partial(_outffn_pool_kernel, ff_chunks=ff_chunks, moba_transposed=moba_transposed,
                          inner_steps=inner, pages_per_step=pages_per_step),
        out_shape=[y_shape, jax.ShapeDtypeStruct((bd, hd, n_blocks), F32)],
        grid_spec=pltpu.PrefetchScalarGridSpec(
            num_scalar_prefetch=1,
            grid=(G, inner),
            in_specs=in_specs + [pl.BlockSpec(memory_space=pl.ANY)],
            out_specs=[row, km_spec],
            scratch_shapes=[pltpu.VMEM((POOL_SLOTS, pages_per_step // SIDE_SLOTS, hd, page), F32),
                            pltpu.SemaphoreType.DMA((POOL_SLOTS,))],
        ),
        compiler_params=_params("arbitrary", "arbitrary"),
        name="outproj_ffn_pool",
    )(page_table, *args, ckt)


def _rope_angles(pos):
    inv = ROPE_BASE ** (-jnp.arange(HALF, dtype=F32) / HALF)
    ang = pos.astype(F32)[:, None] * inv[None, :]
    return jnp.cos(ang), jnp.sin(ang)


def _rope_tables(pos):
    cos, sin = _rope_angles(pos)
    reps = LANES // HEAD_DIM
    return jnp.tile(cos, (1, 2 * reps)), jnp.tile(jnp.concatenate([-sin, sin], axis=1), (1, reps))


def kernel(x_prompt, x_sample, cache_k, cache_v, state_ret, page_table, c_prompt, c_sample,
           norm1_w, norm2_w, final_w, w_mod, b_mod, w_in, ret_gn_w, moba_norm_w, w_out,
           w_gate, w_up, w_down):
    Bp, S, D = x_prompt.shape
    Bd, T, _ = x_sample.shape
    depth = w_in.shape[0]
    assert depth == 1, "single decoder layer"
    n_pool, page, m_heads = cache_k.shape[1], cache_k.shape[2], cache_k.shape[3]
    n_pages = page_table.shape[1]
    past_len = n_pages * page
    assert past_len % MOBA_BLOCK == 0 and MOBA_BLOCK % page == 0
    n_full = past_len // MOBA_BLOCK
    moba_w = m_heads * HEAD_DIM
    ret_w = (w_in.shape[2] - 3 * moba_w) // 4
    assert ret_w == moba_w, "the two head groups share one projection width"
    W = ret_w
    r_heads = ret_w // HEAD_DIM
    fw = final_w.reshape(1, D)
    rows_s = Bd * T
    pad8 = SUBLANES

    w_in0 = w_in[0].astype(BF16)
    col = lambda g: w_in0[:, g * W:(g + 1) * W]
    w_tok = jnp.concatenate([col(0), col(2), col(3)], axis=1)
    w_trn = jnp.concatenate([col(1), col(4), col(5), col(6)], axis=1).T
    wo, wg, wu, wd = (w[0].astype(BF16) for w in (w_out, w_gate, w_up, w_down))
    n1w, n2w = norm1_w[0].reshape(1, D), norm2_w[0].reshape(1, D)
    gnw, mnw = ret_gn_w[0].reshape(1, ret_w), moba_norm_w[0].reshape(1, moba_w)

    mod = _modulation(jnp.concatenate([c_prompt, c_sample], axis=0), w_mod[0], b_mod[0])
    mod_p = [m[:, None, :] for m in jnp.split(mod[:Bp], 6, axis=-1)]
    mod_s = [jnp.repeat(m, T, axis=0)[None] for m in jnp.split(mod[Bp:], 6, axis=-1)]

    cos_a, sin_a = _rope_angles(jnp.arange(S))
    cos_p, sin_p = _rope_tables(jnp.arange(S))
    rq, rkt, rv, rg, mqt, mk, mkt, mvt = _inproj_prompt(x_prompt, mod_p[0], mod_p[1], n1w, cos_p, sin_p,
                                                        cos_a.T, sin_a.T, w_tok, w_trn, tm=512)
    s0_p = jnp.zeros((Bp, r_heads, HEAD_DIM, HEAD_DIM), F32)
    ret_o, ret_p = _retention(rq, rkt, rv, rg, s0_p, _decay_tables(r_heads, RET_CHUNK, RET_CHUNK), gnw,
                              tl=512, out_dtype=BF16)
    nwt = jnp.broadcast_to(mnw.reshape(moba_w, 1), (moba_w, MOBA_BLOCK))
    moba_ot = _moba_prompt(mqt, mk, mkt, mvt, nwt, out_dtype=BF16)
    ckt4 = jnp.transpose(cache_k[0], (0, 2, 3, 1))
    cvt4 = jnp.transpose(cache_v[0], (0, 2, 3, 1))
    y_prompt, kmt = _outffn(x_prompt, ret_o, moba_ot, mod_p[2], mod_p[3], mod_p[4], mod_p[5], n2w, fw,
                            wo, wg, wu, wd, tm=512, moba_transposed=True,
                            pool=(page_table, ckt4.reshape(n_pool, moba_w, page)))
    to_rows = lambda t: t.reshape(1, Bp, m_heads, HEAD_DIM, S).transpose(0, 1, 4, 2, 3)
    k_prompt, v_prompt = to_rows(mkt), to_rows(mvt)

    xs = x_sample.reshape(1, rows_s, D)
    cos_s, sin_s = _rope_tables(past_len + jnp.arange(rows_s) % T)
    sq, sk, sv, sg, smq, smk, smv = _inproj_sample(xs, mod_s[0], mod_s[1], n1w, cos_s, sin_s, w_in0)
    padc = lambda t: jnp.pad(t.reshape(Bd, T, ret_w), ((0, 0), (0, RET_CHUNK - T), (0, 0)))
    skt = jnp.pad(sk.reshape(Bd, T, ret_w).transpose(0, 2, 1), ((0, 0), (0, 0), (0, RET_CHUNK - T)))
    ret_os, ret_s = _retention(padc(sq), skt, padc(sv), padc(sg), state_ret[0],
                               _decay_tables(r_heads, T, RET_CHUNK), gnw, tl=RET_CHUNK, out_dtype=F32)
    ret_os = ret_os[:, :T].reshape(1, rows_s, ret_w)

    assert kmt.shape[2] == n_full
    q8 =jnp.pad(smq.reshape(Bd, T, moba_w), ((0, 0), (0, pad8 - T), (0, 0)))
    top = _topk(q8, kmt)[:, :T, :m_heads * MOBA_TOPK]
    new_t = lambda t: jnp.pad(t.reshape(Bd, T, m_heads, HEAD_DIM).transpose(0, 2, 3, 1),
                              ((0, 0), (0, 0), (0, 0), (0, pad8 - T)))
    moba_os = _attend(top.reshape(-1), page_table.reshape(-1), q8, new_t(smk), new_t(smv), mnw, ckt4, cvt4,
                      tokens=T, n_pages=n_pages)
    moba_os = moba_os[:, :T].reshape(1, rows_s, moba_w)
    y_sample = _outffn(xs, ret_os, moba_os, mod_s[2], mod_s[3], mod_s[4], mod_s[5], n2w, fw,
                       wo, wg, wu, wd, tm=rows_s, moba_transposed=False).reshape(Bd, T, D)
    k_sample = smk.reshape(1, Bd, T, m_heads, HEAD_DIM)
    v_sample = smv.reshape(1, Bd, T, m_heads, HEAD_DIM)

    return (y_prompt, y_sample, k_prompt, v_prompt, ret_p[None], k_sample, v_sample, ret_s[None])
```

```python
import functools
import math

import jax
import jax.numpy as jnp
from jax import lax
from jax.experimental import pallas as pl
from jax.experimental.pallas import tpu as pltpu

F32 = jnp.float32
BF16 = jnp.bfloat16
HIGHEST = lax.Precision.HIGHEST

HEAD_DIM = 64
HALF = HEAD_DIM // 2
RET_CHUNK = 128
MOBA_BLOCK = 256
MOBA_TOPK = 3
ROPE_BASE = 10000.0
NORM_EPS = 1e-6
LANES = 128
SUBLANES = 8
VMEM_LIMIT = 56 * 1024 * 1024
NEG_INF = float("-inf")
MOBA_QSCALE = HEAD_DIM ** -0.5 * math.log2(math.e)
NT = (((1,), (1,)), ((), ()))
TN = (((0,), (0,)), ((), ()))


def _params(*sem):
    return pltpu.CompilerParams(dimension_semantics=sem, vmem_limit_bytes=VMEM_LIMIT)


def _rms(x, w):
    return x * lax.rsqrt(jnp.mean(x * x, axis=-1, keepdims=True) + NORM_EPS) * w


def _silu(x):
    return x * jax.nn.sigmoid(x)


def _const_spec(a):
    return pl.BlockSpec(a.shape, lambda *_: (0,) * a.ndim, pipeline_mode=pl.Buffered(1))


def _mod_kernel(c_ref, w_ref, b_ref, o_ref):
    s = _silu(c_ref[...])
    o_ref[...] = jnp.dot(s, w_ref[...], preferred_element_type=F32, precision=HIGHEST) + b_ref[...]


def _modulation(c, w_mod, b_mod):
    n, d = c.shape
    cols = w_mod.shape[1]
    tn = 1536
    return pl.pallas_call(
        _mod_kernel,
        out_shape=jax.ShapeDtypeStruct((n, cols), F32),
        grid=(cols // tn,),
        in_specs=[pl.BlockSpec((n, d), lambda j: (0, 0)),
                  pl.BlockSpec((d, tn), lambda j: (0, j)),
                  pl.BlockSpec((1, tn), lambda j: (0, j))],
        out_specs=pl.BlockSpec((n, tn), lambda j: (0, j)),
        compiler_params=_params("arbitrary"),
        name="modulation",
    )(c, w_mod, b_mod.reshape(1, cols))


def _mod_spec(arr, tm):
    if arr.shape[1] == 1:
        return pl.BlockSpec((None, 1, arr.shape[2]), lambda g, i, *_: (g, 0, 0))
    return pl.BlockSpec((None, tm, arr.shape[2]), lambda g, i, *_: (g, i, 0))


def _normed_input(x_ref, sh_ref, sc_ref, nw_ref):
    return (_rms(x_ref[...], nw_ref[...]) * (1.0 + sc_ref[...]) + sh_ref[...]).astype(BF16)


def _rope_store(z, cos, sin, o_ref, scale):
    lane = lax.broadcasted_iota(jnp.int32, cos.shape, 1)
    first_half = (lane % HEAD_DIM) < HALF
    for c in range(z.shape[1] // LANES):
        zc = z[:, c * LANES:(c + 1) * LANES]
        partner = jnp.where(first_half, pltpu.roll(zc, LANES - HALF, 1), pltpu.roll(zc, HALF, 1))
        o_ref[:, c * LANES:(c + 1) * LANES] = ((zc * cos + partner * sin) * scale).astype(o_ref.dtype)


def _inproj_sample_kernel(x_ref, sh_ref, sc_ref, nw_ref, cos_ref, sin_ref, w_ref,
                          rq_ref, rk_ref, rv_ref, rg_ref, mq_ref, mk_ref, mv_ref, *, width):
    h = _normed_input(x_ref, sh_ref, sc_ref, nw_ref)
    proj = lambda g: jnp.dot(h, w_ref[:, g * width:(g + 1) * width], preferred_element_type=F32)
    _rope_store(proj(0), cos_ref[...], sin_ref[...], rq_ref, 1.0)
    _rope_store(proj(1), cos_ref[...], sin_ref[...], rk_ref, HEAD_DIM ** -0.5)
    rv_ref[...] = proj(2)
    rg_ref[...] = _silu(proj(3))
    mq_ref[...] = proj(4) * MOBA_QSCALE
    mk_ref[...] = proj(5)
    mv_ref[...] = proj(6)


def _inproj_sample(x, sh, sc, nw, cos_t, sin_t, w_bf):
    G, R, D = x.shape
    width = w_bf.shape[1] // 7
    row = pl.BlockSpec((None, R, D), lambda g, i: (g, i, 0))
    tab = pl.BlockSpec((R, LANES), lambda g, i: (i, 0))
    act = pl.BlockSpec((None, R, width), lambda g, i: (g, i, 0))
    return pl.pallas_call(
        functools.partial(_inproj_sample_kernel, width=width),
        out_shape=[jax.ShapeDtypeStruct((G, R, width), F32)] * 7,
        grid=(G, 1),
        in_specs=[row, _mod_spec(sh, R), _mod_spec(sc, R), _const_spec(nw), tab, tab, _const_spec(w_bf)],
        out_specs=[act] * 7,
        compiler_params=_params("parallel", "parallel"),
        name="inproj_sample",
    )(x, sh, sc, nw, cos_t, sin_t, w_bf)


def _inproj_prompt_kernel(x_ref, sh_ref, sc_ref, nw_ref, cos_ref, sin_ref, cost_ref, sint_ref, w_ref, wt_ref,
                          rq_ref, rkt_ref, rv_ref, rg_ref, mqt_ref, mk_ref, mkt_ref, mvt_ref, *, width):
    h = _normed_input(x_ref, sh_ref, sc_ref, nw_ref)
    proj = lambda g: jnp.dot(h, w_ref[:, g * width:(g + 1) * width], preferred_element_type=F32)
    proj_t = lambda g: lax.dot_general(wt_ref[g * width:(g + 1) * width, :], h, NT, preferred_element_type=F32)
    _rope_store(proj(0), cos_ref[...], sin_ref[...], rq_ref, 1.0)
    rv_ref[...] = proj(1).astype(rv_ref.dtype)
    rg_ref[...] = _silu(proj(2)).astype(rg_ref.dtype)
    zt = proj_t(0)
    cost = cost_ref[...]
    sint = sint_ref[...]
    scale = HEAD_DIM ** -0.5
    for hd in range(width // HEAD_DIM):
        lo = slice(hd * HEAD_DIM, hd * HEAD_DIM + HALF)
        hi = slice(hd * HEAD_DIM + HALF, (hd + 1) * HEAD_DIM)
        a, b = zt[lo], zt[hi]
        rkt_ref[lo, :] = ((a * cost - b * sint) * scale).astype(rkt_ref.dtype)
        rkt_ref[hi, :] = ((a * sint + b * cost) * scale).astype(rkt_ref.dtype)
    mqt_ref[...] = (proj_t(1) * MOBA_QSCALE).astype(mqt_ref.dtype)
    mkt = proj_t(2)
    mkt_ref[...] = mkt
    mk_ref[...] = mkt.T.astype(mk_ref.dtype)
    mvt_ref[...] = proj_t(3)


def _inproj_prompt(x, sh, sc, nw, cos_t, sin_t, cos_tt, sin_tt, w_tok, w_trn, *, tm):
    G, R, D = x.shape
    width = w_tok.shape[1] // 3
    row = pl.BlockSpec((None, tm, D), lambda g, i: (g, i, 0))
    tab = pl.BlockSpec((tm, LANES), lambda g, i: (i, 0))
    tab_t = pl.BlockSpec((HALF, tm), lambda g, i: (0, i))
    act = pl.BlockSpec((None, tm, width), lambda g, i: (g, i, 0))
    act_t = pl.BlockSpec((None, width, tm), lambda g, i: (g, 0, i))
    tok = jax.ShapeDtypeStruct((G, R, width), BF16)
    trn = lambda dt: jax.ShapeDtypeStruct((G, width, R), dt)
    return pl.pallas_call(
        functools.partial(_inproj_prompt_kernel, width=width),
        out_shape=[tok, trn(BF16), tok, tok, trn(BF16), tok, trn(F32), trn(F32)],
        grid=(G, R // tm),
        in_specs=[row, _mod_spec(sh, tm), _mod_spec(sc, tm), _const_spec(nw), tab, tab, tab_t, tab_t,
                  _const_spec(w_tok), _const_spec(w_trn)],
        out_specs=[act, act_t, act, act, act_t, act, act_t, act_t],
        compiler_params=_params("parallel", "parallel"),
        name="inproj_prompt",
    )(x, sh, sc, nw, cos_t, sin_t, cos_tt, sin_tt, w_tok, w_trn)


def _retention_kernel(q_ref, kt_ref, v_ref, g_ref, s0_ref, dmat2_ref, qdec_ref, kdect_ref, cdec_ref, gnw_ref,
                      avg_ref, o_ref, sout_ref, s_scr, o_scr, *, heads, chunks):
    j = pl.program_id(1)
    pairs = heads // 2
    lane = lax.broadcasted_iota(jnp.int32, (LANES, LANES), 1)
    sub = lax.broadcasted_iota(jnp.int32, (LANES, LANES), 0)
    first = lane < HEAD_DIM
    diag = first == (sub < HEAD_DIM)
    keep_a = jnp.where(first, 1.0, 0.0).astype(BF16)
    keep_b = jnp.where(first, 0.0, 1.0).astype(BF16)

    @pl.when(j == 0)
    def _():
        z = jnp.zeros((HEAD_DIM, HEAD_DIM), F32)
        for p in range(pairs):
            s_scr[p] = jnp.concatenate([jnp.concatenate([s0_ref[2 * p], z], axis=1),
                                        jnp.concatenate([z, s0_ref[2 * p + 1]], axis=1)], axis=0)

    gnw = gnw_ref[...]
    avg = avg_ref[...]

    def group_mean(t):
        hi = t.astype(BF16)
        lo = (t - hi.astype(F32)).astype(BF16)
        return jnp.dot(hi, avg, preferred_element_type=F32) + jnp.dot(lo, avg, preferred_element_type=F32)

    for c in range(chunks):
        rows = slice(c * RET_CHUNK, (c + 1) * RET_CHUNK)
        for p in range(pairs):
            cs = slice(p * LANES, (p + 1) * LANES)
            qp = q_ref[rows, cs].astype(BF16)
            vp = v_ref[rows, cs].astype(BF16)
            ktp = kt_ref[cs, rows]
            q2 = jnp.concatenate([qp * keep_a, qp * keep_b], axis=0)
            att2 = jnp.dot(q2, ktp.astype(BF16), preferred_element_type=F32) * dmat2_ref[p]
            o2 = jnp.dot(att2.astype(BF16), vp, preferred_element_type=F32)
            s = s_scr[p]
            o = (jnp.where(first, o2[:RET_CHUNK], o2[RET_CHUNK:])
                 + jnp.dot(qp, s.astype(BF16), preferred_element_type=F32) * qdec_ref[:, cs])
            kdt = (ktp.astype(F32) * kdect_ref[cs, :]).astype(BF16)
            s_scr[p] = s * cdec_ref[p] + jnp.where(diag, jnp.dot(kdt, vp, preferred_element_type=F32), 0.0)
            o_scr[rows, cs] = o

    o_all = o_scr[...]
    d = o_all - group_mean(o_all)
    var = group_mean(d * d)
    o_ref[...] = (g_ref[...].astype(F32) * (d * lax.rsqrt(var + NORM_EPS) * gnw)).astype(o_ref.dtype)

    @pl.when(j == pl.num_programs(1) - 1)
    def _():
        for p in range(pairs):
            s = s_scr[p]
            sout_ref[2 * p] = s[:HEAD_DIM, :HEAD_DIM]
            sout_ref[2 * p + 1] = s[HEAD_DIM:, HEAD_DIM:]


def _retention(q, kt, v, gs, s0, tables, gnw, *, tl, out_dtype):
    B, L, W = q.shape
    heads = W // HEAD_DIM
    seq = pl.BlockSpec((None, tl, W), lambda b, j: (b, j, 0))
    seq_t = pl.BlockSpec((None, W, tl), lambda b, j: (b, 0, j))
    st = pl.BlockSpec((None, heads, HEAD_DIM, HEAD_DIM), lambda b, j: (b, 0, 0, 0))
    return pl.pallas_call(
        functools.partial(_retention_kernel, heads=heads, chunks=tl // RET_CHUNK),
        out_shape=[jax.ShapeDtypeStruct((B, L, W), out_dtype),
                   jax.ShapeDtypeStruct((B, heads, HEAD_DIM, HEAD_DIM), F32)],
        grid=(B, L // tl),
        in_specs=[seq, seq_t, seq, seq, st] + [_const_spec(t) for t in tables] + [_const_spec(gnw),
                                                                                   _const_spec(_head_average(W))],
        out_specs=[seq, st],
        scratch_shapes=[pltpu.VMEM((heads // 2, LANES, LANES), F32), pltpu.VMEM((tl, W), F32)],
        compiler_params=_params("parallel", "arbitrary"),
        name="retention",
    )(q, kt, v, gs, s0, *tables, gnw, _head_average(W))


def _retention_step_kernel(q_ref, k_ref, v_ref, g_ref, s0_ref, dmat2_ref, qdec_ref, kdec_ref, cdec_ref, gnw_ref,
                           avg_ref, o_ref, sout_ref, *, heads):
    seqs, rows, _ = q_ref.shape
    pairs = heads // 2
    lane = lax.broadcasted_iota(jnp.int32, (LANES, LANES), 1)
    sub = lax.broadcasted_iota(jnp.int32, (LANES, LANES), 0)
    diag = (lane < HEAD_DIM) == (sub < HEAD_DIM)
    first = lax.broadcasted_iota(jnp.int32, (rows, LANES), 1) < HEAD_DIM
    keep_a = jnp.where(first, 1.0, 0.0).astype(BF16)
    keep_b = jnp.where(first, 0.0, 1.0).astype(BF16)
    zpad = jnp.zeros((LANES - rows, LANES), BF16)
    zs = jnp.zeros((HEAD_DIM, HEAD_DIM), F32)
    avg = avg_ref[...]
    gnw = gnw_ref[...]

    def group_mean(t):
        hi = t.astype(BF16)
        lo = (t - hi.astype(F32)).astype(BF16)
        return jnp.dot(hi, avg, preferred_element_type=F32) + jnp.dot(lo, avg, preferred_element_type=F32)

    for s in range(seqs):
        outs = []
        for p in range(pairs):
            cs = slice(p * LANES, (p + 1) * LANES)
            qp = q_ref[s, :, cs].astype(BF16)
            kp = k_ref[s, :, cs]
            pad = lambda t: jnp.concatenate([t.astype(BF16), zpad], axis=0)
            kpad, vpad, kdpad = pad(kp), pad(v_ref[s, :, cs]), pad(kp * kdec_ref[:, cs])
            q2 = jnp.concatenate([qp * keep_a, qp * keep_b], axis=0)
            att2 = lax.dot_general(q2, kpad, NT, preferred_element_type=F32) * dmat2_ref[p]
            o2 = jnp.dot(att2.astype(BF16), vpad, preferred_element_type=F32)
            state = jnp.concatenate([jnp.concatenate([s0_ref[s, 2 * p], zs], axis=1),
                                     jnp.concatenate([zs, s0_ref[s, 2 * p + 1]], axis=1)], axis=0)
            outs.append(jnp.where(first, o2[:rows], o2[rows:])
                        + jnp.dot(qp, state.astype(BF16), preferred_element_type=F32) * qdec_ref[:, cs])
            state = state * cdec_ref[p] + jnp.where(
                diag, lax.dot_general(kdpad, vpad, TN, preferred_element_type=F32), 0.0)
            sout_ref[s, 2 * p] = state[:HEAD_DIM, :HEAD_DIM]
            sout_ref[s, 2 * p + 1] = state[HEAD_DIM:, HEAD_DIM:]
        o_all = jnp.concatenate(outs, axis=1)
        d = o_all - group_mean(o_all)
        var = group_mean(d * d)
        o_ref[s] = g_ref[s] * (d * lax.rsqrt(var + NORM_EPS) * gnw)


def _retention_step(q, k, v, gs, s0, tables, gnw):
    bd, rows, W = q.shape
    heads = W // HEAD_DIM
    seqs = max(s for s in (8, 4, 2, 1) if bd % s == 0)
    seq = pl.BlockSpec((seqs, rows, W), lambda b: (b, 0, 0))
    st = pl.BlockSpec((seqs, heads, HEAD_DIM, HEAD_DIM), lambda b: (b, 0, 0, 0))
    consts = list(tables) + [gnw, _head_average(W)]
    return pl.pallas_call(
        functools.partial(_retention_step_kernel, heads=heads),
        out_shape=[jax.ShapeDtypeStruct((bd, rows, W), F32),
                   jax.ShapeDtypeStruct((bd, heads, HEAD_DIM, HEAD_DIM), F32)],
        grid=(bd // seqs,),
        in_specs=[seq, seq, seq, seq, st] + [_const_spec(t) for t in consts],
        out_specs=[seq, st],
        compiler_params=_params("parallel"),
        name="retention_step",
    )(q, k, v, gs, s0, *consts)


def _head_average(width):
    r = jnp.arange(width) // HEAD_DIM
    return jnp.where(r[:, None] == r[None, :], 1.0 / HEAD_DIM, 0.0).astype(BF16)


def _decay_tables(heads, c_len, q_pad, k_pad):
    log_g = jnp.log1p(-jnp.exp2(-5.0 - jnp.arange(heads, dtype=F32)))
    idx = jnp.arange(c_len, dtype=F32)
    diff = idx[:, None] - idx[None, :]
    dmat = jnp.where(diff >= 0, jnp.exp(jnp.maximum(diff, 0.0) * log_g[:, None, None]), 0.0)
    q_dec = jnp.exp((idx + 1.0) * log_g[:, None])
    k_dec = jnp.exp((c_len - 1.0 - idx) * log_g[:, None])
    c_dec = jnp.exp(c_len * log_g)
    qp, kp = q_pad - c_len, k_pad - c_len
    dmat2 = jnp.pad(dmat, ((0, 0), (0, qp), (0, kp))).reshape(heads // 2, 2 * q_pad, k_pad)
    qdec = jnp.pad(jnp.repeat(q_dec.T, HEAD_DIM, axis=1), ((0, qp), (0, 0)))
    kdect = jnp.pad(jnp.repeat(k_dec, HEAD_DIM, axis=0), ((0, 0), (0, kp)))
    cdec = jnp.broadcast_to(jnp.repeat(c_dec, HEAD_DIM).reshape(heads // 2, 2 * HEAD_DIM, 1),
                            (heads // 2, 2 * HEAD_DIM, 2 * HEAD_DIM))
    return dmat2, qdec, kdect, cdec


def _col_reduce(x, op, final):
    while x.shape[0] % (2 * SUBLANES) == 0:
        half = x.shape[0] // 2
        x = op(x[:half], x[half:])
    return final(x, axis=0, keepdims=True)


def _moba_prompt_kernel(qt_ref, k_ref, kt_ref, vt_ref, nwt_ref, o_ref, *, seq):
    nb = seq // MOBA_BLOCK
    nbp = -(-nb // SUBLANES) * SUBLANES
    qb = MOBA_BLOCK
    pair = 2 * HEAD_DIM
    if nb > MOBA_TOPK + 1:
        lane_n = lax.broadcasted_iota(jnp.int32, (pair, LANES), 1)
        kmt = jnp.zeros((pair, LANES), F32)
        for n in range(nb):
            tot = kt_ref[:, n * MOBA_BLOCK:n * MOBA_BLOCK + LANES]
            for c in range(1, MOBA_BLOCK // LANES):
                tot = tot + kt_ref[:, n * MOBA_BLOCK + c * LANES:n * MOBA_BLOCK + (c + 1) * LANES]
            kmt = jnp.where(lane_n == n, jnp.sum(tot, axis=1, keepdims=True) * (1.0 / MOBA_BLOCK), kmt)
        km = kmt.T[0:nbp]
        lane_k = lax.broadcasted_iota(jnp.int32, km.shape, 1)
        km_heads = [jnp.where(lane_k < HEAD_DIM, km, 0.0), jnp.where(lane_k < HEAD_DIM, 0.0, km)]
        blk = lax.broadcasted_iota(jnp.int32, (nbp, qb), 0)
    key_i = lax.broadcasted_iota(jnp.int32, (qb, qb), 0)
    qry_i = lax.broadcasted_iota(jnp.int32, (qb, qb), 1)
    causal = key_i <= qry_i
    zeros = jnp.zeros((HEAD_DIM, qb), BF16)
    ones = jnp.ones((2 * SUBLANES, seq), BF16)
    vt_ones = [jnp.concatenate([vt_ref[hh * HEAD_DIM:(hh + 1) * HEAD_DIM, :].astype(BF16), ones], axis=0)
               for hh in range(2)]

    def scores(i, hh):
        qt = qt_ref[:, i * qb:(i + 1) * qb]
        hr = slice(hh * HEAD_DIM, (hh + 1) * HEAD_DIM)
        qm = jnp.concatenate([qt[hr], zeros] if hh == 0 else [zeros, qt[hr]], axis=0)
        return jnp.dot(k_ref[0:(i + 1) * MOBA_BLOCK, :], qm, preferred_element_type=F32)

    def softmax(i, hh, st):
        biases = [None] * i
        if i > MOBA_TOPK:
            qt = qt_ref[:, i * qb:(i + 1) * qb].astype(F32)
            gate = jnp.dot(km_heads[hh], qt, preferred_element_type=F32, precision=HIGHEST)
            valid = blk < i
            for n in range(i):
                gn = gate[n:n + 1, :]
                ahead = valid & ((gate > gn) | ((gate == gn) & (blk < n)))
                rank = jnp.sum(ahead.astype(F32), axis=0, keepdims=True)
                biases[n] = jnp.where(rank < MOBA_TOPK, 0.0, NEG_INF)
        past = [st[n * MOBA_BLOCK:(n + 1) * MOBA_BLOCK] for n in range(i)]
        own = jnp.where(causal, st[i * MOBA_BLOCK:(i + 1) * MOBA_BLOCK], NEG_INF)
        m = _col_reduce(own, jnp.maximum, jnp.max)
        for sb, bias in zip(past, biases):
            mb = _col_reduce(sb, jnp.maximum, jnp.max)
            m = jnp.maximum(m, mb if bias is None else mb + bias)
        pieces = [jnp.exp2(sb + (-m if bias is None else bias - m)).astype(BF16) for sb, bias in zip(past, biases)]
        pieces.append(jnp.exp2(own - m).astype(BF16))
        return jnp.concatenate(pieces, axis=0) if i > 0 else pieces[0]

    def output(i, hh, p):
        hr = slice(hh * HEAD_DIM, (hh + 1) * HEAD_DIM)
        ot = jnp.dot(vt_ones[hh][:, 0:(i + 1) * MOBA_BLOCK], p, preferred_element_type=F32)
        ot = ot[0:HEAD_DIM] / ot[HEAD_DIM:HEAD_DIM + 1]
        ms = jnp.mean(ot * ot, axis=0, keepdims=True)
        o_ref[hr, i * qb:(i + 1) * qb] = (ot * lax.rsqrt(ms + NORM_EPS) * nwt_ref[hr, :]).astype(o_ref.dtype)

    bodies = [(i, hh) for i in range(nb) for hh in range(2)]
    ahead, behind = 2, 2
    queue = [scores(*b) for b in bodies[:ahead]]
    pending = []
    for idx, body in enumerate(bodies):
        st = queue.pop(0)
        if idx + ahead < len(bodies):
            queue.append(scores(*bodies[idx + ahead]))
        pending.append((*body, softmax(*body, st)))
        if len(pending) > behind:
            output(*pending.pop(0))
    for item in pending:
        output(*item)


def _moba_prompt(qt, k, kt, vt, nwt, *, out_dtype):
    B, W, S = qt.shape
    pair = 2 * HEAD_DIM
    trn = pl.BlockSpec((None, pair, S), lambda b, h: (b, h, 0))
    return pl.pallas_call(
        functools.partial(_moba_prompt_kernel, seq=S),
        out_shape=jax.ShapeDtypeStruct((B, W, S), out_dtype),
        grid=(B, W // pair),
        in_specs=[trn, pl.BlockSpec((None, S, pair), lambda b, h: (b, 0, h)), trn, trn,
                  pl.BlockSpec((pair, MOBA_BLOCK), lambda b, h: (h, 0))],
        out_specs=trn,
        compiler_params=_params("parallel", "parallel"),
        name="moba_prompt",
    )(qt, k, kt, vt, nwt)


def _topk_kernel(q_ref, km_ref, o_ref, *, heads):
    seqs, rows, _ = q_ref.shape
    n_blk = km_ref.shape[2]
    lane_b = lax.broadcasted_iota(jnp.int32, (seqs * heads * rows, n_blk), 1).astype(F32)
    lane_o = lax.broadcasted_iota(jnp.int32, (rows, LANES), 1)
    gate = jnp.concatenate(
        [jnp.dot(q_ref[s, :, h * HEAD_DIM:(h + 1) * HEAD_DIM], km_ref[s, h * HEAD_DIM:(h + 1) * HEAD_DIM, :],
                 preferred_element_type=F32, precision=HIGHEST) for s in range(seqs) for h in range(heads)], axis=0)
    picks = []
    for r in range(MOBA_TOPK):
        m = jnp.max(gate, axis=1, keepdims=True)
        idx = jnp.min(jnp.where(gate == m, lane_b, float(n_blk)), axis=1, keepdims=True)
        gate = jnp.where(lane_b == idx, NEG_INF, gate)
        picks.append(idx)
    for s in range(seqs):
        out = jnp.zeros((rows, LANES), F32)
        for h in range(heads):
            base = (s * heads + h) * rows
            for r in range(MOBA_TOPK):
                out = jnp.where(lane_o == h * MOBA_TOPK + r, picks[r][base:base + rows], out)
        o_ref[s] = out.astype(jnp.int32)


def _topk(q8, kmt):
    bd, rows, W = q8.shape
    n_full = kmt.shape[2]
    seqs = max(s for s in (8, 4, 2, 1) if bd % s == 0)
    return pl.pallas_call(
        functools.partial(_topk_kernel, heads=W // HEAD_DIM),
        out_shape=jax.ShapeDtypeStruct((bd, rows, LANES), jnp.int32),
        grid=(bd // seqs,),
        in_specs=[pl.BlockSpec((seqs, rows, W), lambda b: (b, 0, 0)),
                  pl.BlockSpec((seqs, W, n_full), lambda b: (b, 0, 0))],
        out_specs=pl.BlockSpec((seqs, rows, LANES), lambda b: (b, 0, 0)),
        compiler_params=_params("parallel"),
        name="moba_topk",
    )(q8, kmt)


def _attend_kernel(top_ref, pt_ref, q_ref, knt_ref, vnt_ref, nw_ref, ck_ref, cv_ref, o_ref,
                   kbuf, vbuf, sem, *, heads, tokens, n_pages, page):
    b = pl.program_id(0)
    nb = pl.num_programs(0)
    ppb = MOBA_BLOCK // page
    n_sel = MOBA_TOPK * MOBA_BLOCK
    slot = b % 2

    def page_copies(seq, buf_slot):
        out = []
        for h in range(heads):
            for t in range(tokens):
                for r in range(MOBA_TOPK):
                    blk = top_ref[((seq * tokens + t) * heads + h) * MOBA_TOPK + r]
                    for p in range(ppb):
                        phys = pt_ref[seq * n_pages + blk * ppb + p]
                        dst = pl.ds((r * ppb + p) * page, page)
                        out.append(pltpu.make_async_copy(ck_ref.at[phys, h], kbuf.at[buf_slot, h, t, :, dst],
                                                         sem.at[buf_slot, 0]))
                        out.append(pltpu.make_async_copy(cv_ref.at[phys, h], vbuf.at[buf_slot, h, t, :, dst],
                                                         sem.at[buf_slot, 1]))
        return out

    @pl.when(b == 0)
    def _():
        for c in page_copies(b, slot):
            c.start()
        tail = (2, heads, tokens, HEAD_DIM, LANES)
        kbuf[:, :, :, :, n_sel:n_sel + LANES] = jnp.zeros(tail, F32)
        vbuf[:, :, :, :, n_sel:n_sel + LANES] = jnp.zeros(tail, F32)

    @pl.when(b + 1 < nb)
    def _():
        for c in page_copies(b + 1, 1 - slot):
            c.start()

    for c in page_copies(b, slot):
        c.wait()
    kb = kbuf.at[slot]
    vb = vbuf.at[slot]
    for h in range(heads):
        for t in range(tokens):
            kb[h, t, :, n_sel:n_sel + knt_ref.shape[2]] = knt_ref[h]
            vb[h, t, :, n_sel:n_sel + vnt_ref.shape[2]] = vnt_ref[h]

    rows = q_ref.shape[0]
    col = lax.broadcasted_iota(jnp.int32, (rows, n_sel + LANES), 1)
    row = lax.broadcasted_iota(jnp.int32, (rows, HEAD_DIM), 0)
    pairs = [(h, t) for h in range(heads) for t in range(tokens)]
    qs = [q_ref[:, h * HEAD_DIM:(h + 1) * HEAD_DIM].astype(BF16) for h in range(heads)]
    scores = [jnp.where(col <= n_sel + t,
                        jnp.dot(qs[h], kb[h, t].astype(BF16), preferred_element_type=F32), NEG_INF)
              for h, t in pairs]
    probs = []
    for s in scores:
        p = jnp.exp2(s - jnp.max(s, axis=1, keepdims=True))
        probs.append((p.astype(BF16), jnp.sum(p, axis=1, keepdims=True)))
    outs = [jnp.zeros((rows, HEAD_DIM), F32)] * heads
    for (h, t), (p, l) in zip(pairs, probs):
        o = lax.dot_general(p, vb[h, t].astype(BF16), NT, preferred_element_type=F32) / l
        outs[h] = jnp.where(row == t, o, outs[h])
    o_ref[...] = jnp.concatenate([_rms(outs[h], nw_ref[:, h * HEAD_DIM:(h + 1) * HEAD_DIM])
                                  for h in range(heads)], axis=1)


def _attend(top_flat, pt_flat, q8, knt, vnt, nw, ckt4, cvt4, *, tokens, n_pages):
    bd, rows, W = q8.shape
    heads = W // HEAD_DIM
    page = ckt4.shape[3]
    n_keys = MOBA_TOPK * MOBA_BLOCK + LANES
    any_spec = pl.BlockSpec(memory_space=pl.ANY)
    return pl.pallas_call(
        functools.partial(_attend_kernel, heads=heads, tokens=tokens, n_pages=n_pages, page=page),
        out_shape=jax.ShapeDtypeStruct((bd, rows, W), F32),
        grid_spec=pltpu.PrefetchScalarGridSpec(
            num_scalar_prefetch=2,
            grid=(bd,),
            in_specs=[pl.BlockSpec((None, rows, W), lambda b, *_: (b, 0, 0)),
                      pl.BlockSpec((None,) + knt.shape[1:], lambda b, *_: (b, 0, 0, 0)),
                      pl.BlockSpec((None,) + vnt.shape[1:], lambda b, *_: (b, 0, 0, 0)),
                      pl.BlockSpec((1, W), lambda b, *_: (0, 0)),
                      any_spec, any_spec],
            out_specs=pl.BlockSpec((None, rows, W), lambda b, *_: (b, 0, 0)),
            scratch_shapes=[pltpu.VMEM((2, heads, tokens, HEAD_DIM, n_keys), F32),
                            pltpu.VMEM((2, heads, tokens, HEAD_DIM, n_keys), F32),
                            pltpu.SemaphoreType.DMA((2, 2))],
        ),
        compiler_params=_params("arbitrary"),
        name="moba_attend",
    )(top_flat, pt_flat, q8, knt, vnt, nw, ckt4, cvt4)


FF_CHUNKS = 3
SIDE_SLOTS = FF_CHUNKS + 1
POOL_SLOTS = 3


def _outffn_compute(x_ref, ro_ref, mo_ref, g1_ref, sh2_ref, sc2_ref, g2_ref, n2w_ref, fw_ref,
                    wo_ref, wg_ref, wu_ref, wd_ref, y_ref, *, ff_chunks, moba_transposed, side_work):
    side_work(0)
    rw = ro_ref.shape[1]
    mo = mo_ref[...].astype(BF16)
    attn = (jnp.dot(ro_ref[...].astype(BF16), wo_ref[0:rw, :], preferred_element_type=F32)
            + lax.dot_general(mo, wo_ref[rw:, :], TN if moba_transposed else (((1,), (0,)), ((), ())),
                              preferred_element_type=F32))
    x1 = x_ref[...] + g1_ref[...] * attn
    h2 = (_rms(x1, n2w_ref[...]) * (1.0 + sc2_ref[...]) + sh2_ref[...]).astype(BF16)
    acc = None
    for c, (lo, hi) in enumerate(ff_chunks):
        side_work(1 + c)
        gate = jnp.dot(h2, wg_ref[:, lo:hi], preferred_element_type=F32)
        up = jnp.dot(h2, wu_ref[:, lo:hi], preferred_element_type=F32)
        part = jnp.dot((_silu(gate) * up).astype(BF16), wd_ref[lo:hi, :], preferred_element_type=F32)
        acc = part if acc is None else acc + part
    x2 = x1 + g2_ref[...] * acc
    y_ref[...] = _rms(x2, fw_ref[...])


def _outffn_kernel(*refs, ff_chunks, moba_transposed):
    _outffn_compute(*refs, ff_chunks=ff_chunks, moba_transposed=moba_transposed, side_work=lambda c: None)


def _outffn_pool_kernel(pt_ref, *refs, ff_chunks, moba_transposed, inner_steps, pages_per_step):
    ck_ref, y_ref, km_ref, pbuf, sem = refs[-5:]
    step = pl.program_id(0) * inner_steps + pl.program_id(1)
    n_steps = pl.num_programs(0) * inner_steps
    n_pages = pt_ref.shape[1]
    steps_per_seq = n_pages // pages_per_step
    chunk_pages = pages_per_step // SIDE_SLOTS
    ppb = MOBA_BLOCK // pbuf.shape[3]
    chunk_blocks = chunk_pages // ppb
    lane = lax.broadcasted_iota(jnp.int32, km_ref.shape, 1)

    def chunk_slot(st, c):
        return lax.rem(st * SIDE_SLOTS + c, POOL_SLOTS)

    def chunk_copies(st, c):
        seq = st // steps_per_seq
        base = (st % steps_per_seq) * pages_per_step + c * chunk_pages
        slot = chunk_slot(st, c)
        return [pltpu.make_async_copy(ck_ref.at[pt_ref[seq, base + p]], pbuf.at[slot, p], sem.at[slot])
                for p in range(chunk_pages)]

    def side_work(c):
        if c == 0:
            @pl.when(step == 0)
            def _():
                for a in range(POOL_SLOTS - 1):
                    for cp in chunk_copies(step, a):
                        cp.start()

            @pl.when(step % steps_per_seq == 0)
            def _():
                km_ref[...] = jnp.zeros_like(km_ref)
        ahead = c + POOL_SLOTS - 1
        if ahead < SIDE_SLOTS:
            for cp in chunk_copies(step, ahead):
                cp.start()
        else:
            @pl.when(step + 1 < n_steps)
            def _():
                for cp in chunk_copies(step + 1, ahead - SIDE_SLOTS):
                    cp.start()
        for cp in chunk_copies(step, c):
            cp.wait()
        slot = chunk_slot(step, c)
        first_block = (step % steps_per_seq) * (pages_per_step // ppb) + c * chunk_blocks
        acc = km_ref[...]
        for n in range(chunk_blocks):
            tot = pbuf[slot, n * ppb]
            for p in range(1, ppb):
                tot = tot + pbuf[slot, n * ppb + p]
            col = jnp.sum(tot, axis=1, keepdims=True) * (1.0 / MOBA_BLOCK)
            acc = jnp.where(lane == first_block + n, col, acc)
        km_ref[...] = acc

    _outffn_compute(*refs[:-5], y_ref, ff_chunks=ff_chunks, moba_transposed=moba_transposed, side_work=side_work)


def _outffn(x, ro, mo, g1, sh2, sc2, g2, n2w, fw, wo, wg, wu, wd, *, tm, moba_transposed, pool=None):
    G, R, D = x.shape
    W = ro.shape[2]
    dff = wg.shape[1]
    cuts = [round(dff * c / FF_CHUNKS / 256) * 256 for c in range(FF_CHUNKS)] + [dff]
    ff_chunks = tuple(zip(cuts[:-1], cuts[1:]))
    inner = R // tm
    row = pl.BlockSpec((None, tm, D), lambda g, i, *_: (g, i, 0))
    act = pl.BlockSpec((None, tm, W), lambda g, i, *_: (g, i, 0))
    act_t = pl.BlockSpec((None, W, tm), lambda g, i, *_: (g, 0, i))
    in_specs = [row, act, act_t if moba_transposed else act,
                _mod_spec(g1, tm), _mod_spec(sh2, tm), _mod_spec(sc2, tm), _mod_spec(g2, tm),
                _const_spec(n2w), _const_spec(fw), _const_spec(wo), _const_spec(wg), _const_spec(wu),
                _const_spec(wd)]
    args = (x, ro, mo, g1, sh2, sc2, g2, n2w, fw, wo, wg, wu, wd)
    y_shape = jax.ShapeDtypeStruct((G, R, D), F32)
    if pool is None:
        return pl.pallas_call(
            functools.partial(_outffn_kernel, ff_chunks=ff_chunks, moba_transposed=moba_transposed),
            out_shape=y_shape, grid=(G, inner), in_specs=in_specs, out_specs=row,
            compiler_params=_params("parallel", "parallel"),
            name="outproj_ffn",
        )(*args)
    page_table, ckt = pool
    bd, n_pages = page_table.shape
    _, hd, page = ckt.shape
    ppb = MOBA_BLOCK // page
    n_steps = G * inner
    pages_per_step = bd * n_pages // n_steps
    assert pages_per_step * n_steps == bd * n_pages and n_pages % pages_per_step == 0
    assert pages_per_step % (SIDE_SLOTS * ppb) == 0 and POOL_SLOTS - 1 <= SIDE_SLOTS
    steps_per_seq = n_pages // pages_per_step
    n_blocks = n_pages // ppb
    km_spec = pl.BlockSpec((None, hd, n_blocks), lambda g, i, *_: ((g * inner + i) // steps_per_seq, 0, 0))
    return pl.pallas_call(
        functools.partial(_outffn_pool_kernel, ff_chunks=ff_chunks, moba_transposed=moba_transposed,
                          inner_steps=inner, pages_per_step=pages_per_step),
        out_shape=[y_shape, jax.ShapeDtypeStruct((bd, hd, n_blocks), F32)],
        grid_spec=pltpu.PrefetchScalarGridSpec(
            num_scalar_prefetch=1,
            grid=(G, inner),
            in_specs=in_specs + [pl.BlockSpec(memory_space=pl.ANY)],
            out_specs=[row, km_spec],
            scratch_shapes=[pltpu.VMEM((POOL_SLOTS, pages_per_step // SIDE_SLOTS, hd, page), F32),
                            pltpu.SemaphoreType.DMA((POOL_SLOTS,))],
        ),
        compiler_params=_params("arbitrary", "arbitrary"),
        name="outproj_ffn_pool",
    )(page_table, *args, ckt)


def _rope_angles(pos):
    inv = ROPE_BASE ** (-jnp.arange(HALF, dtype=F32) / HALF)
    ang = pos.astype(F32)[:, None] * inv[None, :]
    return jnp.cos(ang), jnp.sin(ang)


def _rope_tables(pos):
    cos, sin = _rope_angles(pos)
    reps = LANES // HEAD_DIM
    return jnp.tile(cos, (1, 2 * reps)), jnp.tile(jnp.concatenate([-sin, sin], axis=1), (1, reps))


def kernel(x_prompt, x_sample, cache_k, cache_v, state_ret, page_table, c_prompt, c_sample,
           norm1_w, norm2_w, final_w, w_mod, b_mod, w_in, ret_gn_w, moba_norm_w, w_out,
           w_gate, w_up, w_down):
    Bp, S, D = x_prompt.shape
    Bd, T, _ = x_sample.shape
    depth = w_in.shape[0]
    assert depth == 1, "single decoder layer"
    n_pool, page, m_heads = cache_k.shape[1], cache_k.shape[2], cache_k.shape[3]
    n_pages = page_table.shape[1]
    past_len = n_pages * page
    assert past_len % MOBA_BLOCK == 0 and MOBA_BLOCK % page == 0
    n_full = past_len // MOBA_BLOCK
    moba_w = m_heads * HEAD_DIM
    ret_w = (w_in.shape[2] - 3 * moba_w) // 4
    assert ret_w == moba_w, "the two head groups share one projection width"
    W = ret_w
    r_heads = ret_w // HEAD_DIM
    fw = final_w.reshape(1, D)
    rows_s = Bd * T
    pad8 = SUBLANES

    w_in0 = w_in[0].astype(BF16)
    col = lambda g: w_in0[:, g * W:(g + 1) * W]
    w_tok = jnp.concatenate([col(0), col(2), col(3)], axis=1)
    w_trn = jnp.concatenate([col(1), col(4), col(5), col(6)], axis=1).T
    wo, wg, wu, wd = (w[0].astype(BF16) for w in (w_out, w_gate, w_up, w_down))
    n1w, n2w = norm1_w[0].reshape(1, D), norm2_w[0].reshape(1, D)
    gnw, mnw = ret_gn_w[0].reshape(1, ret_w), moba_norm_w[0].reshape(1, moba_w)

    mod = _modulation(jnp.concatenate([c_prompt, c_sample], axis=0), w_mod[0], b_mod[0])
    mod_p = [m[:, None, :] for m in jnp.split(mod[:Bp], 6, axis=-1)]
    mod_s = [jnp.repeat(m, T, axis=0)[None] for m in jnp.split(mod[Bp:], 6, axis=-1)]

    cos_a, sin_a = _rope_angles(jnp.arange(S))
    cos_p, sin_p = _rope_tables(jnp.arange(S))
    rq, rkt, rv, rg, mqt, mk, mkt, mvt = _inproj_prompt(x_prompt, mod_p[0], mod_p[1], n1w, cos_p, sin_p,
                                                        cos_a.T, sin_a.T, w_tok, w_trn, tm=512)
    s0_p = jnp.zeros((Bp, r_heads, HEAD_DIM, HEAD_DIM), F32)
    ret_o, ret_p = _retention(rq, rkt, rv, rg, s0_p, _decay_tables(r_heads, RET_CHUNK, RET_CHUNK, RET_CHUNK), gnw,
                              tl=512, out_dtype=BF16)
    nwt = jnp.broadcast_to(mnw.reshape(moba_w, 1), (moba_w, MOBA_BLOCK))
    moba_ot = _moba_prompt(mqt, mk, mkt, mvt, nwt, out_dtype=BF16)
    ckt4 = jnp.transpose(cache_k[0], (0, 2, 3, 1))
    cvt4 = jnp.transpose(cache_v[0], (0, 2, 3, 1))
    y_prompt, kmt = _outffn(x_prompt, ret_o, moba_ot, mod_p[2], mod_p[3], mod_p[4], mod_p[5], n2w, fw,
                            wo, wg, wu, wd, tm=512, moba_transposed=True,
                            pool=(page_table, ckt4.reshape(n_pool, moba_w, page)))
    to_rows = lambda t: t.reshape(1, Bp, m_heads, HEAD_DIM, S).transpose(0, 1, 4, 2, 3)
    k_prompt, v_prompt = to_rows(mkt), to_rows(mvt)

    xs = x_sample.reshape(1, rows_s, D)
    cos_s, sin_s = _rope_tables(past_len + jnp.arange(rows_s) % T)
    sq, sk, sv, sg, smq, smk, smv = _inproj_sample(xs, mod_s[0], mod_s[1], n1w, cos_s, sin_s, w_in0)
    rows_pad = 4 * SUBLANES
    padc = lambda t: jnp.pad(t.reshape(Bd, T, ret_w), ((0, 0), (0, rows_pad - T), (0, 0)))
    dmat2_s, qdec_s, kdect_s, cdec_s = _decay_tables(r_heads, T, rows_pad, LANES)
    ret_os, ret_s = _retention_step(padc(sq), padc(sk), padc(sv), padc(sg), state_ret[0],
                                    (dmat2_s, qdec_s, kdect_s[:, :rows_pad].T, cdec_s), gnw)
    ret_os = ret_os[:, :T].reshape(1, rows_s, ret_w)

    assert kmt.shape[2] == n_full
    q8 =jnp.pad(smq.reshape(Bd, T, moba_w), ((0, 0), (0, pad8 - T), (0, 0)))
    top = _topk(q8, kmt)[:, :T, :m_heads * MOBA_TOPK]
    new_t = lambda t: jnp.pad(t.reshape(Bd, T, m_heads, HEAD_DIM).transpose(0, 2, 3, 1),
                              ((0, 0), (0, 0), (0, 0), (0, pad8 - T)))
    moba_os = _attend(top.reshape(-1), page_table.reshape(-1), q8, new_t(smk), new_t(smv), mnw, ckt4, cvt4,
                      tokens=T, n_pages=n_pages)
    moba_os = moba_os[:, :T].reshape(1, rows_s, moba_w)
    y_sample = _outffn(xs, ret_os, moba_os, mod_s[2], mod_s[3], mod_s[4], mod_s[5], n2w, fw,
                       wo, wg, wu, wd, tm=rows_s, moba_transposed=False).reshape(Bd, T, D)
    k_sample = smk.reshape(1, Bd, T, m_heads, HEAD_DIM)
    v_sample = smv.reshape(1, Bd, T, m_heads, HEAD_DIM)

    return (y_prompt, y_sample, k_prompt, v_prompt, ret_p[None], k_sample, v_sample, ret_s[None])
```

```python
import functools
import math

import jax
import jax.numpy as jnp
from jax import lax
from jax.experimental import pallas as pl
from jax.experimental.pallas import tpu as pltpu

F32 = jnp.float32
BF16 = jnp.bfloat16
HIGHEST = lax.Precision.HIGHEST

HEAD_DIM = 64
HALF = HEAD_DIM // 2
RET_CHUNK = 128
MOBA_BLOCK = 256
MOBA_TOPK = 3
ROPE_BASE = 10000.0
NORM_EPS = 1e-6
LANES = 128
SUBLANES = 8
MXU_WIDTH = 256
VMEM_LIMIT = 56 * 1024 * 1024
NEG_INF = float("-inf")
MOBA_QSCALE = HEAD_DIM ** -0.5 * math.log2(math.e)
NT = (((1,), (1,)), ((), ()))
TN = (((0,), (0,)), ((), ()))


def _params(*sem):
    return pltpu.CompilerParams(dimension_semantics=sem, vmem_limit_bytes=VMEM_LIMIT)


def _rms(x, w):
    return x * lax.rsqrt(jnp.mean(x * x, axis=-1, keepdims=True) + NORM_EPS) * w


def _silu(x):
    return x * jax.nn.sigmoid(x)


def _const_spec(a):
    return pl.BlockSpec(a.shape, lambda *_: (0,) * a.ndim, pipeline_mode=pl.Buffered(1))


def _mod_kernel(c_ref, w_ref, b_ref, o_ref):
    s = _silu(c_ref[...])
    o_ref[...] = jnp.dot(s, w_ref[...], preferred_element_type=F32, precision=HIGHEST) + b_ref[...]


def _modulation(c, w_mod, b_mod):
    n, d = c.shape
    cols = w_mod.shape[1]
    tn = 1536
    return pl.pallas_call(
        _mod_kernel,
        out_shape=jax.ShapeDtypeStruct((n, cols), F32),
        grid=(cols // tn,),
        in_specs=[pl.BlockSpec((n, d), lambda j: (0, 0)),
                  pl.BlockSpec((d, tn), lambda j: (0, j)),
                  pl.BlockSpec((1, tn), lambda j: (0, j))],
        out_specs=pl.BlockSpec((n, tn), lambda j: (0, j)),
        compiler_params=_params("arbitrary"),
        name="modulation",
    )(c, w_mod, b_mod.reshape(1, cols))


def _mod_spec(arr, tm):
    if arr.shape[1] == 1:
        return pl.BlockSpec((None, 1, arr.shape[2]), lambda g, i, *_: (g, 0, 0))
    return pl.BlockSpec((None, tm, arr.shape[2]), lambda g, i, *_: (g, i, 0))


def _normed_input(x_ref, sh_ref, sc_ref, nw_ref):
    return (_rms(x_ref[...], nw_ref[...]) * (1.0 + sc_ref[...]) + sh_ref[...]).astype(BF16)


def _rope_store(z, cos, sin, o_ref, scale):
    lane = lax.broadcasted_iota(jnp.int32, cos.shape, 1)
    first_half = (lane % HEAD_DIM) < HALF
    for c in range(z.shape[1] // LANES):
        zc = z[:, c * LANES:(c + 1) * LANES]
        partner = jnp.where(first_half, pltpu.roll(zc, LANES - HALF, 1), pltpu.roll(zc, HALF, 1))
        o_ref[:, c * LANES:(c + 1) * LANES] = ((zc * cos + partner * sin) * scale).astype(o_ref.dtype)


def _inproj_sample_kernel(x_ref, sh_ref, sc_ref, nw_ref, cos_ref, sin_ref, w_ref,
                          rq_ref, rk_ref, rv_ref, rg_ref, mq_ref, mk_ref, mv_ref, *, width):
    h = _normed_input(x_ref, sh_ref, sc_ref, nw_ref)
    proj = lambda g: jnp.dot(h, w_ref[:, g * width:(g + 1) * width], preferred_element_type=F32)
    _rope_store(proj(0), cos_ref[...], sin_ref[...], rq_ref, 1.0)
    _rope_store(proj(1), cos_ref[...], sin_ref[...], rk_ref, HEAD_DIM ** -0.5)
    rv_ref[...] = proj(2)
    rg_ref[...] = _silu(proj(3))
    mq_ref[...] = proj(4) * MOBA_QSCALE
    mk_ref[...] = proj(5)
    mv_ref[...] = proj(6)


def _inproj_sample(x, sh, sc, nw, cos_t, sin_t, w_bf):
    G, R, D = x.shape
    width = w_bf.shape[1] // 7
    row = pl.BlockSpec((None, R, D), lambda g, i: (g, i, 0))
    tab = pl.BlockSpec((R, LANES), lambda g, i: (i, 0))
    act = pl.BlockSpec((None, R, width), lambda g, i: (g, i, 0))
    return pl.pallas_call(
        functools.partial(_inproj_sample_kernel, width=width),
        out_shape=[jax.ShapeDtypeStruct((G, R, width), F32)] * 7,
        grid=(G, 1),
        in_specs=[row, _mod_spec(sh, R), _mod_spec(sc, R), _const_spec(nw), tab, tab, _const_spec(w_bf)],
        out_specs=[act] * 7,
        compiler_params=_params("parallel", "parallel"),
        name="inproj_sample",
    )(x, sh, sc, nw, cos_t, sin_t, w_bf)


def _inproj_prompt_kernel(x_ref, sh_ref, sc_ref, nw_ref, cos_ref, sin_ref, cost_ref, sint_ref, w_ref, wt_ref,
                          rq_ref, rkt_ref, rv_ref, rg_ref, mqt_ref, mk_ref, mkt_ref, mvt_ref, *, width):
    h = _normed_input(x_ref, sh_ref, sc_ref, nw_ref)
    proj = lambda g: jnp.dot(h, w_ref[:, g * width:(g + 1) * width], preferred_element_type=F32)
    proj_t = lambda g: lax.dot_general(wt_ref[g * width:(g + 1) * width, :], h, NT, preferred_element_type=F32)
    _rope_store(proj(0), cos_ref[...], sin_ref[...], rq_ref, 1.0)
    rv_ref[...] = proj(1).astype(rv_ref.dtype)
    rg_ref[...] = _silu(proj(2)).astype(rg_ref.dtype)
    zt = proj_t(0)
    cost = cost_ref[...]
    sint = sint_ref[...]
    scale = HEAD_DIM ** -0.5
    for hd in range(width // HEAD_DIM):
        lo = slice(hd * HEAD_DIM, hd * HEAD_DIM + HALF)
        hi = slice(hd * HEAD_DIM + HALF, (hd + 1) * HEAD_DIM)
        a, b = zt[lo], zt[hi]
        rkt_ref[lo, :] = ((a * cost - b * sint) * scale).astype(rkt_ref.dtype)
        rkt_ref[hi, :] = ((a * sint + b * cost) * scale).astype(rkt_ref.dtype)
    mqt_ref[...] = (proj_t(1) * MOBA_QSCALE).astype(mqt_ref.dtype)
    mkt = proj_t(2)
    mkt_ref[...] = mkt
    mk_ref[...] = mkt.T.astype(mk_ref.dtype)
    mvt_ref[...] = proj_t(3)


def _inproj_prompt(x, sh, sc, nw, cos_t, sin_t, cos_tt, sin_tt, w_tok, w_trn, *, tm):
    G, R, D = x.shape
    width = w_tok.shape[1] // 3
    row = pl.BlockSpec((None, tm, D), lambda g, i: (g, i, 0))
    tab = pl.BlockSpec((tm, LANES), lambda g, i: (i, 0))
    tab_t = pl.BlockSpec((HALF, tm), lambda g, i: (0, i))
    act = pl.BlockSpec((None, tm, width), lambda g, i: (g, i, 0))
    act_t = pl.BlockSpec((None, width, tm), lambda g, i: (g, 0, i))
    tok = jax.ShapeDtypeStruct((G, R, width), BF16)
    trn = lambda dt: jax.ShapeDtypeStruct((G, width, R), dt)
    return pl.pallas_call(
        functools.partial(_inproj_prompt_kernel, width=width),
        out_shape=[tok, trn(BF16), tok, tok, trn(BF16), tok, trn(F32), trn(F32)],
        grid=(G, R // tm),
        in_specs=[row, _mod_spec(sh, tm), _mod_spec(sc, tm), _const_spec(nw), tab, tab, tab_t, tab_t,
                  _const_spec(w_tok), _const_spec(w_trn)],
        out_specs=[act, act_t, act, act, act_t, act, act_t, act_t],
        compiler_params=_params("parallel", "parallel"),
        name="inproj_prompt",
    )(x, sh, sc, nw, cos_t, sin_t, cos_tt, sin_tt, w_tok, w_trn)


def _retention_kernel(q_ref, kt_ref, v_ref, g_ref, s0_ref, dmat2_ref, qdec_ref, kdect_ref, cdec_ref, gnw_ref,
                      avg_ref, o_ref, sout_ref, s_scr, o_scr, *, heads, chunks):
    j = pl.program_id(1)
    pairs = heads // 2
    lane = lax.broadcasted_iota(jnp.int32, (LANES, LANES), 1)
    sub = lax.broadcasted_iota(jnp.int32, (LANES, LANES), 0)
    first = lane < HEAD_DIM
    diag = first == (sub < HEAD_DIM)
    keep_a = jnp.where(first, 1.0, 0.0).astype(BF16)
    keep_b = jnp.where(first, 0.0, 1.0).astype(BF16)

    @pl.when(j == 0)
    def _():
        z = jnp.zeros((HEAD_DIM, HEAD_DIM), F32)
        for p in range(pairs):
            s_scr[p] = jnp.concatenate([jnp.concatenate([s0_ref[2 * p], z], axis=1),
                                        jnp.concatenate([z, s0_ref[2 * p + 1]], axis=1)], axis=0)

    gnw = gnw_ref[...]
    avg = avg_ref[...]

    group_mean = functools.partial(_group_mean, avg=avg)

    for c in range(chunks):
        rows = slice(c * RET_CHUNK, (c + 1) * RET_CHUNK)
        for p in range(pairs):
            cs = slice(p * LANES, (p + 1) * LANES)
            qp = q_ref[rows, cs].astype(BF16)
            vp = v_ref[rows, cs].astype(BF16)
            ktp = kt_ref[cs, rows]
            q2 = jnp.concatenate([qp * keep_a, qp * keep_b], axis=0)
            att2 = jnp.dot(q2, ktp.astype(BF16), preferred_element_type=F32) * dmat2_ref[p]
            o2 = jnp.dot(att2.astype(BF16), vp, preferred_element_type=F32)
            s = s_scr[p]
            o = (jnp.where(first, o2[:RET_CHUNK], o2[RET_CHUNK:])
                 + jnp.dot(qp, s.astype(BF16), preferred_element_type=F32) * qdec_ref[:, cs])
            kdt = (ktp.astype(F32) * kdect_ref[cs, :]).astype(BF16)
            s_scr[p] = s * cdec_ref[p] + jnp.where(diag, jnp.dot(kdt, vp, preferred_element_type=F32), 0.0)
            o_scr[rows, cs] = o

    o_all = o_scr[...]
    d = o_all - group_mean(o_all)
    var = group_mean(d * d)
    o_ref[...] = (g_ref[...].astype(F32) * (d * lax.rsqrt(var + NORM_EPS) * gnw)).astype(o_ref.dtype)

    @pl.when(j == pl.num_programs(1) - 1)
    def _():
        for p in range(pairs):
            s = s_scr[p]
            sout_ref[2 * p] = s[:HEAD_DIM, :HEAD_DIM]
            sout_ref[2 * p + 1] = s[HEAD_DIM:, HEAD_DIM:]


def _retention(q, kt, v, gs, s0, tables, gnw, *, tl, out_dtype):
    B, L, W = q.shape
    heads = W // HEAD_DIM
    seq = pl.BlockSpec((None, tl, W), lambda b, j: (b, j, 0))
    seq_t = pl.BlockSpec((None, W, tl), lambda b, j: (b, 0, j))
    st = pl.BlockSpec((None, heads, HEAD_DIM, HEAD_DIM), lambda b, j: (b, 0, 0, 0))
    return pl.pallas_call(
        functools.partial(_retention_kernel, heads=heads, chunks=tl // RET_CHUNK),
        out_shape=[jax.ShapeDtypeStruct((B, L, W), out_dtype),
                   jax.ShapeDtypeStruct((B, heads, HEAD_DIM, HEAD_DIM), F32)],
        grid=(B, L // tl),
        in_specs=[seq, seq_t, seq, seq, st] + [_const_spec(t) for t in tables] + [_const_spec(gnw),
                                                                                   _const_spec(_head_average(W))],
        out_specs=[seq, st],
        scratch_shapes=[pltpu.VMEM((heads // 2, LANES, LANES), F32), pltpu.VMEM((tl, W), F32)],
        compiler_params=_params("parallel", "arbitrary"),
        name="retention",
    )(q, kt, v, gs, s0, *tables, gnw, _head_average(W))


def _retention_step_kernel(q_ref, k_ref, v_ref, g_ref, s0_ref, dmat2_ref, qdec_ref, kdec_ref, cdec_ref, gnw_ref,
                           avg_ref, o_ref, sout_ref, *, heads):
    seqs, rows, _ = q_ref.shape
    pairs = heads // 2
    lane = lax.broadcasted_iota(jnp.int32, (LANES, LANES), 1)
    sub = lax.broadcasted_iota(jnp.int32, (LANES, LANES), 0)
    diag = (lane < HEAD_DIM) == (sub < HEAD_DIM)
    first = lax.broadcasted_iota(jnp.int32, (rows, LANES), 1) < HEAD_DIM
    keep_a = jnp.where(first, 1.0, 0.0).astype(BF16)
    keep_b = jnp.where(first, 0.0, 1.0).astype(BF16)
    zpad = jnp.zeros((LANES - rows, LANES), BF16)
    zs = jnp.zeros((HEAD_DIM, HEAD_DIM), F32)
    avg = avg_ref[...]
    gnw = gnw_ref[...]

    group_mean = functools.partial(_group_mean, avg=avg)

    for s in range(seqs):
        outs = []
        for p in range(pairs):
            cs = slice(p * LANES, (p + 1) * LANES)
            qp = q_ref[s, :, cs].astype(BF16)
            kp = k_ref[s, :, cs]
            pad = lambda t: jnp.concatenate([t.astype(BF16), zpad], axis=0)
            kpad, vpad, kdpad = pad(kp), pad(v_ref[s, :, cs]), pad(kp * kdec_ref[:, cs])
            q2 = jnp.concatenate([qp * keep_a, qp * keep_b], axis=0)
            att2 = lax.dot_general(q2, kpad, NT, preferred_element_type=F32) * dmat2_ref[p]
            o2 = jnp.dot(att2.astype(BF16), vpad, preferred_element_type=F32)
            state = jnp.concatenate([jnp.concatenate([s0_ref[s, 2 * p], zs], axis=1),
                                     jnp.concatenate([zs, s0_ref[s, 2 * p + 1]], axis=1)], axis=0)
            outs.append(jnp.where(first, o2[:rows], o2[rows:])
                        + jnp.dot(qp, state.astype(BF16), preferred_element_type=F32) * qdec_ref[:, cs])
            state = state * cdec_ref[p] + jnp.where(
                diag, lax.dot_general(kdpad, vpad, TN, preferred_element_type=F32), 0.0)
            sout_ref[s, 2 * p] = state[:HEAD_DIM, :HEAD_DIM]
            sout_ref[s, 2 * p + 1] = state[HEAD_DIM:, HEAD_DIM:]
        o_all = jnp.concatenate(outs, axis=1)
        d = o_all - group_mean(o_all)
        var = group_mean(d * d)
        o_ref[s] = g_ref[s] * (d * lax.rsqrt(var + NORM_EPS) * gnw)


def _retention_step(q, k, v, gs, s0, tables, gnw):
    bd, rows, W = q.shape
    heads = W // HEAD_DIM
    seqs = max(s for s in (8, 4, 2, 1) if bd % s == 0)
    seq = pl.BlockSpec((seqs, rows, W), lambda b: (b, 0, 0))
    st = pl.BlockSpec((seqs, heads, HEAD_DIM, HEAD_DIM), lambda b: (b, 0, 0, 0))
    consts = list(tables) + [gnw, _head_average(W)]
    return pl.pallas_call(
        functools.partial(_retention_step_kernel, heads=heads),
        out_shape=[jax.ShapeDtypeStruct((bd, rows, W), F32),
                   jax.ShapeDtypeStruct((bd, heads, HEAD_DIM, HEAD_DIM), F32)],
        grid=(bd // seqs,),
        in_specs=[seq, seq, seq, seq, st] + [_const_spec(t) for t in consts],
        out_specs=[seq, st],
        compiler_params=_params("parallel"),
        name="retention_step",
    )(q, k, v, gs, s0, *consts)


def _group_mean(t, avg):
    hi = t.astype(BF16)
    lo = (t - hi.astype(F32)).astype(BF16)
    w = avg.shape[0]
    cols = [jnp.dot(hi[:, c:c + w], avg, preferred_element_type=F32)
            + jnp.dot(lo[:, c:c + w], avg, preferred_element_type=F32) for c in range(0, t.shape[1], w)]
    return jnp.concatenate(cols, axis=1) if len(cols) > 1 else cols[0]


def _head_average(width):
    width = min(width, MXU_WIDTH)
    r = jnp.arange(width) // HEAD_DIM
    return jnp.where(r[:, None] == r[None, :], 1.0 / HEAD_DIM, 0.0).astype(BF16)


def _decay_tables(heads, c_len, q_pad, k_pad):
    log_g = jnp.log1p(-jnp.exp2(-5.0 - jnp.arange(heads, dtype=F32)))
    idx = jnp.arange(c_len, dtype=F32)
    diff = idx[:, None] - idx[None, :]
    dmat = jnp.where(diff >= 0, jnp.exp(jnp.maximum(diff, 0.0) * log_g[:, None, None]), 0.0)
    q_dec = jnp.exp((idx + 1.0) * log_g[:, None])
    k_dec = jnp.exp((c_len - 1.0 - idx) * log_g[:, None])
    c_dec = jnp.exp(c_len * log_g)
    qp, kp = q_pad - c_len, k_pad - c_len
    dmat2 = jnp.pad(dmat, ((0, 0), (0, qp), (0, kp))).reshape(heads // 2, 2 * q_pad, k_pad)
    qdec = jnp.pad(jnp.repeat(q_dec.T, HEAD_DIM, axis=1), ((0, qp), (0, 0)))
    kdect = jnp.pad(jnp.repeat(k_dec, HEAD_DIM, axis=0), ((0, 0), (0, kp)))
    cdec = jnp.broadcast_to(jnp.repeat(c_dec, HEAD_DIM).reshape(heads // 2, 2 * HEAD_DIM, 1),
                            (heads // 2, 2 * HEAD_DIM, 2 * HEAD_DIM))
    return dmat2, qdec, kdect, cdec


def _col_reduce(x, op, final):
    while x.shape[0] % (2 * SUBLANES) == 0:
        half = x.shape[0] // 2
        x = op(x[:half], x[half:])
    return final(x, axis=0, keepdims=True)


def _moba_prompt_kernel(qt_ref, k_ref, kt_ref, vt_ref, nwt_ref, o_ref, *, seq):
    nb = seq // MOBA_BLOCK
    nbp = -(-nb // SUBLANES) * SUBLANES
    qb = MOBA_BLOCK
    pair = 2 * HEAD_DIM
    if nb > MOBA_TOPK + 1:
        lane_n = lax.broadcasted_iota(jnp.int32, (pair, LANES), 1)
        kmt = jnp.zeros((pair, LANES), F32)
        for n in range(nb):
            tot = kt_ref[:, n * MOBA_BLOCK:n * MOBA_BLOCK + LANES]
            for c in range(1, MOBA_BLOCK // LANES):
                tot = tot + kt_ref[:, n * MOBA_BLOCK + c * LANES:n * MOBA_BLOCK + (c + 1) * LANES]
            kmt = jnp.where(lane_n == n, jnp.sum(tot, axis=1, keepdims=True) * (1.0 / MOBA_BLOCK), kmt)
        km = kmt.T[0:nbp]
        lane_k = lax.broadcasted_iota(jnp.int32, km.shape, 1)
        km_heads = [jnp.where(lane_k < HEAD_DIM, km, 0.0), jnp.where(lane_k < HEAD_DIM, 0.0, km)]
        blk = lax.broadcasted_iota(jnp.int32, (nbp, qb), 0)
    key_i = lax.broadcasted_iota(jnp.int32, (qb, qb), 0)
    qry_i = lax.broadcasted_iota(jnp.int32, (qb, qb), 1)
    causal = key_i <= qry_i
    zeros = jnp.zeros((HEAD_DIM, qb), BF16)
    ones = jnp.ones((2 * SUBLANES, seq), BF16)
    vt_ones = [jnp.concatenate([vt_ref[hh * HEAD_DIM:(hh + 1) * HEAD_DIM, :].astype(BF16), ones], axis=0)
               for hh in range(2)]

    def scores(i, hh):
        qt = qt_ref[:, i * qb:(i + 1) * qb]
        hr = slice(hh * HEAD_DIM, (hh + 1) * HEAD_DIM)
        qm = jnp.concatenate([qt[hr], zeros] if hh == 0 else [zeros, qt[hr]], axis=0)
        return jnp.dot(k_ref[0:(i + 1) * MOBA_BLOCK, :], qm, preferred_element_type=F32)

    def softmax(i, hh, st):
        biases = [None] * i
        if i > MOBA_TOPK:
            qt = qt_ref[:, i * qb:(i + 1) * qb].astype(F32)
            gate = jnp.dot(km_heads[hh], qt, preferred_element_type=F32, precision=HIGHEST)
            valid = blk < i
            for n in range(i):
                gn = gate[n:n + 1, :]
                ahead = valid & ((gate > gn) | ((gate == gn) & (blk < n)))
                rank = jnp.sum(ahead.astype(F32), axis=0, keepdims=True)
                biases[n] = jnp.where(rank < MOBA_TOPK, 0.0, NEG_INF)
        past = [st[n * MOBA_BLOCK:(n + 1) * MOBA_BLOCK] for n in range(i)]
        own = jnp.where(causal, st[i * MOBA_BLOCK:(i + 1) * MOBA_BLOCK], NEG_INF)
        m = _col_reduce(own, jnp.maximum, jnp.max)
        for sb, bias in zip(past, biases):
            mb = _col_reduce(sb, jnp.maximum, jnp.max)
            m = jnp.maximum(m, mb if bias is None else mb + bias)
        pieces = [jnp.exp2(sb + (-m if bias is None else bias - m)).astype(BF16) for sb, bias in zip(past, biases)]
        pieces.append(jnp.exp2(own - m).astype(BF16))
        return jnp.concatenate(pieces, axis=0) if i > 0 else pieces[0]

    def output(i, hh, p):
        hr = slice(hh * HEAD_DIM, (hh + 1) * HEAD_DIM)
        ot = jnp.dot(vt_ones[hh][:, 0:(i + 1) * MOBA_BLOCK], p, preferred_element_type=F32)
        ot = ot[0:HEAD_DIM] / ot[HEAD_DIM:HEAD_DIM + 1]
        ms = jnp.mean(ot * ot, axis=0, keepdims=True)
        o_ref[hr, i * qb:(i + 1) * qb] = (ot * lax.rsqrt(ms + NORM_EPS) * nwt_ref[hr, :]).astype(o_ref.dtype)

    bodies = [(i, hh) for i in range(nb) for hh in range(2)]
    ahead, behind = 2, 2
    queue = [scores(*b) for b in bodies[:ahead]]
    pending = []
    for idx, body in enumerate(bodies):
        st = queue.pop(0)
        if idx + ahead < len(bodies):
            queue.append(scores(*bodies[idx + ahead]))
        pending.append((*body, softmax(*body, st)))
        if len(pending) > behind:
            output(*pending.pop(0))
    for item in pending:
        output(*item)


def _moba_prompt(qt, k, kt, vt, nwt, *, out_dtype):
    B, W, S = qt.shape
    pair = 2 * HEAD_DIM
    trn = pl.BlockSpec((None, pair, S), lambda b, h: (b, h, 0))
    return pl.pallas_call(
        functools.partial(_moba_prompt_kernel, seq=S),
        out_shape=jax.ShapeDtypeStruct((B, W, S), out_dtype),
        grid=(B, W // pair),
        in_specs=[trn, pl.BlockSpec((None, S, pair), lambda b, h: (b, 0, h)), trn, trn,
                  pl.BlockSpec((pair, MOBA_BLOCK), lambda b, h: (h, 0))],
        out_specs=trn,
        compiler_params=_params("parallel", "parallel"),
        name="moba_prompt",
    )(qt, k, kt, vt, nwt)


def _topk_kernel(q_ref, km_ref, o_ref, *, heads):
    seqs, rows, _ = q_ref.shape
    n_blk = km_ref.shape[2]
    lane_b = lax.broadcasted_iota(jnp.int32, (seqs * heads * rows, n_blk), 1).astype(F32)
    lane_o = lax.broadcasted_iota(jnp.int32, (rows, LANES), 1)
    gate = jnp.concatenate(
        [jnp.dot(q_ref[s, :, h * HEAD_DIM:(h + 1) * HEAD_DIM], km_ref[s, h * HEAD_DIM:(h + 1) * HEAD_DIM, :],
                 preferred_element_type=F32, precision=HIGHEST) for s in range(seqs) for h in range(heads)], axis=0)
    picks = []
    for r in range(MOBA_TOPK):
        m = jnp.max(gate, axis=1, keepdims=True)
        idx = jnp.min(jnp.where(gate == m, lane_b, float(n_blk)), axis=1, keepdims=True)
        gate = jnp.where(lane_b == idx, NEG_INF, gate)
        picks.append(idx)
    for s in range(seqs):
        out = jnp.zeros((rows, LANES), F32)
        for h in range(heads):
            base = (s * heads + h) * rows
            for r in range(MOBA_TOPK):
                out = jnp.where(lane_o == h * MOBA_TOPK + r, picks[r][base:base + rows], out)
        o_ref[s] = out.astype(jnp.int32)


def _topk(q8, kmt):
    bd, rows, W = q8.shape
    n_full = kmt.shape[2]
    seqs = max(s for s in (8, 4, 2, 1) if bd % s == 0)
    return pl.pallas_call(
        functools.partial(_topk_kernel, heads=W // HEAD_DIM),
        out_shape=jax.ShapeDtypeStruct((bd, rows, LANES), jnp.int32),
        grid=(bd // seqs,),
        in_specs=[pl.BlockSpec((seqs, rows, W), lambda b: (b, 0, 0)),
                  pl.BlockSpec((seqs, W, n_full), lambda b: (b, 0, 0))],
        out_specs=pl.BlockSpec((seqs, rows, LANES), lambda b: (b, 0, 0)),
        compiler_params=_params("parallel"),
        name="moba_topk",
    )(q8, kmt)


def _attend_kernel(top_ref, pt_ref, q_ref, kn_ref, vn_ref, nw_ref, ck_ref, cv_ref, o_ref,
                   kbuf, vbuf, sem, *, heads, tokens, n_pages, page):
    b = pl.program_id(0)
    nb = pl.num_programs(0)
    ppb = MOBA_BLOCK // page
    n_sel = MOBA_TOPK * MOBA_BLOCK
    slot = b % 2

    def page_copies(seq, buf_slot):
        out = []
        for h in range(heads):
            for t in range(tokens):
                for r in range(MOBA_TOPK):
                    blk = top_ref[((seq * tokens + t) * heads + h) * MOBA_TOPK + r]
                    for p in range(ppb):
                        phys = pt_ref[seq * n_pages + blk * ppb + p]
                        dst = pl.ds((r * ppb + p) * page, page)
                        out.append(pltpu.make_async_copy(ck_ref.at[phys, h], kbuf.at[buf_slot, h, t, :, dst],
                                                         sem.at[buf_slot, 0]))
                        out.append(pltpu.make_async_copy(cv_ref.at[phys, h], vbuf.at[buf_slot, h, t, :, dst],
                                                         sem.at[buf_slot, 1]))
        return out

    @pl.when(b == 0)
    def _():
        for c in page_copies(b, slot):
            c.start()

    @pl.when(b + 1 < nb)
    def _():
        for c in page_copies(b + 1, 1 - slot):
            c.start()

    for c in page_copies(b, slot):
        c.wait()
    kb = kbuf.at[slot]
    vb = vbuf.at[slot]
    zrows = jnp.zeros((LANES - kn_ref.shape[0], LANES), F32)
    for p in range(heads // 2):
        cs = slice(p * LANES, (p + 1) * LANES)
        knt = jnp.concatenate([kn_ref[:, cs], zrows], axis=0).T
        vnt = jnp.concatenate([vn_ref[:, cs], zrows], axis=0).T
        for hh in range(2):
            hr = slice(hh * HEAD_DIM, (hh + 1) * HEAD_DIM)
            for t in range(tokens):
                kb[2 * p + hh, t, :, n_sel:n_sel + LANES] = knt[hr]
                vb[2 * p + hh, t, :, n_sel:n_sel + LANES] = vnt[hr]

    rows = q_ref.shape[0]
    col = lax.broadcasted_iota(jnp.int32, (rows, n_sel + LANES), 1)
    row = lax.broadcasted_iota(jnp.int32, (rows, HEAD_DIM), 0)
    pairs = [(h, t) for h in range(heads) for t in range(tokens)]
    qs = [q_ref[:, h * HEAD_DIM:(h + 1) * HEAD_DIM].astype(BF16) for h in range(heads)]
    scores = [jnp.where(col <= n_sel + t,
                        jnp.dot(qs[h], kb[h, t].astype(BF16), preferred_element_type=F32), NEG_INF)
              for h, t in pairs]
    probs = []
    for s in scores:
        p = jnp.exp2(s - jnp.max(s, axis=1, keepdims=True))
        probs.append((p.astype(BF16), jnp.sum(p, axis=1, keepdims=True)))
    outs = [jnp.zeros((rows, HEAD_DIM), F32)] * heads
    for (h, t), (p, l) in zip(pairs, probs):
        o = lax.dot_general(p, vb[h, t].astype(BF16), NT, preferred_element_type=F32) / l
        outs[h] = jnp.where(row == t, o, outs[h])
    o_ref[...] = jnp.concatenate([_rms(outs[h], nw_ref[:, h * HEAD_DIM:(h + 1) * HEAD_DIM])
                                  for h in range(heads)], axis=1)


def _attend(top_flat, pt_flat, q8, kn8, vn8, nw, ckt4, cvt4, *, tokens, n_pages):
    bd, rows, W = q8.shape
    heads = W // HEAD_DIM
    page = ckt4.shape[3]
    n_keys = MOBA_TOPK * MOBA_BLOCK + LANES
    any_spec = pl.BlockSpec(memory_space=pl.ANY)
    return pl.pallas_call(
        functools.partial(_attend_kernel, heads=heads, tokens=tokens, n_pages=n_pages, page=page),
        out_shape=jax.ShapeDtypeStruct((bd, rows, W), F32),
        grid_spec=pltpu.PrefetchScalarGridSpec(
            num_scalar_prefetch=2,
            grid=(bd,),
            in_specs=[pl.BlockSpec((None, rows, W), lambda b, *_: (b, 0, 0)),
                      pl.BlockSpec((None, rows, W), lambda b, *_: (b, 0, 0)),
                      pl.BlockSpec((None, rows, W), lambda b, *_: (b, 0, 0)),
                      pl.BlockSpec((1, W), lambda b, *_: (0, 0)),
                      any_spec, any_spec],
            out_specs=pl.BlockSpec((None, rows, W), lambda b, *_: (b, 0, 0)),
            scratch_shapes=[pltpu.VMEM((2, heads, tokens, HEAD_DIM, n_keys), F32),
                            pltpu.VMEM((2, heads, tokens, HEAD_DIM, n_keys), F32),
                            pltpu.SemaphoreType.DMA((2, 2))],
        ),
        compiler_params=_params("arbitrary"),
        name="moba_attend",
    )(top_flat, pt_flat, q8, kn8, vn8, nw, ckt4, cvt4)


FF_CHUNKS = 3
SIDE_SLOTS = FF_CHUNKS + 1
POOL_SLOTS = 3


def _outffn_compute(x_ref, ro_ref, mo_ref, g1_ref, sh2_ref, sc2_ref, g2_ref, n2w_ref, fw_ref,
                    wo_ref, wg_ref, wu_ref, wd_ref, y_ref, *, ff_chunks, moba_transposed, side_work):
    side_work(0)
    rw = ro_ref.shape[1]
    mo = mo_ref[...].astype(BF16)
    attn = (jnp.dot(ro_ref[...].astype(BF16), wo_ref[0:rw, :], preferred_element_type=F32)
            + lax.dot_general(mo, wo_ref[rw:, :], TN if moba_transposed else (((1,), (0,)), ((), ())),
                              preferred_element_type=F32))
    x1 = x_ref[...] + g1_ref[...] * attn
    h2 = (_rms(x1, n2w_ref[...]) * (1.0 + sc2_ref[...]) + sh2_ref[...]).astype(BF16)
    acc = None
    for c, (lo, hi) in enumerate(ff_chunks):
        side_work(1 + c)
        gate = jnp.dot(h2, wg_ref[:, lo:hi], preferred_element_type=F32)
        up = jnp.dot(h2, wu_ref[:, lo:hi], preferred_element_type=F32)
        part = jnp.dot((_silu(gate) * up).astype(BF16), wd_ref[lo:hi, :], preferred_element_type=F32)
        acc = part if acc is None else acc + part
    x2 = x1 + g2_ref[...] * acc
    y_ref[...] = _rms(x2, fw_ref[...])


def _outffn_kernel(*refs, ff_chunks, moba_transposed):
    _outffn_compute(*refs, ff_chunks=ff_chunks, moba_transposed=moba_transposed, side_work=lambda c: None)


def _outffn_pool_kernel(pt_ref, *refs, ff_chunks, moba_transposed, inner_steps, pages_per_step):
    ck_ref, y_ref, km_ref, pbuf, sem = refs[-5:]
    step = pl.program_id(0) * inner_steps + pl.program_id(1)
    n_steps = pl.num_programs(0) * inner_steps
    n_pages = pt_ref.shape[1]
    steps_per_seq = n_pages // pages_per_step
    chunk_pages = pages_per_step // SIDE_SLOTS
    ppb = MOBA_BLOCK // pbuf.shape[3]
    chunk_blocks = chunk_pages // ppb
    lane = lax.broadcasted_iota(jnp.int32, km_ref.shape, 1)

    def chunk_slot(st, c):
        return lax.rem(st * SIDE_SLOTS + c, POOL_SLOTS)

    def chunk_copies(st, c):
        seq = st // steps_per_seq
        base = (st % steps_per_seq) * pages_per_step + c * chunk_pages
        slot = chunk_slot(st, c)
        return [pltpu.make_async_copy(ck_ref.at[pt_ref[seq, base + p]], pbuf.at[slot, p], sem.at[slot])
                for p in range(chunk_pages)]

    def side_work(c):
        if c == 0:
            @pl.when(step == 0)
            def _():
                for a in range(POOL_SLOTS - 1):
                    for cp in chunk_copies(step, a):
                        cp.start()

            @pl.when(step % steps_per_seq == 0)
            def _():
                km_ref[...] = jnp.zeros_like(km_ref)
        ahead = c + POOL_SLOTS - 1
        if ahead < SIDE_SLOTS:
            for cp in chunk_copies(step, ahead):
                cp.start()
        else:
            @pl.when(step + 1 < n_steps)
            def _():
                for cp in chunk_copies(step + 1, ahead - SIDE_SLOTS):
                    cp.start()
        for cp in chunk_copies(step, c):
            cp.wait()
        slot = chunk_slot(step, c)
        first_block = (step % steps_per_seq) * (pages_per_step // ppb) + c * chunk_blocks
        acc = km_ref[...]
        for n in range(chunk_blocks):
            tot = pbuf[slot, n * ppb]
            for p in range(1, ppb):
                tot = tot + pbuf[slot, n * ppb + p]
            col = jnp.sum(tot, axis=1, keepdims=True) * (1.0 / MOBA_BLOCK)
            acc = jnp.where(lane == first_block + n, col, acc)
        km_ref[...] = acc

    _outffn_compute(*refs[:-5], y_ref, ff_chunks=ff_chunks, moba_transposed=moba_transposed, side_work=side_work)


def _outffn(x, ro, mo, g1, sh2, sc2, g2, n2w, fw, wo, wg, wu, wd, *, tm, moba_transposed, pool=None):
    G, R, D = x.shape
    W = ro.shape[2]
    dff = wg.shape[1]
    cuts = [round(dff * c / FF_CHUNKS / 256) * 256 for c in range(FF_CHUNKS)] + [dff]
    ff_chunks = tuple(zip(cuts[:-1], cuts[1:]))
    inner = R // tm
    row = pl.BlockSpec((None, tm, D), lambda g, i, *_: (g, i, 0))
    act = pl.BlockSpec((None, tm, W), lambda g, i, *_: (g, i, 0))
    act_t = pl.BlockSpec((None, W, tm), lambda g, i, *_: (g, 0, i))
    in_specs = [row, act, act_t if moba_transposed else act,
                _mod_spec(g1, tm), _mod_spec(sh2, tm), _mod_spec(sc2, tm), _mod_spec(g2, tm),
                _const_spec(n2w), _const_spec(fw), _const_spec(wo), _const_spec(wg), _const_spec(wu),
                _const_spec(wd)]
    args = (x, ro, mo, g1, sh2, sc2, g2, n2w, fw, wo, wg, wu, wd)
    y_shape = jax.ShapeDtypeStruct((G, R, D), F32)
    if pool is None:
        return pl.pallas_call(
            functools.partial(_outffn_kernel, ff_chunks=ff_chunks, moba_transposed=moba_transposed),
            out_shape=y_shape, grid=(G, inner), in_specs=in_specs, out_specs=row,
            compiler_params=_params("parallel", "parallel"),
            name="outproj_ffn",
        )(*args)
    page_table, ckt = pool
    bd, n_pages = page_table.shape
    _, hd, page = ckt.shape
    ppb = MOBA_BLOCK // page
    n_steps = G * inner
    pages_per_step = bd * n_pages // n_steps
    assert pages_per_step * n_steps == bd * n_pages and n_pages % pages_per_step == 0
    assert pages_per_step % (SIDE_SLOTS * ppb) == 0 and POOL_SLOTS - 1 <= SIDE_SLOTS
    steps_per_seq = n_pages // pages_per_step
    n_blocks = n_pages // ppb
    km_spec = pl.BlockSpec((None, hd, n_blocks), lambda g, i, *_: ((g * inner + i) // steps_per_seq, 0, 0))
    return pl.pallas_call(
        functools.partial(_outffn_pool_kernel, ff_chunks=ff_chunks, moba_transposed=moba_transposed,
                          inner_steps=inner, pages_per_step=pages_per_step),
        out_shape=[y_shape, jax.ShapeDtypeStruct((bd, hd, n_blocks), F32)],
        grid_spec=pltpu.PrefetchScalarGridSpec(
            num_scalar_prefetch=1,
            grid=(G, inner),
            in_specs=in_specs + [pl.BlockSpec(memory_space=pl.ANY)],
            out_specs=[row, km_spec],
            scratch_shapes=[pltpu.VMEM((POOL_SLOTS, pages_per_step // SIDE_SLOTS, hd, page), F32),
                            pltpu.SemaphoreType.DMA((POOL_SLOTS,))],
        ),
        compiler_params=_params("arbitrary", "arbitrary"),
        name="outproj_ffn_pool",
    )(page_table, *args, ckt)


def _rope_angles(pos):
    inv = ROPE_BASE ** (-jnp.arange(HALF, dtype=F32) / HALF)
    ang = pos.astype(F32)[:, None] * inv[None, :]
    return jnp.cos(ang), jnp.sin(ang)


def _rope_tables(pos):
    cos, sin = _rope_angles(pos)
    reps = LANES // HEAD_DIM
    return jnp.tile(cos, (1, 2 * reps)), jnp.tile(jnp.concatenate([-sin, sin], axis=1), (1, reps))


def kernel(x_prompt, x_sample, cache_k, cache_v, state_ret, page_table, c_prompt, c_sample,
           norm1_w, norm2_w, final_w, w_mod, b_mod, w_in, ret_gn_w, moba_norm_w, w_out,
           w_gate, w_up, w_down):
    Bp, S, D = x_prompt.shape
    Bd, T, _ = x_sample.shape
    depth = w_in.shape[0]
    assert depth == 1, "single decoder layer"
    n_pool, page, m_heads = cache_k.shape[1], cache_k.shape[2], cache_k.shape[3]
    n_pages = page_table.shape[1]
    past_len = n_pages * page
    assert past_len % MOBA_BLOCK == 0 and MOBA_BLOCK % page == 0
    n_full = past_len // MOBA_BLOCK
    moba_w = m_heads * HEAD_DIM
    ret_w = (w_in.shape[2] - 3 * moba_w) // 4
    assert ret_w == moba_w, "the two head groups share one projection width"
    W = ret_w
    r_heads = ret_w // HEAD_DIM
    fw = final_w.reshape(1, D)
    rows_s = Bd * T
    pad8 = SUBLANES

    w_in0 = w_in[0].astype(BF16)
    col = lambda g: w_in0[:, g * W:(g + 1) * W]
    w_tok = jnp.concatenate([col(0), col(2), col(3)], axis=1)
    w_trn = jnp.concatenate([col(1), col(4), col(5), col(6)], axis=1).T
    wo, wg, wu, wd = (w[0].astype(BF16) for w in (w_out, w_gate, w_up, w_down))
    n1w, n2w = norm1_w[0].reshape(1, D), norm2_w[0].reshape(1, D)
    gnw, mnw = ret_gn_w[0].reshape(1, ret_w), moba_norm_w[0].reshape(1, moba_w)

    mod = _modulation(jnp.concatenate([c_prompt, c_sample], axis=0), w_mod[0], b_mod[0])
    mod_p = [m[:, None, :] for m in jnp.split(mod[:Bp], 6, axis=-1)]
    mod_s = [jnp.repeat(m, T, axis=0)[None] for m in jnp.split(mod[Bp:], 6, axis=-1)]

    cos_a, sin_a = _rope_angles(jnp.arange(S))
    cos_p, sin_p = _rope_tables(jnp.arange(S))
    rq, rkt, rv, rg, mqt, mk, mkt, mvt = _inproj_prompt(x_prompt, mod_p[0], mod_p[1], n1w, cos_p, sin_p,
                                                        cos_a.T, sin_a.T, w_tok, w_trn, tm=512)
    s0_p = jnp.zeros((Bp, r_heads, HEAD_DIM, HEAD_DIM), F32)
    ret_o, ret_p = _retention(rq, rkt, rv, rg, s0_p, _decay_tables(r_heads, RET_CHUNK, RET_CHUNK, RET_CHUNK), gnw,
                              tl=512, out_dtype=BF16)
    nwt = jnp.broadcast_to(mnw.reshape(moba_w, 1), (moba_w, MOBA_BLOCK))
    moba_ot = _moba_prompt(mqt, mk, mkt, mvt, nwt, out_dtype=BF16)
    ckt4 = jnp.transpose(cache_k[0], (0, 2, 3, 1))
    cvt4 = jnp.transpose(cache_v[0], (0, 2, 3, 1))
    y_prompt, kmt = _outffn(x_prompt, ret_o, moba_ot, mod_p[2], mod_p[3], mod_p[4], mod_p[5], n2w, fw,
                            wo, wg, wu, wd, tm=512, moba_transposed=True,
                            pool=(page_table, ckt4.reshape(n_pool, moba_w, page)))
    to_rows = lambda t: t.reshape(1, Bp, m_heads, HEAD_DIM, S).transpose(0, 1, 4, 2, 3)
    k_prompt, v_prompt = to_rows(mkt), to_rows(mvt)

    xs = x_sample.reshape(1, rows_s, D)
    cos_s, sin_s = _rope_tables(past_len + jnp.arange(rows_s) % T)
    sq, sk, sv, sg, smq, smk, smv = _inproj_sample(xs, mod_s[0], mod_s[1], n1w, cos_s, sin_s, w_in0)
    rows_pad = 4 * SUBLANES
    padc = lambda t: jnp.pad(t.reshape(Bd, T, ret_w), ((0, 0), (0, rows_pad - T), (0, 0)))
    dmat2_s, qdec_s, kdect_s, cdec_s = _decay_tables(r_heads, T, rows_pad, LANES)
    ret_os, ret_s = _retention_step(padc(sq), padc(sk), padc(sv), padc(sg), state_ret[0],
                                    (dmat2_s, qdec_s, kdect_s[:, :rows_pad].T, cdec_s), gnw)
    ret_os = ret_os[:, :T].reshape(1, rows_s, ret_w)

    assert kmt.shape[2] == n_full
    pad_rows = lambda t: jnp.pad(t.reshape(Bd, T, moba_w), ((0, 0), (0, pad8 - T), (0, 0)))
    q8 = pad_rows(smq)
    top = _topk(q8, kmt)[:, :T, :m_heads * MOBA_TOPK]
    moba_os = _attend(top.reshape(-1), page_table.reshape(-1), q8, pad_rows(smk), pad_rows(smv), mnw, ckt4, cvt4,
                      tokens=T, n_pages=n_pages)
    moba_os = moba_os[:, :T].reshape(1, rows_s, moba_w)
    y_sample = _outffn(xs, ret_os, moba_os, mod_s[2], mod_s[3], mod_s[4], mod_s[5], n2w, fw,
                       wo, wg, wu, wd, tm=rows_s, moba_transposed=False).reshape(Bd, T, D)
    k_sample = smk.reshape(1, Bd, T, m_heads, HEAD_DIM)
    v_sample = smv.reshape(1, Bd, T, m_heads, HEAD_DIM)

    return (y_prompt, y_sample, k_prompt, v_prompt, ret_p[None], k_sample, v_sample, ret_s[None])
```

```python
import functools
import math

import jax
import jax.numpy as jnp
from jax import lax
from jax.experimental import pallas as pl
from jax.experimental.pallas import tpu as pltpu

F32 = jnp.float32
BF16 = jnp.bfloat16
HIGHEST = lax.Precision.HIGHEST

HEAD_DIM = 64
HALF = HEAD_DIM // 2
RET_CHUNK = 128
MOBA_BLOCK = 256
MOBA_TOPK = 3
ROPE_BASE = 10000.0
NORM_EPS = 1e-6
LANES = 128
SUBLANES = 8
MXU_WIDTH = 256
VMEM_LIMIT = 56 * 1024 * 1024
NEG_INF = float("-inf")
MOBA_QSCALE = HEAD_DIM ** -0.5 * math.log2(math.e)
NT = (((1,), (1,)), ((), ()))
TN = (((0,), (0,)), ((), ()))


def _params(*sem):
    return pltpu.CompilerParams(dimension_semantics=sem, vmem_limit_bytes=VMEM_LIMIT)


def _rms(x, w):
    return x * lax.rsqrt(jnp.mean(x * x, axis=-1, keepdims=True) + NORM_EPS) * w


def _silu(x):
    return x * jax.nn.sigmoid(x)


def _const_spec(a):
    return pl.BlockSpec(a.shape, lambda *_: (0,) * a.ndim, pipeline_mode=pl.Buffered(1))


def _mod_kernel(c_ref, w_ref, b_ref, o_ref):
    s = _silu(c_ref[...])
    o_ref[...] = jnp.dot(s, w_ref[...], preferred_element_type=F32, precision=HIGHEST) + b_ref[...]


def _modulation(c, w_mod, b_mod):
    n, d = c.shape
    cols = w_mod.shape[1]
    tn = 1536
    return pl.pallas_call(
        _mod_kernel,
        out_shape=jax.ShapeDtypeStruct((n, cols), F32),
        grid=(cols // tn,),
        in_specs=[pl.BlockSpec((n, d), lambda j: (0, 0)),
                  pl.BlockSpec((d, tn), lambda j: (0, j)),
                  pl.BlockSpec((1, tn), lambda j: (0, j))],
        out_specs=pl.BlockSpec((n, tn), lambda j: (0, j)),
        compiler_params=_params("arbitrary"),
        name="modulation",
    )(c, w_mod, b_mod.reshape(1, cols))


def _mod_spec(arr, tm):
    if arr.shape[1] == 1:
        return pl.BlockSpec((None, 1, arr.shape[2]), lambda g, i, *_: (g, 0, 0))
    return pl.BlockSpec((None, tm, arr.shape[2]), lambda g, i, *_: (g, i, 0))


def _normed_input(x_ref, sh_ref, sc_ref, nw_ref):
    return (_rms(x_ref[...], nw_ref[...]) * (1.0 + sc_ref[...]) + sh_ref[...]).astype(BF16)


def _rope_store(z, cos, sin, o_ref, scale):
    lane = lax.broadcasted_iota(jnp.int32, cos.shape, 1)
    first_half = (lane % HEAD_DIM) < HALF
    for c in range(z.shape[1] // LANES):
        zc = z[:, c * LANES:(c + 1) * LANES]
        partner = jnp.where(first_half, pltpu.roll(zc, LANES - HALF, 1), pltpu.roll(zc, HALF, 1))
        o_ref[:, c * LANES:(c + 1) * LANES] = ((zc * cos + partner * sin) * scale).astype(o_ref.dtype)


def _inproj_sample_kernel(x_ref, sh_ref, sc_ref, nw_ref, cos_ref, sin_ref, w_ref,
                          rq_ref, rk_ref, rv_ref, rg_ref, mq_ref, mk_ref, mv_ref, *, width):
    h = _normed_input(x_ref, sh_ref, sc_ref, nw_ref)
    proj = lambda g: jnp.dot(h, w_ref[:, g * width:(g + 1) * width], preferred_element_type=F32)
    _rope_store(proj(0), cos_ref[...], sin_ref[...], rq_ref, 1.0)
    _rope_store(proj(1), cos_ref[...], sin_ref[...], rk_ref, HEAD_DIM ** -0.5)
    rv_ref[...] = proj(2)
    rg_ref[...] = _silu(proj(3))
    mq_ref[...] = proj(4) * MOBA_QSCALE
    mk_ref[...] = proj(5)
    mv_ref[...] = proj(6)


def _inproj_sample(x, sh, sc, nw, cos_t, sin_t, w_bf):
    G, R, D = x.shape
    width = w_bf.shape[1] // 7
    row = pl.BlockSpec((None, R, D), lambda g, i: (g, i, 0))
    tab = pl.BlockSpec((R, LANES), lambda g, i: (i, 0))
    act = pl.BlockSpec((None, R, width), lambda g, i: (g, i, 0))
    return pl.pallas_call(
        functools.partial(_inproj_sample_kernel, width=width),
        out_shape=[jax.ShapeDtypeStruct((G, R, width), F32)] * 7,
        grid=(G, 1),
        in_specs=[row, _mod_spec(sh, R), _mod_spec(sc, R), _const_spec(nw), tab, tab, _const_spec(w_bf)],
        out_specs=[act] * 7,
        compiler_params=_params("parallel", "parallel"),
        name="inproj_sample",
    )(x, sh, sc, nw, cos_t, sin_t, w_bf)


def _inproj_prompt_kernel(x_ref, sh_ref, sc_ref, nw_ref, cos_ref, sin_ref, cost_ref, sint_ref, w_ref, wt_ref,
                          rq_ref, rkt_ref, rv_ref, rg_ref, mqt_ref, mk_ref, mkt_ref, mvt_ref, *, width):
    h = _normed_input(x_ref, sh_ref, sc_ref, nw_ref)
    proj = lambda g: jnp.dot(h, w_ref[:, g * width:(g + 1) * width], preferred_element_type=F32)
    proj_t = lambda g: lax.dot_general(wt_ref[g * width:(g + 1) * width, :], h, NT, preferred_element_type=F32)
    _rope_store(proj(0), cos_ref[...], sin_ref[...], rq_ref, 1.0)
    rv_ref[...] = proj(1).astype(rv_ref.dtype)
    rg_ref[...] = _silu(proj(2)).astype(rg_ref.dtype)
    zt = proj_t(0)
    cost = cost_ref[...]
    sint = sint_ref[...]
    scale = HEAD_DIM ** -0.5
    for hd in range(width // HEAD_DIM):
        lo = slice(hd * HEAD_DIM, hd * HEAD_DIM + HALF)
        hi = slice(hd * HEAD_DIM + HALF, (hd + 1) * HEAD_DIM)
        a, b = zt[lo], zt[hi]
        rkt_ref[lo, :] = ((a * cost - b * sint) * scale).astype(rkt_ref.dtype)
        rkt_ref[hi, :] = ((a * sint + b * cost) * scale).astype(rkt_ref.dtype)
    mqt_ref[...] = (proj_t(1) * MOBA_QSCALE).astype(mqt_ref.dtype)
    mkt = proj_t(2)
    mkt_ref[...] = mkt
    mk_ref[...] = mkt.T.astype(mk_ref.dtype)
    mvt_ref[...] = proj_t(3)


def _inproj_prompt(x, sh, sc, nw, cos_t, sin_t, cos_tt, sin_tt, w_tok, w_trn, *, tm):
    G, R, D = x.shape
    width = w_tok.shape[1] // 3
    row = pl.BlockSpec((None, tm, D), lambda g, i: (g, i, 0))
    tab = pl.BlockSpec((tm, LANES), lambda g, i: (i, 0))
    tab_t = pl.BlockSpec((HALF, tm), lambda g, i: (0, i))
    act = pl.BlockSpec((None, tm, width), lambda g, i: (g, i, 0))
    act_t = pl.BlockSpec((None, width, tm), lambda g, i: (g, 0, i))
    tok = jax.ShapeDtypeStruct((G, R, width), BF16)
    trn = lambda dt: jax.ShapeDtypeStruct((G, width, R), dt)
    return pl.pallas_call(
        functools.partial(_inproj_prompt_kernel, width=width),
        out_shape=[tok, trn(BF16), tok, tok, trn(BF16), tok, trn(F32), trn(F32)],
        grid=(G, R // tm),
        in_specs=[row, _mod_spec(sh, tm), _mod_spec(sc, tm), _const_spec(nw), tab, tab, tab_t, tab_t,
                  _const_spec(w_tok), _const_spec(w_trn)],
        out_specs=[act, act_t, act, act, act_t, act, act_t, act_t],
        compiler_params=_params("parallel", "parallel"),
        name="inproj_prompt",
    )(x, sh, sc, nw, cos_t, sin_t, cos_tt, sin_tt, w_tok, w_trn)


def _retention_kernel(q_ref, kt_ref, v_ref, g_ref, s0_ref, dmat2_ref, qdec_ref, kdect_ref, cdec_ref, gnw_ref,
                      avg_ref, o_ref, sout_ref, s_scr, o_scr, *, heads, chunks):
    j = pl.program_id(1)
    pairs = heads // 2
    lane = lax.broadcasted_iota(jnp.int32, (LANES, LANES), 1)
    sub = lax.broadcasted_iota(jnp.int32, (LANES, LANES), 0)
    first = lane < HEAD_DIM
    diag = first == (sub < HEAD_DIM)
    keep_a = jnp.where(first, 1.0, 0.0).astype(BF16)
    keep_b = jnp.where(first, 0.0, 1.0).astype(BF16)

    @pl.when(j == 0)
    def _():
        z = jnp.zeros((HEAD_DIM, HEAD_DIM), F32)
        for p in range(pairs):
            s_scr[p] = jnp.concatenate([jnp.concatenate([s0_ref[2 * p], z], axis=1),
                                        jnp.concatenate([z, s0_ref[2 * p + 1]], axis=1)], axis=0)

    gnw = gnw_ref[...]
    avg = avg_ref[...]

    group_mean = functools.partial(_group_mean, avg=avg)

    for c in range(chunks):
        rows = slice(c * RET_CHUNK, (c + 1) * RET_CHUNK)
        for p in range(pairs):
            cs = slice(p * LANES, (p + 1) * LANES)
            qp = q_ref[rows, cs].astype(BF16)
            vp = v_ref[rows, cs].astype(BF16)
            ktp = kt_ref[cs, rows]
            q2 = jnp.concatenate([qp * keep_a, qp * keep_b], axis=0)
            att2 = jnp.dot(q2, ktp.astype(BF16), preferred_element_type=F32) * dmat2_ref[p]
            o2 = jnp.dot(att2.astype(BF16), vp, preferred_element_type=F32)
            s = s_scr[p]
            o = (jnp.where(first, o2[:RET_CHUNK], o2[RET_CHUNK:])
                 + jnp.dot(qp, s.astype(BF16), preferred_element_type=F32) * qdec_ref[:, cs])
            kdt = (ktp.astype(F32) * kdect_ref[cs, :]).astype(BF16)
            s_scr[p] = s * cdec_ref[p] + jnp.where(diag, jnp.dot(kdt, vp, preferred_element_type=F32), 0.0)
            o_scr[rows, cs] = o

    o_all = o_scr[...]
    d = o_all - group_mean(o_all)
    var = group_mean(d * d)
    o_ref[...] = (g_ref[...].astype(F32) * (d * lax.rsqrt(var + NORM_EPS) * gnw)).astype(o_ref.dtype)

    @pl.when(j == pl.num_programs(1) - 1)
    def _():
        for p in range(pairs):
            s = s_scr[p]
            sout_ref[2 * p] = s[:HEAD_DIM, :HEAD_DIM]
            sout_ref[2 * p + 1] = s[HEAD_DIM:, HEAD_DIM:]


def _retention(q, kt, v, gs, s0, tables, gnw, *, tl, out_dtype):
    B, L, W = q.shape
    heads = W // HEAD_DIM
    seq = pl.BlockSpec((None, tl, W), lambda b, j: (b, j, 0))
    seq_t = pl.BlockSpec((None, W, tl), lambda b, j: (b, 0, j))
    st = pl.BlockSpec((None, heads, HEAD_DIM, HEAD_DIM), lambda b, j: (b, 0, 0, 0))
    return pl.pallas_call(
        functools.partial(_retention_kernel, heads=heads, chunks=tl // RET_CHUNK),
        out_shape=[jax.ShapeDtypeStruct((B, L, W), out_dtype),
                   jax.ShapeDtypeStruct((B, heads, HEAD_DIM, HEAD_DIM), F32)],
        grid=(B, L // tl),
        in_specs=[seq, seq_t, seq, seq, st] + [_const_spec(t) for t in tables] + [_const_spec(gnw),
                                                                                   _const_spec(_head_average(W))],
        out_specs=[seq, st],
        scratch_shapes=[pltpu.VMEM((heads // 2, LANES, LANES), F32), pltpu.VMEM((tl, W), F32)],
        compiler_params=_params("parallel", "arbitrary"),
        name="retention",
    )(q, kt, v, gs, s0, *tables, gnw, _head_average(W))


def _retention_step_kernel(q_ref, k_ref, v_ref, g_ref, s0_ref, dmat2_ref, qdec_ref, kdec_ref, cdec_ref, gnw_ref,
                           avg_ref, o_ref, sout_ref, *, heads):
    seqs, rows, _ = q_ref.shape
    pairs = heads // 2
    lane = lax.broadcasted_iota(jnp.int32, (LANES, LANES), 1)
    sub = lax.broadcasted_iota(jnp.int32, (LANES, LANES), 0)
    diag = (lane < HEAD_DIM) == (sub < HEAD_DIM)
    first = lax.broadcasted_iota(jnp.int32, (rows, LANES), 1) < HEAD_DIM
    keep_a = jnp.where(first, 1.0, 0.0).astype(BF16)
    keep_b = jnp.where(first, 0.0, 1.0).astype(BF16)
    zpad = jnp.zeros((LANES - rows, LANES), BF16)
    zs = jnp.zeros((HEAD_DIM, HEAD_DIM), F32)
    avg = avg_ref[...]
    gnw = gnw_ref[...]

    group_mean = functools.partial(_group_mean, avg=avg)

    for s in range(seqs):
        outs = []
        for p in range(pairs):
            cs = slice(p * LANES, (p + 1) * LANES)
            qp = q_ref[s, :, cs].astype(BF16)
            kp = k_ref[s, :, cs]
            pad = lambda t: jnp.concatenate([t.astype(BF16), zpad], axis=0)
            kpad, vpad, kdpad = pad(kp), pad(v_ref[s, :, cs]), pad(kp * kdec_ref[:, cs])
            q2 = jnp.concatenate([qp * keep_a, qp * keep_b], axis=0)
            att2 = lax.dot_general(q2, kpad, NT, preferred_element_type=F32) * dmat2_ref[p]
            o2 = jnp.dot(att2.astype(BF16), vpad, preferred_element_type=F32)
            state = jnp.concatenate([jnp.concatenate([s0_ref[s, 2 * p], zs], axis=1),
                                     jnp.concatenate([zs, s0_ref[s, 2 * p + 1]], axis=1)], axis=0)
            outs.append(jnp.where(first, o2[:rows], o2[rows:])
                        + jnp.dot(qp, state.astype(BF16), preferred_element_type=F32) * qdec_ref[:, cs])
            state = state * cdec_ref[p] + jnp.where(
                diag, lax.dot_general(kdpad, vpad, TN, preferred_element_type=F32), 0.0)
            sout_ref[s, 2 * p] = state[:HEAD_DIM, :HEAD_DIM]
            sout_ref[s, 2 * p + 1] = state[HEAD_DIM:, HEAD_DIM:]
        o_all = jnp.concatenate(outs, axis=1)
        d = o_all - group_mean(o_all)
        var = group_mean(d * d)
        o_ref[s] = g_ref[s] * (d * lax.rsqrt(var + NORM_EPS) * gnw)


def _retention_step(q, k, v, gs, s0, tables, gnw):
    bd, rows, W = q.shape
    heads = W // HEAD_DIM
    seqs = max(s for s in (8, 4, 2, 1) if bd % s == 0)
    seq = pl.BlockSpec((seqs, rows, W), lambda b: (b, 0, 0))
    st = pl.BlockSpec((seqs, heads, HEAD_DIM, HEAD_DIM), lambda b: (b, 0, 0, 0))
    consts = list(tables) + [gnw, _head_average(W)]
    return pl.pallas_call(
        functools.partial(_retention_step_kernel, heads=heads),
        out_shape=[jax.ShapeDtypeStruct((bd, rows, W), F32),
                   jax.ShapeDtypeStruct((bd, heads, HEAD_DIM, HEAD_DIM), F32)],
        grid=(bd // seqs,),
        in_specs=[seq, seq, seq, seq, st] + [_const_spec(t) for t in consts],
        out_specs=[seq, st],
        compiler_params=_params("parallel"),
        name="retention_step",
    )(q, k, v, gs, s0, *consts)


def _group_mean(t, avg):
    hi = t.astype(BF16)
    lo = (t - hi.astype(F32)).astype(BF16)
    w = avg.shape[0]
    cols = [jnp.dot(hi[:, c:c + w], avg, preferred_element_type=F32)
            + jnp.dot(lo[:, c:c + w], avg, preferred_element_type=F32) for c in range(0, t.shape[1], w)]
    return jnp.concatenate(cols, axis=1) if len(cols) > 1 else cols[0]


def _head_average(width):
    width = min(width, MXU_WIDTH)
    r = jnp.arange(width) // HEAD_DIM
    return jnp.where(r[:, None] == r[None, :], 1.0 / HEAD_DIM, 0.0).astype(BF16)


def _decay_tables(heads, c_len, q_pad, k_pad):
    log_g = jnp.log1p(-jnp.exp2(-5.0 - jnp.arange(heads, dtype=F32)))
    idx = jnp.arange(c_len, dtype=F32)
    diff = idx[:, None] - idx[None, :]
    dmat = jnp.where(diff >= 0, jnp.exp(jnp.maximum(diff, 0.0) * log_g[:, None, None]), 0.0)
    q_dec = jnp.exp((idx + 1.0) * log_g[:, None])
    k_dec = jnp.exp((c_len - 1.0 - idx) * log_g[:, None])
    c_dec = jnp.exp(c_len * log_g)
    qp, kp = q_pad - c_len, k_pad - c_len
    dmat2 = jnp.pad(dmat, ((0, 0), (0, qp), (0, kp))).reshape(heads // 2, 2 * q_pad, k_pad)
    qdec = jnp.pad(jnp.repeat(q_dec.T, HEAD_DIM, axis=1), ((0, qp), (0, 0)))
    kdect = jnp.pad(jnp.repeat(k_dec, HEAD_DIM, axis=0), ((0, 0), (0, kp)))
    cdec = jnp.broadcast_to(jnp.repeat(c_dec, HEAD_DIM).reshape(heads // 2, 2 * HEAD_DIM, 1),
                            (heads // 2, 2 * HEAD_DIM, 2 * HEAD_DIM))
    return dmat2, qdec, kdect, cdec


def _col_reduce(x, op, final):
    while x.shape[0] % (2 * SUBLANES) == 0:
        half = x.shape[0] // 2
        x = op(x[:half], x[half:])
    return final(x, axis=0, keepdims=True)


def _moba_prompt_kernel(qt_ref, k_ref, kt_ref, vt_ref, nwt_ref, o_ref, *, seq):
    nb = seq // MOBA_BLOCK
    nbp = -(-nb // SUBLANES) * SUBLANES
    qb = MOBA_BLOCK
    pair = 2 * HEAD_DIM
    if nb > MOBA_TOPK + 1:
        lane_n = lax.broadcasted_iota(jnp.int32, (pair, LANES), 1)
        kmt = jnp.zeros((pair, LANES), F32)
        for n in range(nb):
            tot = kt_ref[:, n * MOBA_BLOCK:n * MOBA_BLOCK + LANES]
            for c in range(1, MOBA_BLOCK // LANES):
                tot = tot + kt_ref[:, n * MOBA_BLOCK + c * LANES:n * MOBA_BLOCK + (c + 1) * LANES]
            kmt = jnp.where(lane_n == n, jnp.sum(tot, axis=1, keepdims=True) * (1.0 / MOBA_BLOCK), kmt)
        km = kmt.T[0:nbp]
        lane_k = lax.broadcasted_iota(jnp.int32, km.shape, 1)
        km_heads = [jnp.where(lane_k < HEAD_DIM, km, 0.0), jnp.where(lane_k < HEAD_DIM, 0.0, km)]
        blk = lax.broadcasted_iota(jnp.int32, (nbp, qb), 0)
    key_i = lax.broadcasted_iota(jnp.int32, (qb, qb), 0)
    qry_i = lax.broadcasted_iota(jnp.int32, (qb, qb), 1)
    causal = key_i <= qry_i
    zeros = jnp.zeros((HEAD_DIM, qb), BF16)
    ones = jnp.ones((2 * SUBLANES, seq), BF16)
    vt_ones = [jnp.concatenate([vt_ref[hh * HEAD_DIM:(hh + 1) * HEAD_DIM, :].astype(BF16), ones], axis=0)
               for hh in range(2)]

    def scores(i, hh):
        qt = qt_ref[:, i * qb:(i + 1) * qb]
        hr = slice(hh * HEAD_DIM, (hh + 1) * HEAD_DIM)
        qm = jnp.concatenate([qt[hr], zeros] if hh == 0 else [zeros, qt[hr]], axis=0)
        return jnp.dot(k_ref[0:(i + 1) * MOBA_BLOCK, :], qm, preferred_element_type=F32)

    def softmax(i, hh, st):
        biases = [None] * i
        if i > MOBA_TOPK:
            qt = qt_ref[:, i * qb:(i + 1) * qb].astype(F32)
            gate = jnp.dot(km_heads[hh], qt, preferred_element_type=F32, precision=HIGHEST)
            valid = blk < i
            for n in range(i):
                gn = gate[n:n + 1, :]
                ahead = valid & ((gate > gn) | ((gate == gn) & (blk < n)))
                rank = jnp.sum(ahead.astype(F32), axis=0, keepdims=True)
                biases[n] = jnp.where(rank < MOBA_TOPK, 0.0, NEG_INF)
        past = [st[n * MOBA_BLOCK:(n + 1) * MOBA_BLOCK] for n in range(i)]
        own = jnp.where(causal, st[i * MOBA_BLOCK:(i + 1) * MOBA_BLOCK], NEG_INF)
        m = _col_reduce(own, jnp.maximum, jnp.max)
        for sb, bias in zip(past, biases):
            mb = _col_reduce(sb, jnp.maximum, jnp.max)
            m = jnp.maximum(m, mb if bias is None else mb + bias)
        pieces = [jnp.exp2(sb + (-m if bias is None else bias - m)).astype(BF16) for sb, bias in zip(past, biases)]
        pieces.append(jnp.exp2(own - m).astype(BF16))
        return jnp.concatenate(pieces, axis=0) if i > 0 else pieces[0]

    def output(i, hh, p):
        hr = slice(hh * HEAD_DIM, (hh + 1) * HEAD_DIM)
        ot = jnp.dot(vt_ones[hh][:, 0:(i + 1) * MOBA_BLOCK], p, preferred_element_type=F32)
        ot = ot[0:HEAD_DIM] / ot[HEAD_DIM:HEAD_DIM + 1]
        ms = jnp.mean(ot * ot, axis=0, keepdims=True)
        o_ref[hr, i * qb:(i + 1) * qb] = (ot * lax.rsqrt(ms + NORM_EPS) * nwt_ref[hr, :]).astype(o_ref.dtype)

    bodies = [(i, hh) for i in range(nb) for hh in range(2)]
    ahead, behind = 2, 2
    queue = [scores(*b) for b in bodies[:ahead]]
    pending = []
    for idx, body in enumerate(bodies):
        st = queue.pop(0)
        if idx + ahead < len(bodies):
            queue.append(scores(*bodies[idx + ahead]))
        pending.append((*body, softmax(*body, st)))
        if len(pending) > behind:
            output(*pending.pop(0))
    for item in pending:
        output(*item)


def _moba_prompt(qt, k, kt, vt, nwt, *, out_dtype):
    B, W, S = qt.shape
    pair = 2 * HEAD_DIM
    trn = pl.BlockSpec((None, pair, S), lambda b, h: (b, h, 0))
    return pl.pallas_call(
        functools.partial(_moba_prompt_kernel, seq=S),
        out_shape=jax.ShapeDtypeStruct((B, W, S), out_dtype),
        grid=(B, W // pair),
        in_specs=[trn, pl.BlockSpec((None, S, pair), lambda b, h: (b, 0, h)), trn, trn,
                  pl.BlockSpec((pair, MOBA_BLOCK), lambda b, h: (h, 0))],
        out_specs=trn,
        compiler_params=_params("parallel", "parallel"),
        name="moba_prompt",
    )(qt, k, kt, vt, nwt)


def _topk_kernel(q_ref, km_ref, o_ref, *, heads):
    seqs, rows, _ = q_ref.shape
    n_blk = km_ref.shape[2]
    lane_b = lax.broadcasted_iota(jnp.int32, (seqs * heads * rows, n_blk), 1).astype(F32)
    lane_o = lax.broadcasted_iota(jnp.int32, (rows, LANES), 1)
    gate = jnp.concatenate(
        [jnp.dot(q_ref[s, :, h * HEAD_DIM:(h + 1) * HEAD_DIM], km_ref[s, h * HEAD_DIM:(h + 1) * HEAD_DIM, :],
                 preferred_element_type=F32, precision=HIGHEST) for s in range(seqs) for h in range(heads)], axis=0)
    picks = []
    for r in range(MOBA_TOPK):
        m = jnp.max(gate, axis=1, keepdims=True)
        idx = jnp.min(jnp.where(gate == m, lane_b, float(n_blk)), axis=1, keepdims=True)
        gate = jnp.where(lane_b == idx, NEG_INF, gate)
        picks.append(idx)
    for s in range(seqs):
        out = jnp.zeros((rows, LANES), F32)
        for h in range(heads):
            base = (s * heads + h) * rows
            for r in range(MOBA_TOPK):
                out = jnp.where(lane_o == h * MOBA_TOPK + r, picks[r][base:base + rows], out)
        o_ref[s] = out.astype(jnp.int32)


def _topk(q8, kmt):
    bd, rows, W = q8.shape
    n_full = kmt.shape[2]
    seqs = max(s for s in (8, 4, 2, 1) if bd % s == 0)
    return pl.pallas_call(
        functools.partial(_topk_kernel, heads=W // HEAD_DIM),
        out_shape=jax.ShapeDtypeStruct((bd, rows, LANES), jnp.int32),
        grid=(bd // seqs,),
        in_specs=[pl.BlockSpec((seqs, rows, W), lambda b: (b, 0, 0)),
                  pl.BlockSpec((seqs, W, n_full), lambda b: (b, 0, 0))],
        out_specs=pl.BlockSpec((seqs, rows, LANES), lambda b: (b, 0, 0)),
        compiler_params=_params("parallel"),
        name="moba_topk",
    )(q8, kmt)


def _attend_kernel(top_ref, pt_ref, q_ref, kn_ref, vn_ref, nw_ref, ck_ref, cv_ref, o_ref,
                   kbuf, vbuf, sem, *, heads, tokens, n_pages, page):
    b = pl.program_id(0)
    nb = pl.num_programs(0)
    ppb = MOBA_BLOCK // page
    n_sel = MOBA_TOPK * MOBA_BLOCK
    slot = b % 2

    def page_copies(seq, buf_slot):
        out = []
        for h in range(heads):
            for t in range(tokens):
                for r in range(MOBA_TOPK):
                    blk = top_ref[((seq * tokens + t) * heads + h) * MOBA_TOPK + r]
                    for p in range(ppb):
                        phys = pt_ref[seq * n_pages + blk * ppb + p]
                        dst = pl.ds((r * ppb + p) * page, page)
                        out.append(pltpu.make_async_copy(ck_ref.at[phys, h], kbuf.at[buf_slot, h, t, :, dst],
                                                         sem.at[buf_slot, 0]))
                        out.append(pltpu.make_async_copy(cv_ref.at[phys, h], vbuf.at[buf_slot, h, t, :, dst],
                                                         sem.at[buf_slot, 1]))
        return out

    @pl.when(b == 0)
    def _():
        for c in page_copies(b, slot):
            c.start()

    @pl.when(b + 1 < nb)
    def _():
        for c in page_copies(b + 1, 1 - slot):
            c.start()

    for c in page_copies(b, slot):
        c.wait()
    kb = kbuf.at[slot]
    vb = vbuf.at[slot]
    zrows = jnp.zeros((LANES - kn_ref.shape[0], LANES), F32)
    for p in range(heads // 2):
        cs = slice(p * LANES, (p + 1) * LANES)
        knt = jnp.concatenate([kn_ref[:, cs], zrows], axis=0).T
        vnt = jnp.concatenate([vn_ref[:, cs], zrows], axis=0).T
        for hh in range(2):
            hr = slice(hh * HEAD_DIM, (hh + 1) * HEAD_DIM)
            for t in range(tokens):
                kb[2 * p + hh, t, :, n_sel:n_sel + LANES] = knt[hr]
                vb[2 * p + hh, t, :, n_sel:n_sel + LANES] = vnt[hr]

    rows = q_ref.shape[0]
    col = lax.broadcasted_iota(jnp.int32, (rows, n_sel + LANES), 1)
    row = lax.broadcasted_iota(jnp.int32, (rows, HEAD_DIM), 0)
    pairs = [(h, t) for h in range(heads) for t in range(tokens)]
    qs = [q_ref[:, h * HEAD_DIM:(h + 1) * HEAD_DIM].astype(BF16) for h in range(heads)]
    scores = [jnp.where(col <= n_sel + t,
                        jnp.dot(qs[h], kb[h, t].astype(BF16), preferred_element_type=F32), NEG_INF)
              for h, t in pairs]
    probs = []
    for s in scores:
        p = jnp.exp2(s - jnp.max(s, axis=1, keepdims=True))
        probs.append((p.astype(BF16), jnp.sum(p, axis=1, keepdims=True)))
    outs = [jnp.zeros((rows, HEAD_DIM), F32)] * heads
    for (h, t), (p, l) in zip(pairs, probs):
        o = lax.dot_general(p, vb[h, t].astype(BF16), NT, preferred_element_type=F32) / l
        outs[h] = jnp.where(row == t, o, outs[h])
    o_ref[...] = jnp.concatenate([_rms(outs[h], nw_ref[:, h * HEAD_DIM:(h + 1) * HEAD_DIM])
                                  for h in range(heads)], axis=1)


def _attend(top_flat, pt_flat, q8, kn8, vn8, nw, ckt4, cvt4, *, tokens, n_pages):
    bd, rows, W = q8.shape
    heads = W // HEAD_DIM
    page = ckt4.shape[3]
    n_keys = MOBA_TOPK * MOBA_BLOCK + LANES
    any_spec = pl.BlockSpec(memory_space=pl.ANY)
    return pl.pallas_call(
        functools.partial(_attend_kernel, heads=heads, tokens=tokens, n_pages=n_pages, page=page),
        out_shape=jax.ShapeDtypeStruct((bd, rows, W), F32),
        grid_spec=pltpu.PrefetchScalarGridSpec(
            num_scalar_prefetch=2,
            grid=(bd,),
            in_specs=[pl.BlockSpec((None, rows, W), lambda b, *_: (b, 0, 0)),
                      pl.BlockSpec((None, rows, W), lambda b, *_: (b, 0, 0)),
                      pl.BlockSpec((None, rows, W), lambda b, *_: (b, 0, 0)),
                      pl.BlockSpec((1, W), lambda b, *_: (0, 0)),
                      any_spec, any_spec],
            out_specs=pl.BlockSpec((None, rows, W), lambda b, *_: (b, 0, 0)),
            scratch_shapes=[pltpu.VMEM((2, heads, tokens, HEAD_DIM, n_keys), F32),
                            pltpu.VMEM((2, heads, tokens, HEAD_DIM, n_keys), F32),
                            pltpu.SemaphoreType.DMA((2, 2))],
        ),
        compiler_params=_params("arbitrary"),
        name="moba_attend",
    )(top_flat, pt_flat, q8, kn8, vn8, nw, ckt4, cvt4)


FF_CHUNKS = 3
SIDE_SLOTS = FF_CHUNKS + 1
POOL_SLOTS = 3


def _outffn_compute(x_ref, ro_ref, mo_ref, g1_ref, sh2_ref, sc2_ref, g2_ref, n2w_ref, fw_ref,
                    wo_ref, wg_ref, wu_ref, wd_ref, y_ref, *, ff_chunks, moba_transposed, side_work):
    side_work(0)
    rw = ro_ref.shape[1]
    mo = mo_ref[...].astype(BF16)
    attn = (jnp.dot(ro_ref[...].astype(BF16), wo_ref[0:rw, :], preferred_element_type=F32)
            + lax.dot_general(mo, wo_ref[rw:, :], TN if moba_transposed else (((1,), (0,)), ((), ())),
                              preferred_element_type=F32))
    x1 = x_ref[...] + g1_ref[...] * attn
    h2 = (_rms(x1, n2w_ref[...]) * (1.0 + sc2_ref[...]) + sh2_ref[...]).astype(BF16)
    acc = None
    for c, (lo, hi) in enumerate(ff_chunks):
        side_work(1 + c)
        gate = jnp.dot(h2, wg_ref[:, lo:hi], preferred_element_type=F32)
        up = jnp.dot(h2, wu_ref[:, lo:hi], preferred_element_type=F32)
        part = jnp.dot((_silu(gate) * up).astype(BF16), wd_ref[lo:hi, :], preferred_element_type=F32)
        acc = part if acc is None else acc + part
    x2 = x1 + g2_ref[...] * acc
    y_ref[...] = _rms(x2, fw_ref[...])


def _outffn_kernel(*refs, ff_chunks, moba_transposed):
    _outffn_compute(*refs, ff_chunks=ff_chunks, moba_transposed=moba_transposed, side_work=lambda c: None)


def _outffn_pool_kernel(pt_ref, *refs, ff_chunks, moba_transposed, inner_steps, pages_per_step):
    ck_ref, y_ref, km_ref, pbuf, sem = refs[-5:]
    step = pl.program_id(0) * inner_steps + pl.program_id(1)
    n_steps = pl.num_programs(0) * inner_steps
    n_pages = pt_ref.shape[1]
    steps_per_seq = n_pages // pages_per_step
    chunk_pages = pages_per_step // SIDE_SLOTS
    ppb = MOBA_BLOCK // pbuf.shape[3]
    chunk_blocks = chunk_pages // ppb
    lane = lax.broadcasted_iota(jnp.int32, km_ref.shape, 1)

    def chunk_slot(st, c):
        return lax.rem(st * SIDE_SLOTS + c, POOL_SLOTS)

    def chunk_copies(st, c):
        seq = st // steps_per_seq
        base = (st % steps_per_seq) * pages_per_step + c * chunk_pages
        slot = chunk_slot(st, c)
        return [pltpu.make_async_copy(ck_ref.at[pt_ref[seq, base + p]], pbuf.at[slot, p], sem.at[slot])
                for p in range(chunk_pages)]

    def side_work(c):
        if c == 0:
            @pl.when(step == 0)
            def _():
                for a in range(POOL_SLOTS - 1):
                    for cp in chunk_copies(step, a):
                        cp.start()

            @pl.when(step % steps_per_seq == 0)
            def _():
                km_ref[...] = jnp.zeros_like(km_ref)
        ahead = c + POOL_SLOTS - 1
        if ahead < SIDE_SLOTS:
            for cp in chunk_copies(step, ahead):
                cp.start()
        else:
            @pl.when(step + 1 < n_steps)
            def _():
                for cp in chunk_copies(step + 1, ahead - SIDE_SLOTS):
                    cp.start()
        for cp in chunk_copies(step, c):
            cp.wait()
        slot = chunk_slot(step, c)
        first_block = (step % steps_per_seq) * (pages_per_step // ppb) + c * chunk_blocks
        acc = km_ref[...]
        for n in range(chunk_blocks):
            tot = pbuf[slot, n * ppb]
            for p in range(1, ppb):
                tot = tot + pbuf[slot, n * ppb + p]
            col = jnp.sum(tot, axis=1, keepdims=True) * (1.0 / MOBA_BLOCK)
            acc = jnp.where(lane == first_block + n, col, acc)
        km_ref[...] = acc

    _outffn_compute(*refs[:-5], y_ref, ff_chunks=ff_chunks, moba_transposed=moba_transposed, side_work=side_work)


def _outffn(x, ro, mo, g1, sh2, sc2, g2, n2w, fw, wo, wg, wu, wd, *, tm, moba_transposed, pool=None):
    G, R, D = x.shape
    W = ro.shape[2]
    dff = wg.shape[1]
    cuts = [round(dff * c / FF_CHUNKS / 256) * 256 for c in range(FF_CHUNKS)] + [dff]
    ff_chunks = tuple(zip(cuts[:-1], cuts[1:]))
    inner = R // tm
    row = pl.BlockSpec((None, tm, D), lambda g, i, *_: (g, i, 0))
    act = pl.BlockSpec((None, tm, W), lambda g, i, *_: (g, i, 0))
    act_t = pl.BlockSpec((None, W, tm), lambda g, i, *_: (g, 0, i))
    in_specs = [row, act, act_t if moba_transposed else act,
                _mod_spec(g1, tm), _mod_spec(sh2, tm), _mod_spec(sc2, tm), _mod_spec(g2, tm),
                _const_spec(n2w), _const_spec(fw), _const_spec(wo), _const_spec(wg), _const_spec(wu),
                _const_spec(wd)]
    args = (x, ro, mo, g1, sh2, sc2, g2, n2w, fw, wo, wg, wu, wd)
    y_shape = jax.ShapeDtypeStruct((G, R, D), F32)
    if pool is None:
        return pl.pallas_call(
            functools.partial(_outffn_kernel, ff_chunks=ff_chunks, moba_transposed=moba_transposed),
            out_shape=y_shape, grid=(G, inner), in_specs=in_specs, out_specs=row,
            compiler_params=_params("parallel", "parallel"),
            name="outproj_ffn",
        )(*args)
    page_table, ckt = pool
    bd, n_pages = page_table.shape
    _, hd, page = ckt.shape
    ppb = MOBA_BLOCK // page
    n_steps = G * inner
    pages_per_step = bd * n_pages // n_steps
    assert pages_per_step * n_steps == bd * n_pages and n_pages % pages_per_step == 0
    assert pages_per_step % (SIDE_SLOTS * ppb) == 0 and POOL_SLOTS - 1 <= SIDE_SLOTS
    steps_per_seq = n_pages // pages_per_step
    n_blocks = n_pages // ppb
    km_spec = pl.BlockSpec((None, hd, n_blocks), lambda g, i, *_: ((g * inner + i) // steps_per_seq, 0, 0))
    return pl.pallas_call(
        functools.partial(_outffn_pool_kernel, ff_chunks=ff_chunks, moba_transposed=moba_transposed,
                          inner_steps=inner, pages_per_step=pages_per_step),
        out_shape=[y_shape, jax.ShapeDtypeStruct((bd, hd, n_blocks), F32)],
        grid_spec=pltpu.PrefetchScalarGridSpec(
            num_scalar_prefetch=1,
            grid=(G, inner),
            in_specs=in_specs + [pl.BlockSpec(memory_space=pl.ANY)],
            out_specs=[row, km_spec],
            scratch_shapes=[pltpu.VMEM((POOL_SLOTS, pages_per_step // SIDE_SLOTS, hd, page), F32),
                            pltpu.SemaphoreType.DMA((POOL_SLOTS,))],
        ),
        compiler_params=_params("arbitrary", "arbitrary"),
        name="outproj_ffn_pool",
    )(page_table, *args, ckt)


def _rope_angles(pos):
    inv = ROPE_BASE ** (-jnp.arange(HALF, dtype=F32) / HALF)
    ang = pos.astype(F32)[:, None] * inv[None, :]
    return jnp.cos(ang), jnp.sin(ang)


def _rope_tables(pos):
    cos, sin = _rope_angles(pos)
    reps = LANES // HEAD_DIM
    return jnp.tile(cos, (1, 2 * reps)), jnp.tile(jnp.concatenate([-sin, sin], axis=1), (1, reps))


def kernel(x_prompt, x_sample, cache_k, cache_v, state_ret, page_table, c_prompt, c_sample,
           norm1_w, norm2_w, final_w, w_mod, b_mod, w_in, ret_gn_w, moba_norm_w, w_out,
           w_gate, w_up, w_down):
    Bp, S, D = x_prompt.shape
    Bd, T, _ = x_sample.shape
    depth = w_in.shape[0]
    assert depth == 1, "single decoder layer"
    n_pool, page, m_heads = cache_k.shape[1], cache_k.shape[2], cache_k.shape[3]
    n_pages = page_table.shape[1]
    past_len = n_pages * page
    assert past_len % MOBA_BLOCK == 0 and MOBA_BLOCK % page == 0
    n_full = past_len // MOBA_BLOCK
    moba_w = m_heads * HEAD_DIM
    ret_w = (w_in.shape[2] - 3 * moba_w) // 4
    assert ret_w == moba_w, "the two head groups share one projection width"
    W = ret_w
    r_heads = ret_w // HEAD_DIM
    fw = final_w.reshape(1, D)
    rows_s = Bd * T
    pad8 = SUBLANES

    w_in0 = w_in[0].astype(BF16)
    col = lambda g: w_in0[:, g * W:(g + 1) * W]
    w_tok = jnp.concatenate([col(0), col(2), col(3)], axis=1)
    w_trn = jnp.concatenate([col(1), col(4), col(5), col(6)], axis=1).T
    wo, wg, wu, wd = (w[0].astype(BF16) for w in (w_out, w_gate, w_up, w_down))
    n1w, n2w = norm1_w[0].reshape(1, D), norm2_w[0].reshape(1, D)
    gnw, mnw = ret_gn_w[0].reshape(1, ret_w), moba_norm_w[0].reshape(1, moba_w)

    mod = _modulation(jnp.concatenate([c_prompt, c_sample], axis=0), w_mod[0], b_mod[0])
    mod_p = [m[:, None, :] for m in jnp.split(mod[:Bp], 6, axis=-1)]
    mod_s = [jnp.repeat(m, T, axis=0)[None] for m in jnp.split(mod[Bp:], 6, axis=-1)]

    cos_a, sin_a = _rope_angles(jnp.arange(S))
    cos_p, sin_p = _rope_tables(jnp.arange(S))
    rq, rkt, rv, rg, mqt, mk, mkt, mvt = _inproj_prompt(x_prompt, mod_p[0], mod_p[1], n1w, cos_p, sin_p,
                                                        cos_a.T, sin_a.T, w_tok, w_trn, tm=1024)
    s0_p = jnp.zeros((Bp, r_heads, HEAD_DIM, HEAD_DIM), F32)
    ret_o, ret_p = _retention(rq, rkt, rv, rg, s0_p, _decay_tables(r_heads, RET_CHUNK, RET_CHUNK, RET_CHUNK), gnw,
                              tl=1024, out_dtype=BF16)
    nwt = jnp.broadcast_to(mnw.reshape(moba_w, 1), (moba_w, MOBA_BLOCK))
    moba_ot = _moba_prompt(mqt, mk, mkt, mvt, nwt, out_dtype=BF16)
    ckt4 = jnp.transpose(cache_k[0], (0, 2, 3, 1))
    cvt4 = jnp.transpose(cache_v[0], (0, 2, 3, 1))
    y_prompt, kmt = _outffn(x_prompt, ret_o, moba_ot, mod_p[2], mod_p[3], mod_p[4], mod_p[5], n2w, fw,
                            wo, wg, wu, wd, tm=512, moba_transposed=True,
                            pool=(page_table, ckt4.reshape(n_pool, moba_w, page)))
    to_rows = lambda t: t.reshape(1, Bp, m_heads, HEAD_DIM, S).transpose(0, 1, 4, 2, 3)
    k_prompt, v_prompt = to_rows(mkt), to_rows(mvt)

    xs = x_sample.reshape(1, rows_s, D)
    cos_s, sin_s = _rope_tables(past_len + jnp.arange(rows_s) % T)
    sq, sk, sv, sg, smq, smk, smv = _inproj_sample(xs, mod_s[0], mod_s[1], n1w, cos_s, sin_s, w_in0)
    rows_pad = 4 * SUBLANES
    padc = lambda t: jnp.pad(t.reshape(Bd, T, ret_w), ((0, 0), (0, rows_pad - T), (0, 0)))
    dmat2_s, qdec_s, kdect_s, cdec_s = _decay_tables(r_heads, T, rows_pad, LANES)
    ret_os, ret_s = _retention_step(padc(sq), padc(sk), padc(sv), padc(sg), state_ret[0],
                                    (dmat2_s, qdec_s, kdect_s[:, :rows_pad].T, cdec_s), gnw)
    ret_os = ret_os[:, :T].reshape(1, rows_s, ret_w)

    assert kmt.shape[2] == n_full
    pad_rows = lambda t: jnp.pad(t.reshape(Bd, T, moba_w), ((0, 0), (0, pad8 - T), (0, 0)))
    q8 = pad_rows(smq)
    top = _topk(q8, kmt)[:, :T, :m_heads * MOBA_TOPK]
    moba_os = _attend(top.reshape(-1), page_table.reshape(-1), q8, pad_rows(smk), pad_rows(smv), mnw, ckt4, cvt4,
                      tokens=T, n_pages=n_pages)
    moba_os = moba_os[:, :T].reshape(1, rows_s, moba_w)
    y_sample = _outffn(xs, ret_os, moba_os, mod_s[2], mod_s[3], mod_s[4], mod_s[5], n2w, fw,
                       wo, wg, wu, wd, tm=rows_s, moba_transposed=False).reshape(Bd, T, D)
    k_sample = smk.reshape(1, Bd, T, m_heads, HEAD_DIM)
    v_sample = smv.reshape(1, Bd, T, m_heads, HEAD_DIM)

    return (y_prompt, y_sample, k_prompt, v_prompt, ret_p[None], k_sample, v_sample, ret_s[None])
```

```python
import functools
import math

import jax
import jax.numpy as jnp
from jax import lax
from jax.experimental import pallas as pl
from jax.experimental.pallas import tpu as pltpu

F32 = jnp.float32
BF16 = jnp.bfloat16
HIGHEST = lax.Precision.HIGHEST

HEAD_DIM = 64
HALF = HEAD_DIM // 2
RET_CHUNK = 128
MOBA_BLOCK = 256
MOBA_TOPK = 3
ROPE_BASE = 10000.0
NORM_EPS = 1e-6
LANES = 128
SUBLANES = 8
MXU_WIDTH = 256
VMEM_LIMIT = 56 * 1024 * 1024
NEG_INF = float("-inf")
MOBA_QSCALE = HEAD_DIM ** -0.5 * math.log2(math.e)
NT = (((1,), (1,)), ((), ()))
TN = (((0,), (0,)), ((), ()))


def _params(*sem):
    return pltpu.CompilerParams(dimension_semantics=sem, vmem_limit_bytes=VMEM_LIMIT)


def _rms(x, w):
    return x * lax.rsqrt(jnp.mean(x * x, axis=-1, keepdims=True) + NORM_EPS) * w


def _silu(x):
    return x * jax.nn.sigmoid(x)


def _const_spec(a):
    return pl.BlockSpec(a.shape, lambda *_: (0,) * a.ndim, pipeline_mode=pl.Buffered(1))


def _mod_kernel(c_ref, w_ref, b_ref, o_ref):
    s = _silu(c_ref[...])
    o_ref[...] = jnp.dot(s, w_ref[...], preferred_element_type=F32, precision=HIGHEST) + b_ref[...]


def _modulation(c, w_mod, b_mod):
    n, d = c.shape
    cols = w_mod.shape[1]
    tn = 1536
    return pl.pallas_call(
        _mod_kernel,
        out_shape=jax.ShapeDtypeStruct((n, cols), F32),
        grid=(cols // tn,),
        in_specs=[pl.BlockSpec((n, d), lambda j: (0, 0)),
                  pl.BlockSpec((d, tn), lambda j: (0, j)),
                  pl.BlockSpec((1, tn), lambda j: (0, j))],
        out_specs=pl.BlockSpec((n, tn), lambda j: (0, j)),
        compiler_params=_params("arbitrary"),
        name="modulation",
    )(c, w_mod, b_mod.reshape(1, cols))


def _mod_spec(arr, tm):
    if arr.shape[1] == 1:
        return pl.BlockSpec((None, 1, arr.shape[2]), lambda g, i, *_: (g, 0, 0))
    return pl.BlockSpec((None, tm, arr.shape[2]), lambda g, i, *_: (g, i, 0))


def _normed_input(x_ref, sh_ref, sc_ref, nw_ref):
    return (_rms(x_ref[...], nw_ref[...]) * (1.0 + sc_ref[...]) + sh_ref[...]).astype(BF16)


def _rope_store(z, cos, sin, o_ref, scale):
    lane = lax.broadcasted_iota(jnp.int32, cos.shape, 1)
    first_half = (lane % HEAD_DIM) < HALF
    for c in range(z.shape[1] // LANES):
        zc = z[:, c * LANES:(c + 1) * LANES]
        partner = jnp.where(first_half, pltpu.roll(zc, LANES - HALF, 1), pltpu.roll(zc, HALF, 1))
        o_ref[:, c * LANES:(c + 1) * LANES] = ((zc * cos + partner * sin) * scale).astype(o_ref.dtype)


def _inproj_sample_kernel(x_ref, sh_ref, sc_ref, nw_ref, cos_ref, sin_ref, w_ref,
                          rq_ref, rk_ref, rv_ref, rg_ref, mq_ref, mk_ref, mv_ref, *, width):
    h = _normed_input(x_ref, sh_ref, sc_ref, nw_ref)
    proj = lambda g: jnp.dot(h, w_ref[:, g * width:(g + 1) * width], preferred_element_type=F32)
    _rope_store(proj(0), cos_ref[...], sin_ref[...], rq_ref, 1.0)
    _rope_store(proj(1), cos_ref[...], sin_ref[...], rk_ref, HEAD_DIM ** -0.5)
    rv_ref[...] = proj(2)
    rg_ref[...] = _silu(proj(3))
    mq_ref[...] = proj(4) * MOBA_QSCALE
    mk_ref[...] = proj(5)
    mv_ref[...] = proj(6)


def _inproj_sample(x, sh, sc, nw, cos_t, sin_t, w_bf):
    G, R, D = x.shape
    width = w_bf.shape[1] // 7
    row = pl.BlockSpec((None, R, D), lambda g, i: (g, i, 0))
    tab = pl.BlockSpec((R, LANES), lambda g, i: (i, 0))
    act = pl.BlockSpec((None, R, width), lambda g, i: (g, i, 0))
    return pl.pallas_call(
        functools.partial(_inproj_sample_kernel, width=width),
        out_shape=[jax.ShapeDtypeStruct((G, R, width), F32)] * 7,
        grid=(G, 1),
        in_specs=[row, _mod_spec(sh, R), _mod_spec(sc, R), _const_spec(nw), tab, tab, _const_spec(w_bf)],
        out_specs=[act] * 7,
        compiler_params=_params("parallel", "parallel"),
        name="inproj_sample",
    )(x, sh, sc, nw, cos_t, sin_t, w_bf)


def _inproj_prompt_kernel(x_ref, sh_ref, sc_ref, nw_ref, cos_ref, sin_ref, cost_ref, sint_ref, w_ref, wt_ref,
                          rq_ref, rkt_ref, rv_ref, rg_ref, mqt_ref, mk_ref, mkt_ref, mvt_ref, *, width):
    h = _normed_input(x_ref, sh_ref, sc_ref, nw_ref)
    proj = lambda g: jnp.dot(h, w_ref[:, g * width:(g + 1) * width], preferred_element_type=F32)
    proj_t = lambda g: lax.dot_general(wt_ref[g * width:(g + 1) * width, :], h, NT, preferred_element_type=F32)
    _rope_store(proj(0), cos_ref[...], sin_ref[...], rq_ref, 1.0)
    rv_ref[...] = proj(1).astype(rv_ref.dtype)
    rg_ref[...] = _silu(proj(2)).astype(rg_ref.dtype)
    zt = proj_t(0)
    cost = cost_ref[...]
    sint = sint_ref[...]
    scale = HEAD_DIM ** -0.5
    for hd in range(width // HEAD_DIM):
        lo = slice(hd * HEAD_DIM, hd * HEAD_DIM + HALF)
        hi = slice(hd * HEAD_DIM + HALF, (hd + 1) * HEAD_DIM)
        a, b = zt[lo], zt[hi]
        rkt_ref[lo, :] = ((a * cost - b * sint) * scale).astype(rkt_ref.dtype)
        rkt_ref[hi, :] = ((a * sint + b * cost) * scale).astype(rkt_ref.dtype)
    mqt_ref[...] = (proj_t(1) * MOBA_QSCALE).astype(mqt_ref.dtype)
    mkt = proj_t(2)
    mkt_ref[...] = mkt
    mk_ref[...] = mkt.T.astype(mk_ref.dtype)
    mvt_ref[...] = proj_t(3)


def _inproj_prompt(x, sh, sc, nw, cos_t, sin_t, cos_tt, sin_tt, w_tok, w_trn, *, tm):
    G, R, D = x.shape
    width = w_tok.shape[1] // 3
    row = pl.BlockSpec((None, tm, D), lambda g, i: (g, i, 0))
    tab = pl.BlockSpec((tm, LANES), lambda g, i: (i, 0))
    tab_t = pl.BlockSpec((HALF, tm), lambda g, i: (0, i))
    act = pl.BlockSpec((None, tm, width), lambda g, i: (g, i, 0))
    act_t = pl.BlockSpec((None, width, tm), lambda g, i: (g, 0, i))
    tok = jax.ShapeDtypeStruct((G, R, width), BF16)
    trn = lambda dt: jax.ShapeDtypeStruct((G, width, R), dt)
    return pl.pallas_call(
        functools.partial(_inproj_prompt_kernel, width=width),
        out_shape=[tok, trn(BF16), tok, tok, trn(BF16), tok, trn(F32), trn(F32)],
        grid=(G, R // tm),
        in_specs=[row, _mod_spec(sh, tm), _mod_spec(sc, tm), _const_spec(nw), tab, tab, tab_t, tab_t,
                  _const_spec(w_tok), _const_spec(w_trn)],
        out_specs=[act, act_t, act, act, act_t, act, act_t, act_t],
        compiler_params=_params("parallel", "parallel"),
        name="inproj_prompt",
    )(x, sh, sc, nw, cos_t, sin_t, cos_tt, sin_tt, w_tok, w_trn)


def _retention_kernel(q_ref, kt_ref, v_ref, g_ref, s0_ref, dmat2_ref, qdec_ref, kdect_ref, cdec_ref, gnw_ref,
                      avg_ref, o_ref, sout_ref, s_scr, o_scr, *, heads, chunks):
    j = pl.program_id(1)
    pairs = heads // 2
    lane = lax.broadcasted_iota(jnp.int32, (LANES, LANES), 1)
    sub = lax.broadcasted_iota(jnp.int32, (LANES, LANES), 0)
    first = lane < HEAD_DIM
    diag = first == (sub < HEAD_DIM)
    keep_a = jnp.where(first, 1.0, 0.0).astype(BF16)
    keep_b = jnp.where(first, 0.0, 1.0).astype(BF16)

    @pl.when(j == 0)
    def _():
        z = jnp.zeros((HEAD_DIM, HEAD_DIM), F32)
        for p in range(pairs):
            s_scr[p] = jnp.concatenate([jnp.concatenate([s0_ref[2 * p], z], axis=1),
                                        jnp.concatenate([z, s0_ref[2 * p + 1]], axis=1)], axis=0)

    gnw = gnw_ref[...]
    avg = avg_ref[...]

    group_mean = functools.partial(_group_mean, avg=avg)

    for c in range(chunks):
        rows = slice(c * RET_CHUNK, (c + 1) * RET_CHUNK)
        for p in range(pairs):
            cs = slice(p * LANES, (p + 1) * LANES)
            qp = q_ref[rows, cs].astype(BF16)
            vp = v_ref[rows, cs].astype(BF16)
            ktp = kt_ref[cs, rows]
            q2 = jnp.concatenate([qp * keep_a, qp * keep_b], axis=0)
            att2 = jnp.dot(q2, ktp.astype(BF16), preferred_element_type=F32) * dmat2_ref[p]
            o2 = jnp.dot(att2.astype(BF16), vp, preferred_element_type=F32)
            s = s_scr[p]
            o = (jnp.where(first, o2[:RET_CHUNK], o2[RET_CHUNK:])
                 + jnp.dot(qp, s.astype(BF16), preferred_element_type=F32) * qdec_ref[:, cs])
            kdt = (ktp.astype(F32) * kdect_ref[cs, :]).astype(BF16)
            s_scr[p] = s * cdec_ref[p] + jnp.where(diag, jnp.dot(kdt, vp, preferred_element_type=F32), 0.0)
            o_scr[rows, cs] = o

    o_all = o_scr[...]
    d = o_all - group_mean(o_all)
    var = group_mean(d * d)
    o_ref[...] = (g_ref[...].astype(F32) * (d * lax.rsqrt(var + NORM_EPS) * gnw)).astype(o_ref.dtype)

    @pl.when(j == pl.num_programs(1) - 1)
    def _():
        for p in range(pairs):
            s = s_scr[p]
            sout_ref[2 * p] = s[:HEAD_DIM, :HEAD_DIM]
            sout_ref[2 * p + 1] = s[HEAD_DIM:, HEAD_DIM:]


def _retention(q, kt, v, gs, s0, tables, gnw, *, tl, out_dtype):
    B, L, W = q.shape
    heads = W // HEAD_DIM
    seq = pl.BlockSpec((None, tl, W), lambda b, j: (b, j, 0))
    seq_t = pl.BlockSpec((None, W, tl), lambda b, j: (b, 0, j))
    st = pl.BlockSpec((None, heads, HEAD_DIM, HEAD_DIM), lambda b, j: (b, 0, 0, 0))
    return pl.pallas_call(
        functools.partial(_retention_kernel, heads=heads, chunks=tl // RET_CHUNK),
        out_shape=[jax.ShapeDtypeStruct((B, L, W), out_dtype),
                   jax.ShapeDtypeStruct((B, heads, HEAD_DIM, HEAD_DIM), F32)],
        grid=(B, L // tl),
        in_specs=[seq, seq_t, seq, seq, st] + [_const_spec(t) for t in tables] + [_const_spec(gnw),
                                                                                   _const_spec(_head_average(W))],
        out_specs=[seq, st],
        scratch_shapes=[pltpu.VMEM((heads // 2, LANES, LANES), F32), pltpu.VMEM((tl, W), F32)],
        compiler_params=_params("parallel", "arbitrary"),
        name="retention",
    )(q, kt, v, gs, s0, *tables, gnw, _head_average(W))


def _retention_step_kernel(q_ref, k_ref, v_ref, g_ref, s0_ref, dmat2_ref, qdec_ref, kdec_ref, cdec_ref, gnw_ref,
                           avg_ref, o_ref, sout_ref, *, heads):
    seqs, rows, _ = q_ref.shape
    pairs = heads // 2
    lane = lax.broadcasted_iota(jnp.int32, (LANES, LANES), 1)
    sub = lax.broadcasted_iota(jnp.int32, (LANES, LANES), 0)
    diag = (lane < HEAD_DIM) == (sub < HEAD_DIM)
    first = lax.broadcasted_iota(jnp.int32, (rows, LANES), 1) < HEAD_DIM
    keep_a = jnp.where(first, 1.0, 0.0).astype(BF16)
    keep_b = jnp.where(first, 0.0, 1.0).astype(BF16)
    zpad = jnp.zeros((LANES - rows, LANES), BF16)
    zs = jnp.zeros((HEAD_DIM, HEAD_DIM), F32)
    avg = avg_ref[...]
    gnw = gnw_ref[...]

    group_mean = functools.partial(_group_mean, avg=avg)

    for s in range(seqs):
        outs = []
        for p in range(pairs):
            cs = slice(p * LANES, (p + 1) * LANES)
            qp = q_ref[s, :, cs].astype(BF16)
            kp = k_ref[s, :, cs]
            pad = lambda t: jnp.concatenate([t.astype(BF16), zpad], axis=0)
            kpad, vpad, kdpad = pad(kp), pad(v_ref[s, :, cs]), pad(kp * kdec_ref[:, cs])
            q2 = jnp.concatenate([qp * keep_a, qp * keep_b], axis=0)
            att2 = lax.dot_general(q2, kpad, NT, preferred_element_type=F32) * dmat2_ref[p]
            o2 = jnp.dot(att2.astype(BF16), vpad, preferred_element_type=F32)
            state = jnp.concatenate([jnp.concatenate([s0_ref[s, 2 * p], zs], axis=1),
                                     jnp.concatenate([zs, s0_ref[s, 2 * p + 1]], axis=1)], axis=0)
            outs.append(jnp.where(first, o2[:rows], o2[rows:])
                        + jnp.dot(qp, state.astype(BF16), preferred_element_type=F32) * qdec_ref[:, cs])
            state = state * cdec_ref[p] + jnp.where(
                diag, lax.dot_general(kdpad, vpad, TN, preferred_element_type=F32), 0.0)
            sout_ref[s, 2 * p] = state[:HEAD_DIM, :HEAD_DIM]
            sout_ref[s, 2 * p + 1] = state[HEAD_DIM:, HEAD_DIM:]
        o_all = jnp.concatenate(outs, axis=1)
        d = o_all - group_mean(o_all)
        var = group_mean(d * d)
        o_ref[s] = g_ref[s] * (d * lax.rsqrt(var + NORM_EPS) * gnw)


def _retention_step(q, k, v, gs, s0, tables, gnw):
    bd, rows, W = q.shape
    heads = W // HEAD_DIM
    seqs = max(s for s in (8, 4, 2, 1) if bd % s == 0)
    seq = pl.BlockSpec((seqs, rows, W), lambda b: (b, 0, 0))
    st = pl.BlockSpec((seqs, heads, HEAD_DIM, HEAD_DIM), lambda b: (b, 0, 0, 0))
    consts = list(tables) + [gnw, _head_average(W)]
    return pl.pallas_call(
        functools.partial(_retention_step_kernel, heads=heads),
        out_shape=[jax.ShapeDtypeStruct((bd, rows, W), F32),
                   jax.ShapeDtypeStruct((bd, heads, HEAD_DIM, HEAD_DIM), F32)],
        grid=(bd // seqs,),
        in_specs=[seq, seq, seq, seq, st] + [_const_spec(t) for t in consts],
        out_specs=[seq, st],
        compiler_params=_params("parallel"),
        name="retention_step",
    )(q, k, v, gs, s0, *consts)


def _group_mean(t, avg):
    hi = t.astype(BF16)
    lo = (t - hi.astype(F32)).astype(BF16)
    w = avg.shape[0]
    cols = [jnp.dot(hi[:, c:c + w], avg, preferred_element_type=F32)
            + jnp.dot(lo[:, c:c + w], avg, preferred_element_type=F32) for c in range(0, t.shape[1], w)]
    return jnp.concatenate(cols, axis=1) if len(cols) > 1 else cols[0]


def _head_average(width):
    width = min(width, MXU_WIDTH)
    r = jnp.arange(width) // HEAD_DIM
    return jnp.where(r[:, None] == r[None, :], 1.0 / HEAD_DIM, 0.0).astype(BF16)


def _decay_tables(heads, c_len, q_pad, k_pad):
    log_g = jnp.log1p(-jnp.exp2(-5.0 - jnp.arange(heads, dtype=F32)))
    idx = jnp.arange(c_len, dtype=F32)
    diff = idx[:, None] - idx[None, :]
    dmat = jnp.where(diff >= 0, jnp.exp(jnp.maximum(diff, 0.0) * log_g[:, None, None]), 0.0)
    q_dec = jnp.exp((idx + 1.0) * log_g[:, None])
    k_dec = jnp.exp((c_len - 1.0 - idx) * log_g[:, None])
    c_dec = jnp.exp(c_len * log_g)
    qp, kp = q_pad - c_len, k_pad - c_len
    dmat2 = jnp.pad(dmat, ((0, 0), (0, qp), (0, kp))).reshape(heads // 2, 2 * q_pad, k_pad)
    qdec = jnp.pad(jnp.repeat(q_dec.T, HEAD_DIM, axis=1), ((0, qp), (0, 0)))
    kdect = jnp.pad(jnp.repeat(k_dec, HEAD_DIM, axis=0), ((0, 0), (0, kp)))
    cdec = jnp.broadcast_to(jnp.repeat(c_dec, HEAD_DIM).reshape(heads // 2, 2 * HEAD_DIM, 1),
                            (heads // 2, 2 * HEAD_DIM, 2 * HEAD_DIM))
    return dmat2, qdec, kdect, cdec


def _col_reduce(x, op, final):
    while x.shape[0] % (2 * SUBLANES) == 0:
        half = x.shape[0] // 2
        x = op(x[:half], x[half:])
    return final(x, axis=0, keepdims=True)


def _moba_prompt_kernel(qt_ref, k_ref, kt_ref, vt_ref, nwt_ref, o_ref, *, seq):
    nb = seq // MOBA_BLOCK
    nbp = -(-nb // SUBLANES) * SUBLANES
    qb = MOBA_BLOCK
    pair = 2 * HEAD_DIM
    n_heads = qt_ref.shape[0] // HEAD_DIM
    rows_of = lambda h: slice(h * HEAD_DIM, (h + 1) * HEAD_DIM)
    pair_of = lambda h: slice((h // 2) * pair, (h // 2 + 1) * pair)
    km_heads = []
    for pr in ([slice(p * pair, (p + 1) * pair) for p in range(n_heads // 2)] if nb > MOBA_TOPK + 1 else []):
        lane_n = lax.broadcasted_iota(jnp.int32, (pair, LANES), 1)
        kmt = jnp.zeros((pair, LANES), F32)
        for n in range(nb):
            tot = kt_ref[pr, n * MOBA_BLOCK:n * MOBA_BLOCK + LANES]
            for c in range(1, MOBA_BLOCK // LANES):
                tot = tot + kt_ref[pr, n * MOBA_BLOCK + c * LANES:n * MOBA_BLOCK + (c + 1) * LANES]
            kmt = jnp.where(lane_n == n, jnp.sum(tot, axis=1, keepdims=True) * (1.0 / MOBA_BLOCK), kmt)
        km = kmt.T[0:nbp]
        lane_k = lax.broadcasted_iota(jnp.int32, km.shape, 1)
        km_heads += [jnp.where(lane_k < HEAD_DIM, km, 0.0), jnp.where(lane_k < HEAD_DIM, 0.0, km)]
    blk = lax.broadcasted_iota(jnp.int32, (nbp, qb), 0)
    key_i = lax.broadcasted_iota(jnp.int32, (qb, qb), 0)
    qry_i = lax.broadcasted_iota(jnp.int32, (qb, qb), 1)
    causal = key_i <= qry_i
    zeros = jnp.zeros((HEAD_DIM, qb), BF16)
    ones = jnp.ones((2 * SUBLANES, seq), BF16)
    vt_ones = [jnp.concatenate([vt_ref[rows_of(h), :].astype(BF16), ones], axis=0) for h in range(n_heads)]

    def scores(i, h):
        qh = qt_ref[rows_of(h), i * qb:(i + 1) * qb]
        qm = jnp.concatenate([qh, zeros] if h % 2 == 0 else [zeros, qh], axis=0)
        return jnp.dot(k_ref[0:(i + 1) * MOBA_BLOCK, pair_of(h)], qm, preferred_element_type=F32)

    def softmax(i, h, st):
        biases = [None] * i
        if i > MOBA_TOPK:
            qt = qt_ref[pair_of(h), i * qb:(i + 1) * qb].astype(F32)
            gate = jnp.dot(km_heads[h], qt, preferred_element_type=F32, precision=HIGHEST)
            valid = blk < i
            for n in range(i):
                gn = gate[n:n + 1, :]
                ahead = valid & ((gate > gn) | ((gate == gn) & (blk < n)))
                rank = jnp.sum(ahead.astype(F32), axis=0, keepdims=True)
                biases[n] = jnp.where(rank < MOBA_TOPK, 0.0, NEG_INF)
        past = [st[n * MOBA_BLOCK:(n + 1) * MOBA_BLOCK] for n in range(i)]
        own = jnp.where(causal, st[i * MOBA_BLOCK:(i + 1) * MOBA_BLOCK], NEG_INF)
        m = _col_reduce(own, jnp.maximum, jnp.max)
        for sb, bias in zip(past, biases):
            mb = _col_reduce(sb, jnp.maximum, jnp.max)
            m = jnp.maximum(m, mb if bias is None else mb + bias)
        pieces = [jnp.exp2(sb + (-m if bias is None else bias - m)).astype(BF16) for sb, bias in zip(past, biases)]
        pieces.append(jnp.exp2(own - m).astype(BF16))
        return jnp.concatenate(pieces, axis=0) if i > 0 else pieces[0]

    def output(i, h, p):
        hr = rows_of(h)
        ot = jnp.dot(vt_ones[h][:, 0:(i + 1) * MOBA_BLOCK], p, preferred_element_type=F32)
        ot = ot[0:HEAD_DIM] / ot[HEAD_DIM:HEAD_DIM + 1]
        ms = jnp.mean(ot * ot, axis=0, keepdims=True)
        o_ref[hr, i * qb:(i + 1) * qb] = (ot * lax.rsqrt(ms + NORM_EPS) * nwt_ref[hr, :]).astype(o_ref.dtype)

    bodies = [(i, h) for i in range(nb) for h in range(n_heads)]
    ahead, behind = 2, 2
    queue = [scores(*b) for b in bodies[:ahead]]
    pending = []
    for idx, body in enumerate(bodies):
        st = queue.pop(0)
        if idx + ahead < len(bodies):
            queue.append(scores(*bodies[idx + ahead]))
        pending.append((*body, softmax(*body, st)))
        if len(pending) > behind:
            output(*pending.pop(0))
    for item in pending:
        output(*item)


def _moba_prompt(qt, k, kt, vt, nwt, *, out_dtype):
    B, W, S = qt.shape
    pair = 2 * HEAD_DIM * (2 if W % (4 * HEAD_DIM) == 0 else 1)
    trn = pl.BlockSpec((None, pair, S), lambda b, h: (b, h, 0))
    return pl.pallas_call(
        functools.partial(_moba_prompt_kernel, seq=S),
        out_shape=jax.ShapeDtypeStruct((B, W, S), out_dtype),
        grid=(B, W // pair),
        in_specs=[trn, pl.BlockSpec((None, S, pair), lambda b, h: (b, 0, h)), trn, trn,
                  pl.BlockSpec((pair, MOBA_BLOCK), lambda b, h: (h, 0))],
        out_specs=trn,
        compiler_params=_params("parallel", "parallel"),
        name="moba_prompt",
    )(qt, k, kt, vt, nwt)


def _topk_kernel(q_ref, km_ref, o_ref, *, heads):
    seqs, rows, _ = q_ref.shape
    n_blk = km_ref.shape[2]
    lane_b = lax.broadcasted_iota(jnp.int32, (seqs * heads * rows, n_blk), 1).astype(F32)
    lane_o = lax.broadcasted_iota(jnp.int32, (rows, LANES), 1)
    gate = jnp.concatenate(
        [jnp.dot(q_ref[s, :, h * HEAD_DIM:(h + 1) * HEAD_DIM], km_ref[s, h * HEAD_DIM:(h + 1) * HEAD_DIM, :],
                 preferred_element_type=F32, precision=HIGHEST) for s in range(seqs) for h in range(heads)], axis=0)
    picks = []
    for r in range(MOBA_TOPK):
        m = jnp.max(gate, axis=1, keepdims=True)
        idx = jnp.min(jnp.where(gate == m, lane_b, float(n_blk)), axis=1, keepdims=True)
        gate = jnp.where(lane_b == idx, NEG_INF, gate)
        picks.append(idx)
    for s in range(seqs):
        out = jnp.zeros((rows, LANES), F32)
        for h in range(heads):
            base = (s * heads + h) * rows
            for r in range(MOBA_TOPK):
                out = jnp.where(lane_o == h * MOBA_TOPK + r, picks[r][base:base + rows], out)
        o_ref[s] = out.astype(jnp.int32)


def _topk(q8, kmt):
    bd, rows, W = q8.shape
    n_full = kmt.shape[2]
    seqs = max(s for s in (8, 4, 2, 1) if bd % s == 0)
    return pl.pallas_call(
        functools.partial(_topk_kernel, heads=W // HEAD_DIM),
        out_shape=jax.ShapeDtypeStruct((bd, rows, LANES), jnp.int32),
        grid=(bd // seqs,),
        in_specs=[pl.BlockSpec((seqs, rows, W), lambda b: (b, 0, 0)),
                  pl.BlockSpec((seqs, W, n_full), lambda b: (b, 0, 0))],
        out_specs=pl.BlockSpec((seqs, rows, LANES), lambda b: (b, 0, 0)),
        compiler_params=_params("parallel"),
        name="moba_topk",
    )(q8, kmt)


def _attend_kernel(top_ref, pt_ref, q_ref, kn_ref, vn_ref, nw_ref, ck_ref, cv_ref, o_ref,
                   kbuf, vbuf, sem, *, heads, tokens, n_pages, page):
    b = pl.program_id(0)
    nb = pl.num_programs(0)
    ppb = MOBA_BLOCK // page
    n_sel = MOBA_TOPK * MOBA_BLOCK
    slot = b % 2

    def page_copies(seq, buf_slot):
        out = []
        for h in range(heads):
            for t in range(tokens):
                for r in range(MOBA_TOPK):
                    blk = top_ref[((seq * tokens + t) * heads + h) * MOBA_TOPK + r]
                    for p in range(ppb):
                        phys = pt_ref[seq * n_pages + blk * ppb + p]
                        dst = pl.ds((r * ppb + p) * page, page)
                        out.append(pltpu.make_async_copy(ck_ref.at[phys, h], kbuf.at[buf_slot, h, t, :, dst],
                                                         sem.at[buf_slot, 0]))
                        out.append(pltpu.make_async_copy(cv_ref.at[phys, h], vbuf.at[buf_slot, h, t, :, dst],
                                                         sem.at[buf_slot, 1]))
        return out

    @pl.when(b == 0)
    def _():
        for c in page_copies(b, slot):
            c.start()

    @pl.when(b + 1 < nb)
    def _():
        for c in page_copies(b + 1, 1 - slot):
            c.start()

    for c in page_copies(b, slot):
        c.wait()
    kb = kbuf.at[slot]
    vb = vbuf.at[slot]
    zrows = jnp.zeros((LANES - kn_ref.shape[0], LANES), F32)
    for p in range(heads // 2):
        cs = slice(p * LANES, (p + 1) * LANES)
        knt = jnp.concatenate([kn_ref[:, cs], zrows], axis=0).T
        vnt = jnp.concatenate([vn_ref[:, cs], zrows], axis=0).T
        for hh in range(2):
            hr = slice(hh * HEAD_DIM, (hh + 1) * HEAD_DIM)
            for t in range(tokens):
                kb[2 * p + hh, t, :, n_sel:n_sel + LANES] = knt[hr]
                vb[2 * p + hh, t, :, n_sel:n_sel + LANES] = vnt[hr]

    rows = q_ref.shape[0]
    col = lax.broadcasted_iota(jnp.int32, (rows, n_sel + LANES), 1)
    row = lax.broadcasted_iota(jnp.int32, (rows, HEAD_DIM), 0)
    pairs = [(h, t) for h in range(heads) for t in range(tokens)]
    qs = [q_ref[:, h * HEAD_DIM:(h + 1) * HEAD_DIM].astype(BF16) for h in range(heads)]
    scores = [jnp.where(col <= n_sel + t,
                        jnp.dot(qs[h], kb[h, t].astype(BF16), preferred_element_type=F32), NEG_INF)
              for h, t in pairs]
    probs = []
    for s in scores:
        p = jnp.exp2(s - jnp.max(s, axis=1, keepdims=True))
        probs.append((p.astype(BF16), jnp.sum(p, axis=1, keepdims=True)))
    outs = [jnp.zeros((rows, HEAD_DIM), F32)] * heads
    for (h, t), (p, l) in zip(pairs, probs):
        o = lax.dot_general(p, vb[h, t].astype(BF16), NT, preferred_element_type=F32) / l
        outs[h] = jnp.where(row == t, o, outs[h])
    o_ref[...] = jnp.concatenate([_rms(outs[h], nw_ref[:, h * HEAD_DIM:(h + 1) * HEAD_DIM])
                                  for h in range(heads)], axis=1)


def _attend(top_flat, pt_flat, q8, kn8, vn8, nw, ckt4, cvt4, *, tokens, n_pages):
    bd, rows, W = q8.shape
    heads = W // HEAD_DIM
    page = ckt4.shape[3]
    n_keys = MOBA_TOPK * MOBA_BLOCK + LANES
    any_spec = pl.BlockSpec(memory_space=pl.ANY)
    return pl.pallas_call(
        functools.partial(_attend_kernel, heads=heads, tokens=tokens, n_pages=n_pages, page=page),
        out_shape=jax.ShapeDtypeStruct((bd, rows, W), F32),
        grid_spec=pltpu.PrefetchScalarGridSpec(
            num_scalar_prefetch=2,
            grid=(bd,),
            in_specs=[pl.BlockSpec((None, rows, W), lambda b, *_: (b, 0, 0)),
                      pl.BlockSpec((None, rows, W), lambda b, *_: (b, 0, 0)),
                      pl.BlockSpec((None, rows, W), lambda b, *_: (b, 0, 0)),
                      pl.BlockSpec((1, W), lambda b, *_: (0, 0)),
                      any_spec, any_spec],
            out_specs=pl.BlockSpec((None, rows, W), lambda b, *_: (b, 0, 0)),
            scratch_shapes=[pltpu.VMEM((2, heads, tokens, HEAD_DIM, n_keys), F32),
                            pltpu.VMEM((2, heads, tokens, HEAD_DIM, n_keys), F32),
                            pltpu.SemaphoreType.DMA((2, 2))],
        ),
        compiler_params=_params("arbitrary"),
        name="moba_attend",
    )(top_flat, pt_flat, q8, kn8, vn8, nw, ckt4, cvt4)


FF_CHUNKS = 3
SIDE_SLOTS = FF_CHUNKS + 1
POOL_SLOTS = 3


def _outffn_compute(x_ref, ro_ref, mo_ref, g1_ref, sh2_ref, sc2_ref, g2_ref, n2w_ref, fw_ref,
                    wo_ref, wg_ref, wu_ref, wd_ref, y_ref, *, ff_chunks, moba_transposed, side_work):
    side_work(0)
    rw = ro_ref.shape[1]
    mo = mo_ref[...].astype(BF16)
    attn = (jnp.dot(ro_ref[...].astype(BF16), wo_ref[0:rw, :], preferred_element_type=F32)
            + lax.dot_general(mo, wo_ref[rw:, :], TN if moba_transposed else (((1,), (0,)), ((), ())),
                              preferred_element_type=F32))
    x1 = x_ref[...] + g1_ref[...] * attn
    h2 = (_rms(x1, n2w_ref[...]) * (1.0 + sc2_ref[...]) + sh2_ref[...]).astype(BF16)
    acc = None
    for c, (lo, hi) in enumerate(ff_chunks):
        side_work(1 + c)
        gate = jnp.dot(h2, wg_ref[:, lo:hi], preferred_element_type=F32)
        up = jnp.dot(h2, wu_ref[:, lo:hi], preferred_element_type=F32)
        part = jnp.dot((_silu(gate) * up).astype(BF16), wd_ref[lo:hi, :], preferred_element_type=F32)
        acc = part if acc is None else acc + part
    x2 = x1 + g2_ref[...] * acc
    y_ref[...] = _rms(x2, fw_ref[...])


def _outffn_kernel(*refs, ff_chunks, moba_transposed):
    _outffn_compute(*refs, ff_chunks=ff_chunks, moba_transposed=moba_transposed, side_work=lambda c: None)


def _outffn_pool_kernel(pt_ref, *refs, ff_chunks, moba_transposed, inner_steps, pages_per_step):
    ck_ref, y_ref, km_ref, pbuf, sem = refs[-5:]
    step = pl.program_id(0) * inner_steps + pl.program_id(1)
    n_steps = pl.num_programs(0) * inner_steps
    n_pages = pt_ref.shape[1]
    steps_per_seq = n_pages // pages_per_step
    chunk_pages = pages_per_step // SIDE_SLOTS
    ppb = MOBA_BLOCK // pbuf.shape[3]
    chunk_blocks = chunk_pages // ppb
    lane = lax.broadcasted_iota(jnp.int32, km_ref.shape, 1)

    def chunk_slot(st, c):
        return lax.rem(st * SIDE_SLOTS + c, POOL_SLOTS)

    def chunk_copies(st, c):
        seq = st // steps_per_seq
        base = (st % steps_per_seq) * pages_per_step + c * chunk_pages
        slot = chunk_slot(st, c)
        return [pltpu.make_async_copy(ck_ref.at[pt_ref[seq, base + p]], pbuf.at[slot, p], sem.at[slot])
                for p in range(chunk_pages)]

    def side_work(c):
        if c == 0:
            @pl.when(step == 0)
            def _():
                for a in range(POOL_SLOTS - 1):
                    for cp in chunk_copies(step, a):
                        cp.start()

            @pl.when(step % steps_per_seq == 0)
            def _():
                km_ref[...] = jnp.zeros_like(km_ref)
        ahead = c + POOL_SLOTS - 1
        if ahead < SIDE_SLOTS:
            for cp in chunk_copies(step, ahead):
                cp.start()
        else:
            @pl.when(step + 1 < n_steps)
            def _():
                for cp in chunk_copies(step + 1, ahead - SIDE_SLOTS):
                    cp.start()
        for cp in chunk_copies(step, c):
            cp.wait()
        slot = chunk_slot(step, c)
        first_block = (step % steps_per_seq) * (pages_per_step // ppb) + c * chunk_blocks
        acc = km_ref[...]
        for n in range(chunk_blocks):
            tot = pbuf[slot, n * ppb]
            for p in range(1, ppb):
                tot = tot + pbuf[slot, n * ppb + p]
            col = jnp.sum(tot, axis=1, keepdims=True) * (1.0 / MOBA_BLOCK)
            acc = jnp.where(lane == first_block + n, col, acc)
        km_ref[...] = acc

    _outffn_compute(*refs[:-5], y_ref, ff_chunks=ff_chunks, moba_transposed=moba_transposed, side_work=side_work)


def _outffn(x, ro, mo, g1, sh2, sc2, g2, n2w, fw, wo, wg, wu, wd, *, tm, moba_transposed, pool=None):
    G, R, D = x.shape
    W = ro.shape[2]
    dff = wg.shape[1]
    cuts = [round(dff * c / FF_CHUNKS / 256) * 256 for c in range(FF_CHUNKS)] + [dff]
    ff_chunks = tuple(zip(cuts[:-1], cuts[1:]))
    inner = R // tm
    row = pl.BlockSpec((None, tm, D), lambda g, i, *_: (g, i, 0))
    act = pl.BlockSpec((None, tm, W), lambda g, i, *_: (g, i, 0))
    act_t = pl.BlockSpec((None, W, tm), lambda g, i, *_: (g, 0, i))
    in_specs = [row, act, act_t if moba_transposed else act,
                _mod_spec(g1, tm), _mod_spec(sh2, tm), _mod_spec(sc2, tm), _mod_spec(g2, tm),
                _const_spec(n2w), _const_spec(fw), _const_spec(wo), _const_spec(wg), _const_spec(wu),
                _const_spec(wd)]
    args = (x, ro, mo, g1, sh2, sc2, g2, n2w, fw, wo, wg, wu, wd)
    y_shape = jax.ShapeDtypeStruct((G, R, D), F32)
    if pool is None:
        return pl.pallas_call(
            functools.partial(_outffn_kernel, ff_chunks=ff_chunks, moba_transposed=moba_transposed),
            out_shape=y_shape, grid=(G, inner), in_specs=in_specs, out_specs=row,
            compiler_params=_params("parallel", "parallel"),
            name="outproj_ffn",
        )(*args)
    page_table, ckt = pool
    bd, n_pages = page_table.shape
    _, hd, page = ckt.shape
    ppb = MOBA_BLOCK // page
    n_steps = G * inner
    pages_per_step = bd * n_pages // n_steps
    assert pages_per_step * n_steps == bd * n_pages and n_pages % pages_per_step == 0
    assert pages_per_step % (SIDE_SLOTS * ppb) == 0 and POOL_SLOTS - 1 <= SIDE_SLOTS
    steps_per_seq = n_pages // pages_per_step
    n_blocks = n_pages // ppb
    km_spec = pl.BlockSpec((None, hd, n_blocks), lambda g, i, *_: ((g * inner + i) // steps_per_seq, 0, 0))
    return pl.pallas_call(
        functools.partial(_outffn_pool_kernel, ff_chunks=ff_chunks, moba_transposed=moba_transposed,
                          inner_steps=inner, pages_per_step=pages_per_step),
        out_shape=[y_shape, jax.ShapeDtypeStruct((bd, hd, n_blocks), F32)],
        grid_spec=pltpu.PrefetchScalarGridSpec(
            num_scalar_prefetch=1,
            grid=(G, inner),
            in_specs=in_specs + [pl.BlockSpec(memory_space=pl.ANY)],
            out_specs=[row, km_spec],
            scratch_shapes=[pltpu.VMEM((POOL_SLOTS, pages_per_step // SIDE_SLOTS, hd, page), F32),
                            pltpu.SemaphoreType.DMA((POOL_SLOTS,))],
        ),
        compiler_params=_params("arbitrary", "arbitrary"),
        name="outproj_ffn_pool",
    )(page_table, *args, ckt)


def _rope_angles(pos):
    inv = ROPE_BASE ** (-jnp.arange(HALF, dtype=F32) / HALF)
    ang = pos.astype(F32)[:, None] * inv[None, :]
    return jnp.cos(ang), jnp.sin(ang)


def _rope_tables(pos):
    cos, sin = _rope_angles(pos)
    reps = LANES // HEAD_DIM
    return jnp.tile(cos, (1, 2 * reps)), jnp.tile(jnp.concatenate([-sin, sin], axis=1), (1, reps))


def kernel(x_prompt, x_sample, cache_k, cache_v, state_ret, page_table, c_prompt, c_sample,
           norm1_w, norm2_w, final_w, w_mod, b_mod, w_in, ret_gn_w, moba_norm_w, w_out,
           w_gate, w_up, w_down):
    Bp, S, D = x_prompt.shape
    Bd, T, _ = x_sample.shape
    depth = w_in.shape[0]
    assert depth == 1, "single decoder layer"
    n_pool, page, m_heads = cache_k.shape[1], cache_k.shape[2], cache_k.shape[3]
    n_pages = page_table.shape[1]
    past_len = n_pages * page
    assert past_len % MOBA_BLOCK == 0 and MOBA_BLOCK % page == 0
    n_full = past_len // MOBA_BLOCK
    moba_w = m_heads * HEAD_DIM
    ret_w = (w_in.shape[2] - 3 * moba_w) // 4
    assert ret_w == moba_w, "the two head groups share one projection width"
    W = ret_w
    r_heads = ret_w // HEAD_DIM
    fw = final_w.reshape(1, D)
    rows_s = Bd * T
    pad8 = SUBLANES

    w_in0 = w_in[0].astype(BF16)
    col = lambda g: w_in0[:, g * W:(g + 1) * W]
    w_tok = jnp.concatenate([col(0), col(2), col(3)], axis=1)
    w_trn = jnp.concatenate([col(1), col(4), col(5), col(6)], axis=1).T
    wo, wg, wu, wd = (w[0].astype(BF16) for w in (w_out, w_gate, w_up, w_down))
    n1w, n2w = norm1_w[0].reshape(1, D), norm2_w[0].reshape(1, D)
    gnw, mnw = ret_gn_w[0].reshape(1, ret_w), moba_norm_w[0].reshape(1, moba_w)

    mod = _modulation(jnp.concatenate([c_prompt, c_sample], axis=0), w_mod[0], b_mod[0])
    mod_p = [m[:, None, :] for m in jnp.split(mod[:Bp], 6, axis=-1)]
    mod_s = [jnp.repeat(m, T, axis=0)[None] for m in jnp.split(mod[Bp:], 6, axis=-1)]

    cos_a, sin_a = _rope_angles(jnp.arange(S))
    cos_p, sin_p = _rope_tables(jnp.arange(S))
    rq, rkt, rv, rg, mqt, mk, mkt, mvt = _inproj_prompt(x_prompt, mod_p[0], mod_p[1], n1w, cos_p, sin_p,
                                                        cos_a.T, sin_a.T, w_tok, w_trn, tm=1024)
    s0_p = jnp.zeros((Bp, r_heads, HEAD_DIM, HEAD_DIM), F32)
    ret_o, ret_p = _retention(rq, rkt, rv, rg, s0_p, _decay_tables(r_heads, RET_CHUNK, RET_CHUNK, RET_CHUNK), gnw,
                              tl=1024, out_dtype=BF16)
    nwt = jnp.broadcast_to(mnw.reshape(moba_w, 1), (moba_w, MOBA_BLOCK))
    moba_ot = _moba_prompt(mqt, mk, mkt, mvt, nwt, out_dtype=BF16)
    ckt4 = jnp.transpose(cache_k[0], (0, 2, 3, 1))
    cvt4 = jnp.transpose(cache_v[0], (0, 2, 3, 1))
    y_prompt, kmt = _outffn(x_prompt, ret_o, moba_ot, mod_p[2], mod_p[3], mod_p[4], mod_p[5], n2w, fw,
                            wo, wg, wu, wd, tm=512, moba_transposed=True,
                            pool=(page_table, ckt4.reshape(n_pool, moba_w, page)))
    to_rows = lambda t: t.reshape(1, Bp, m_heads, HEAD_DIM, S).transpose(0, 1, 4, 2, 3)
    k_prompt, v_prompt = to_rows(mkt), to_rows(mvt)

    xs = x_sample.reshape(1, rows_s, D)
    cos_s, sin_s = _rope_tables(past_len + jnp.arange(rows_s) % T)
    sq, sk, sv, sg, smq, smk, smv = _inproj_sample(xs, mod_s[0], mod_s[1], n1w, cos_s, sin_s, w_in0)
    rows_pad = 4 * SUBLANES
    padc = lambda t: jnp.pad(t.reshape(Bd, T, ret_w), ((0, 0), (0, rows_pad - T), (0, 0)))
    dmat2_s, qdec_s, kdect_s, cdec_s = _decay_tables(r_heads, T, rows_pad, LANES)
    ret_os, ret_s = _retention_step(padc(sq), padc(sk), padc(sv), padc(sg), state_ret[0],
                                    (dmat2_s, qdec_s, kdect_s[:, :rows_pad].T, cdec_s), gnw)
    ret_os = ret_os[:, :T].reshape(1, rows_s, ret_w)

    assert kmt.shape[2] == n_full
    pad_rows = lambda t: jnp.pad(t.reshape(Bd, T, moba_w), ((0, 0), (0, pad8 - T), (0, 0)))
    q8 = pad_rows(smq)
    top = _topk(q8, kmt)[:, :T, :m_heads * MOBA_TOPK]
    moba_os = _attend(top.reshape(-1), page_table.reshape(-1), q8, pad_rows(smk), pad_rows(smv), mnw, ckt4, cvt4,
                      tokens=T, n_pages=n_pages)
    moba_os = moba_os[:, :T].reshape(1, rows_s, moba_w)
    y_sample = _outffn(xs, ret_os, moba_os, mod_s[2], mod_s[3], mod_s[4], mod_s[5], n2w, fw,
                       wo, wg, wu, wd, tm=rows_s, moba_transposed=False).reshape(Bd, T, D)
    k_sample = smk.reshape(1, Bd, T, m_heads, HEAD_DIM)
    v_sample = smv.reshape(1, Bd, T, m_heads, HEAD_DIM)

    return (y_prompt, y_sample, k_prompt, v_prompt, ret_p[None], k_sample, v_sample, ret_s[None])
```

```python
import functools
import math

import jax
import jax.numpy as jnp
from jax import lax
from jax.experimental import pallas as pl
from jax.experimental.pallas import tpu as pltpu

F32 = jnp.float32
BF16 = jnp.bfloat16
HIGHEST = lax.Precision.HIGHEST

HEAD_DIM = 64
HALF = HEAD_DIM // 2
RET_CHUNK = 128
MOBA_BLOCK = 256
MOBA_TOPK = 3
ROPE_BASE = 10000.0
NORM_EPS = 1e-6
LANES = 128
SUBLANES = 8
MXU_WIDTH = 256
VMEM_LIMIT = 56 * 1024 * 1024
NEG_INF = float("-inf")
MOBA_QSCALE = HEAD_DIM ** -0.5 * math.log2(math.e)
NT = (((1,), (1,)), ((), ()))
TN = (((0,), (0,)), ((), ()))


def _params(*sem):
    return pltpu.CompilerParams(dimension_semantics=sem, vmem_limit_bytes=VMEM_LIMIT)


def _rms(x, w):
    return x * lax.rsqrt(jnp.mean(x * x, axis=-1, keepdims=True) + NORM_EPS) * w


def _silu(x):
    return x * jax.nn.sigmoid(x)


def _const_spec(a):
    return pl.BlockSpec(a.shape, lambda *_: (0,) * a.ndim, pipeline_mode=pl.Buffered(1))


def _mod_kernel(c_ref, w_ref, b_ref, o_ref):
    s = _silu(c_ref[...])
    o_ref[...] = jnp.dot(s, w_ref[...], preferred_element_type=F32, precision=HIGHEST) + b_ref[...]


def _modulation(c, w_mod, b_mod):
    n, d = c.shape
    cols = w_mod.shape[1]
    tn = 1536
    return pl.pallas_call(
        _mod_kernel,
        out_shape=jax.ShapeDtypeStruct((n, cols), F32),
        grid=(cols // tn,),
        in_specs=[pl.BlockSpec((n, d), lambda j: (0, 0)),
                  pl.BlockSpec((d, tn), lambda j: (0, j)),
                  pl.BlockSpec((1, tn), lambda j: (0, j))],
        out_specs=pl.BlockSpec((n, tn), lambda j: (0, j)),
        compiler_params=_params("arbitrary"),
        name="modulation",
    )(c, w_mod, b_mod.reshape(1, cols))


def _mod_spec(arr, tm):
    if arr.shape[1] == 1:
        return pl.BlockSpec((None, 1, arr.shape[2]), lambda g, i, *_: (g, 0, 0))
    return pl.BlockSpec((None, tm, arr.shape[2]), lambda g, i, *_: (g, i, 0))


def _normed_input(x_ref, sh_ref, sc_ref, nw_ref):
    return (_rms(x_ref[...], nw_ref[...]) * (1.0 + sc_ref[...]) + sh_ref[...]).astype(BF16)


def _rope_store(z, cos, sin, o_ref, scale):
    lane = lax.broadcasted_iota(jnp.int32, cos.shape, 1)
    first_half = (lane % HEAD_DIM) < HALF
    for c in range(z.shape[1] // LANES):
        zc = z[:, c * LANES:(c + 1) * LANES]
        partner = jnp.where(first_half, pltpu.roll(zc, LANES - HALF, 1), pltpu.roll(zc, HALF, 1))
        o_ref[:, c * LANES:(c + 1) * LANES] = ((zc * cos + partner * sin) * scale).astype(o_ref.dtype)


def _inproj_sample_kernel(x_ref, sh_ref, sc_ref, nw_ref, cos_ref, sin_ref, w_ref,
                          rq_ref, rk_ref, rv_ref, rg_ref, mq_ref, mk_ref, mv_ref, *, width):
    h = _normed_input(x_ref, sh_ref, sc_ref, nw_ref)
    proj = lambda g: jnp.dot(h, w_ref[:, g * width:(g + 1) * width], preferred_element_type=F32)
    _rope_store(proj(0), cos_ref[...], sin_ref[...], rq_ref, 1.0)
    _rope_store(proj(1), cos_ref[...], sin_ref[...], rk_ref, HEAD_DIM ** -0.5)
    rv_ref[...] = proj(2)
    rg_ref[...] = _silu(proj(3))
    mq_ref[...] = proj(4) * MOBA_QSCALE
    mk_ref[...] = proj(5)
    mv_ref[...] = proj(6)


def _inproj_sample(x, sh, sc, nw, cos_t, sin_t, w_bf):
    G, R, D = x.shape
    width = w_bf.shape[1] // 7
    row = pl.BlockSpec((None, R, D), lambda g, i: (g, i, 0))
    tab = pl.BlockSpec((R, LANES), lambda g, i: (i, 0))
    act = pl.BlockSpec((None, R, width), lambda g, i: (g, i, 0))
    return pl.pallas_call(
        functools.partial(_inproj_sample_kernel, width=width),
        out_shape=[jax.ShapeDtypeStruct((G, R, width), F32)] * 7,
        grid=(G, 1),
        in_specs=[row, _mod_spec(sh, R), _mod_spec(sc, R), _const_spec(nw), tab, tab, _const_spec(w_bf)],
        out_specs=[act] * 7,
        compiler_params=_params("parallel", "parallel"),
        name="inproj_sample",
    )(x, sh, sc, nw, cos_t, sin_t, w_bf)


def _inproj_prompt_kernel(x_ref, sh_ref, sc_ref, nw_ref, cos_ref, sin_ref, cost_ref, sint_ref, w_ref, wt_ref,
                          rq_ref, rkt_ref, rv_ref, rg_ref, mqt_ref, mk_ref, mkt_ref, mvt_ref, *, width):
    h = _normed_input(x_ref, sh_ref, sc_ref, nw_ref)
    proj = lambda g: jnp.dot(h, w_ref[:, g * width:(g + 1) * width], preferred_element_type=F32)
    proj_t = lambda g: lax.dot_general(wt_ref[g * width:(g + 1) * width, :], h, NT, preferred_element_type=F32)
    _rope_store(proj(0), cos_ref[...], sin_ref[...], rq_ref, 1.0)
    rv_ref[...] = proj(1).astype(rv_ref.dtype)
    rg_ref[...] = _silu(proj(2)).astype(rg_ref.dtype)
    zt = proj_t(0)
    cost = cost_ref[...]
    sint = sint_ref[...]
    scale = HEAD_DIM ** -0.5
    for hd in range(width // HEAD_DIM):
        lo = slice(hd * HEAD_DIM, hd * HEAD_DIM + HALF)
        hi = slice(hd * HEAD_DIM + HALF, (hd + 1) * HEAD_DIM)
        a, b = zt[lo], zt[hi]
        rkt_ref[lo, :] = ((a * cost - b * sint) * scale).astype(rkt_ref.dtype)
        rkt_ref[hi, :] = ((a * sint + b * cost) * scale).astype(rkt_ref.dtype)
    mqt_ref[...] = (proj_t(1) * MOBA_QSCALE).astype(mqt_ref.dtype)
    mkt = proj_t(2)
    mkt_ref[...] = mkt
    mk_ref[...] = mkt.T.astype(mk_ref.dtype)
    mvt_ref[...] = proj_t(3)


def _inproj_prompt(x, sh, sc, nw, cos_t, sin_t, cos_tt, sin_tt, w_tok, w_trn, *, tm):
    G, R, D = x.shape
    width = w_tok.shape[1] // 3
    row = pl.BlockSpec((None, tm, D), lambda g, i: (g, i, 0))
    tab = pl.BlockSpec((tm, LANES), lambda g, i: (i, 0))
    tab_t = pl.BlockSpec((HALF, tm), lambda g, i: (0, i))
    act = pl.BlockSpec((None, tm, width), lambda g, i: (g, i, 0))
    act_t = pl.BlockSpec((None, width, tm), lambda g, i: (g, 0, i))
    tok = jax.ShapeDtypeStruct((G, R, width), BF16)
    trn = lambda dt: jax.ShapeDtypeStruct((G, width, R), dt)
    return pl.pallas_call(
        functools.partial(_inproj_prompt_kernel, width=width),
        out_shape=[tok, trn(BF16), tok, tok, trn(BF16), tok, trn(F32), trn(F32)],
        grid=(G, R // tm),
        in_specs=[row, _mod_spec(sh, tm), _mod_spec(sc, tm), _const_spec(nw), tab, tab, tab_t, tab_t,
                  _const_spec(w_tok), _const_spec(w_trn)],
        out_specs=[act, act_t, act, act, act_t, act, act_t, act_t],
        compiler_params=_params("parallel", "parallel"),
        name="inproj_prompt",
    )(x, sh, sc, nw, cos_t, sin_t, cos_tt, sin_tt, w_tok, w_trn)


def _retention_kernel(q_ref, kt_ref, v_ref, g_ref, s0_ref, dmat2_ref, qdec_ref, kdect_ref, cdec_ref, gnw_ref,
                      avg_ref, o_ref, sout_ref, s_scr, o_scr, *, heads, chunks):
    j = pl.program_id(1)
    pairs = heads // 2
    lane = lax.broadcasted_iota(jnp.int32, (LANES, LANES), 1)
    sub = lax.broadcasted_iota(jnp.int32, (LANES, LANES), 0)
    first = lane < HEAD_DIM
    diag = first == (sub < HEAD_DIM)
    keep_a = jnp.where(first, 1.0, 0.0).astype(BF16)
    keep_b = jnp.where(first, 0.0, 1.0).astype(BF16)

    @pl.when(j == 0)
    def _():
        z = jnp.zeros((HEAD_DIM, HEAD_DIM), F32)
        for p in range(pairs):
            s_scr[p] = jnp.concatenate([jnp.concatenate([s0_ref[2 * p], z], axis=1),
                                        jnp.concatenate([z, s0_ref[2 * p + 1]], axis=1)], axis=0)

    gnw = gnw_ref[...]
    avg = avg_ref[...]

    group_mean = functools.partial(_group_mean, avg=avg)

    for c in range(chunks):
        rows = slice(c * RET_CHUNK, (c + 1) * RET_CHUNK)
        for p in range(pairs):
            cs = slice(p * LANES, (p + 1) * LANES)
            qp = q_ref[rows, cs].astype(BF16)
            vp = v_ref[rows, cs].astype(BF16)
            ktp = kt_ref[cs, rows]
            q2 = jnp.concatenate([qp * keep_a, qp * keep_b], axis=0)
            att2 = jnp.dot(q2, ktp.astype(BF16), preferred_element_type=F32) * dmat2_ref[p]
            o2 = jnp.dot(att2.astype(BF16), vp, preferred_element_type=F32)
            s = s_scr[p]
            o = (jnp.where(first, o2[:RET_CHUNK], o2[RET_CHUNK:])
                 + jnp.dot(qp, s.astype(BF16), preferred_element_type=F32) * qdec_ref[:, cs])
            kdt = (ktp.astype(F32) * kdect_ref[cs, :]).astype(BF16)
            s_scr[p] = s * cdec_ref[p] + jnp.where(diag, jnp.dot(kdt, vp, preferred_element_type=F32), 0.0)
            o_scr[rows, cs] = o

    o_all = o_scr[...]
    d = o_all - group_mean(o_all)
    var = group_mean(d * d)
    o_ref[...] = (g_ref[...].astype(F32) * (d * lax.rsqrt(var + NORM_EPS) * gnw)).astype(o_ref.dtype)

    @pl.when(j == pl.num_programs(1) - 1)
    def _():
        for p in range(pairs):
            s = s_scr[p]
            sout_ref[2 * p] = s[:HEAD_DIM, :HEAD_DIM]
            sout_ref[2 * p + 1] = s[HEAD_DIM:, HEAD_DIM:]


def _retention(q, kt, v, gs, s0, tables, gnw, *, tl, out_dtype):
    B, L, W = q.shape
    heads = W // HEAD_DIM
    seq = pl.BlockSpec((None, tl, W), lambda b, j: (b, j, 0))
    seq_t = pl.BlockSpec((None, W, tl), lambda b, j: (b, 0, j))
    st = pl.BlockSpec((None, heads, HEAD_DIM, HEAD_DIM), lambda b, j: (b, 0, 0, 0))
    return pl.pallas_call(
        functools.partial(_retention_kernel, heads=heads, chunks=tl // RET_CHUNK),
        out_shape=[jax.ShapeDtypeStruct((B, L, W), out_dtype),
                   jax.ShapeDtypeStruct((B, heads, HEAD_DIM, HEAD_DIM), F32)],
        grid=(B, L // tl),
        in_specs=[seq, seq_t, seq, seq, st] + [_const_spec(t) for t in tables] + [_const_spec(gnw),
                                                                                   _const_spec(_head_average(W))],
        out_specs=[seq, st],
        scratch_shapes=[pltpu.VMEM((heads // 2, LANES, LANES), F32), pltpu.VMEM((tl, W), F32)],
        compiler_params=_params("parallel", "arbitrary"),
        name="retention",
    )(q, kt, v, gs, s0, *tables, gnw, _head_average(W))


def _retention_step_kernel(q_ref, k_ref, v_ref, g_ref, s0_ref, dmat2_ref, qdec_ref, kdec_ref, cdec_ref, gnw_ref,
                           avg_ref, o_ref, sout_ref, *, heads):
    seqs, rows, _ = q_ref.shape
    pairs = heads // 2
    lane = lax.broadcasted_iota(jnp.int32, (LANES, LANES), 1)
    sub = lax.broadcasted_iota(jnp.int32, (LANES, LANES), 0)
    diag = (lane < HEAD_DIM) == (sub < HEAD_DIM)
    first = lax.broadcasted_iota(jnp.int32, (rows, LANES), 1) < HEAD_DIM
    keep_a = jnp.where(first, 1.0, 0.0).astype(BF16)
    keep_b = jnp.where(first, 0.0, 1.0).astype(BF16)
    zpad = jnp.zeros((LANES - rows, LANES), BF16)
    zs = jnp.zeros((HEAD_DIM, HEAD_DIM), F32)
    avg = avg_ref[...]
    gnw = gnw_ref[...]

    group_mean = functools.partial(_group_mean, avg=avg)

    for s in range(seqs):
        outs = []
        for p in range(pairs):
            cs = slice(p * LANES, (p + 1) * LANES)
            qp = q_ref[s, :, cs].astype(BF16)
            kp = k_ref[s, :, cs]
            pad = lambda t: jnp.concatenate([t.astype(BF16), zpad], axis=0)
            kpad, vpad, kdpad = pad(kp), pad(v_ref[s, :, cs]), pad(kp * kdec_ref[:, cs])
            q2 = jnp.concatenate([qp * keep_a, qp * keep_b], axis=0)
            att2 = lax.dot_general(q2, kpad, NT, preferred_element_type=F32) * dmat2_ref[p]
            o2 = jnp.dot(att2.astype(BF16), vpad, preferred_element_type=F32)
            state = jnp.concatenate([jnp.concatenate([s0_ref[s, 2 * p], zs], axis=1),
                                     jnp.concatenate([zs, s0_ref[s, 2 * p + 1]], axis=1)], axis=0)
            outs.append(jnp.where(first, o2[:rows], o2[rows:])
                        + jnp.dot(qp, state.astype(BF16), preferred_element_type=F32) * qdec_ref[:, cs])
            state = state * cdec_ref[p] + jnp.where(
                diag, lax.dot_general(kdpad, vpad, TN, preferred_element_type=F32), 0.0)
            sout_ref[s, 2 * p] = state[:HEAD_DIM, :HEAD_DIM]
            sout_ref[s, 2 * p + 1] = state[HEAD_DIM:, HEAD_DIM:]
        o_all = jnp.concatenate(outs, axis=1)
        d = o_all - group_mean(o_all)
        var = group_mean(d * d)
        o_ref[s] = g_ref[s] * (d * lax.rsqrt(var + NORM_EPS) * gnw)


def _retention_step(q, k, v, gs, s0, tables, gnw):
    bd, rows, W = q.shape
    heads = W // HEAD_DIM
    seqs = max(s for s in (8, 4, 2, 1) if bd % s == 0)
    seq = pl.BlockSpec((seqs, rows, W), lambda b: (b, 0, 0))
    st = pl.BlockSpec((seqs, heads, HEAD_DIM, HEAD_DIM), lambda b: (b, 0, 0, 0))
    consts = list(tables) + [gnw, _head_average(W)]
    return pl.pallas_call(
        functools.partial(_retention_step_kernel, heads=heads),
        out_shape=[jax.ShapeDtypeStruct((bd, rows, W), F32),
                   jax.ShapeDtypeStruct((bd, heads, HEAD_DIM, HEAD_DIM), F32)],
        grid=(bd // seqs,),
        in_specs=[seq, seq, seq, seq, st] + [_const_spec(t) for t in consts],
        out_specs=[seq, st],
        compiler_params=_params("parallel"),
        name="retention_step",
    )(q, k, v, gs, s0, *consts)


def _group_mean(t, avg):
    hi = t.astype(BF16)
    lo = (t - hi.astype(F32)).astype(BF16)
    w = avg.shape[0]
    cols = [jnp.dot(hi[:, c:c + w], avg, preferred_element_type=F32)
            + jnp.dot(lo[:, c:c + w], avg, preferred_element_type=F32) for c in range(0, t.shape[1], w)]
    return jnp.concatenate(cols, axis=1) if len(cols) > 1 else cols[0]


def _head_average(width):
    width = min(width, MXU_WIDTH)
    r = jnp.arange(width) // HEAD_DIM
    return jnp.where(r[:, None] == r[None, :], 1.0 / HEAD_DIM, 0.0).astype(BF16)


def _decay_tables(heads, c_len, q_pad, k_pad):
    log_g = jnp.log1p(-jnp.exp2(-5.0 - jnp.arange(heads, dtype=F32)))
    idx = jnp.arange(c_len, dtype=F32)
    diff = idx[:, None] - idx[None, :]
    dmat = jnp.where(diff >= 0, jnp.exp(jnp.maximum(diff, 0.0) * log_g[:, None, None]), 0.0)
    q_dec = jnp.exp((idx + 1.0) * log_g[:, None])
    k_dec = jnp.exp((c_len - 1.0 - idx) * log_g[:, None])
    c_dec = jnp.exp(c_len * log_g)
    qp, kp = q_pad - c_len, k_pad - c_len
    dmat2 = jnp.pad(dmat, ((0, 0), (0, qp), (0, kp))).reshape(heads // 2, 2 * q_pad, k_pad)
    qdec = jnp.pad(jnp.repeat(q_dec.T, HEAD_DIM, axis=1), ((0, qp), (0, 0)))
    kdect = jnp.pad(jnp.repeat(k_dec, HEAD_DIM, axis=0), ((0, 0), (0, kp)))
    cdec = jnp.broadcast_to(jnp.repeat(c_dec, HEAD_DIM).reshape(heads // 2, 2 * HEAD_DIM, 1),
                            (heads // 2, 2 * HEAD_DIM, 2 * HEAD_DIM))
    return dmat2, qdec, kdect, cdec


def _col_reduce(x, op, final):
    while x.shape[0] % (2 * SUBLANES) == 0:
        half = x.shape[0] // 2
        x = op(x[:half], x[half:])
    return final(x, axis=0, keepdims=True)


def _moba_prompt_kernel(qt_ref, k_ref, kt_ref, vt_ref, nwt_ref, o_ref, *, seq):
    nb = seq // MOBA_BLOCK
    nbp = -(-nb // SUBLANES) * SUBLANES
    qb = MOBA_BLOCK
    pair = 2 * HEAD_DIM
    n_heads = qt_ref.shape[0] // HEAD_DIM
    rows_of = lambda h: slice(h * HEAD_DIM, (h + 1) * HEAD_DIM)
    pair_of = lambda h: slice((h // 2) * pair, (h // 2 + 1) * pair)
    km_heads = []
    for pr in ([slice(p * pair, (p + 1) * pair) for p in range(n_heads // 2)] if nb > MOBA_TOPK + 1 else []):
        lane_n = lax.broadcasted_iota(jnp.int32, (pair, LANES), 1)
        kmt = jnp.zeros((pair, LANES), F32)
        for n in range(nb):
            tot = kt_ref[pr, n * MOBA_BLOCK:n * MOBA_BLOCK + LANES]
            for c in range(1, MOBA_BLOCK // LANES):
                tot = tot + kt_ref[pr, n * MOBA_BLOCK + c * LANES:n * MOBA_BLOCK + (c + 1) * LANES]
            kmt = jnp.where(lane_n == n, jnp.sum(tot, axis=1, keepdims=True) * (1.0 / MOBA_BLOCK), kmt)
        km = kmt.T[0:nbp]
        lane_k = lax.broadcasted_iota(jnp.int32, km.shape, 1)
        km_heads += [jnp.where(lane_k < HEAD_DIM, km, 0.0), jnp.where(lane_k < HEAD_DIM, 0.0, km)]
    blk = lax.broadcasted_iota(jnp.int32, (nbp, qb), 0)
    key_i = lax.broadcasted_iota(jnp.int32, (qb, qb), 0)
    qry_i = lax.broadcasted_iota(jnp.int32, (qb, qb), 1)
    causal = key_i <= qry_i
    zeros = jnp.zeros((HEAD_DIM, qb), BF16)
    ones = jnp.ones((2 * SUBLANES, seq), BF16)
    vt_ones = [jnp.concatenate([vt_ref[rows_of(h), :].astype(BF16), ones], axis=0) for h in range(n_heads)]

    def scores(i, h):
        qh = qt_ref[rows_of(h), i * qb:(i + 1) * qb]
        qm = jnp.concatenate([qh, zeros] if h % 2 == 0 else [zeros, qh], axis=0)
        return jnp.dot(k_ref[0:(i + 1) * MOBA_BLOCK, pair_of(h)], qm, preferred_element_type=F32)

    def softmax(i, h, st):
        biases = [None] * i
        if i > MOBA_TOPK:
            qt = qt_ref[pair_of(h), i * qb:(i + 1) * qb].astype(F32)
            gate = jnp.dot(km_heads[h], qt, preferred_element_type=F32, precision=HIGHEST)
            valid = blk < i
            for n in range(i):
                gn = gate[n:n + 1, :]
                ahead = valid & ((gate > gn) | ((gate == gn) & (blk < n)))
                rank = jnp.sum(ahead.astype(F32), axis=0, keepdims=True)
                biases[n] = jnp.where(rank < MOBA_TOPK, 0.0, NEG_INF)
        past = [st[n * MOBA_BLOCK:(n + 1) * MOBA_BLOCK] for n in range(i)]
        own = jnp.where(causal, st[i * MOBA_BLOCK:(i + 1) * MOBA_BLOCK], NEG_INF)
        m = _col_reduce(own, jnp.maximum, jnp.max)
        for sb, bias in zip(past, biases):
            mb = _col_reduce(sb, jnp.maximum, jnp.max)
            m = jnp.maximum(m, mb if bias is None else mb + bias)
        pieces = [jnp.exp2(sb + (-m if bias is None else bias - m)).astype(BF16) for sb, bias in zip(past, biases)]
        pieces.append(jnp.exp2(own - m).astype(BF16))
        return jnp.concatenate(pieces, axis=0) if i > 0 else pieces[0]

    def output(i, h, p):
        hr = rows_of(h)
        ot = jnp.dot(vt_ones[h][:, 0:(i + 1) * MOBA_BLOCK], p, preferred_element_type=F32)
        ot = ot[0:HEAD_DIM] / ot[HEAD_DIM:HEAD_DIM + 1]
        ms = jnp.mean(ot * ot, axis=0, keepdims=True)
        o_ref[hr, i * qb:(i + 1) * qb] = (ot * lax.rsqrt(ms + NORM_EPS) * nwt_ref[hr, :]).astype(o_ref.dtype)

    bodies = [(i, h) for i in range(nb) for h in range(n_heads)]
    ahead, behind = 2, 2
    queue = [scores(*b) for b in bodies[:ahead]]
    pending = []
    for idx, body in enumerate(bodies):
        st = queue.pop(0)
        if idx + ahead < len(bodies):
            queue.append(scores(*bodies[idx + ahead]))
        pending.append((*body, softmax(*body, st)))
        if len(pending) > behind:
            output(*pending.pop(0))
    for item in pending:
        output(*item)


def _moba_prompt(qt, k, kt, vt, nwt, *, out_dtype):
    B, W, S = qt.shape
    pair = 2 * HEAD_DIM * (2 if W % (4 * HEAD_DIM) == 0 else 1)
    trn = pl.BlockSpec((None, pair, S), lambda b, h: (b, h, 0))
    return pl.pallas_call(
        functools.partial(_moba_prompt_kernel, seq=S),
        out_shape=jax.ShapeDtypeStruct((B, W, S), out_dtype),
        grid=(B, W // pair),
        in_specs=[trn, pl.BlockSpec((None, S, pair), lambda b, h: (b, 0, h)), trn, trn,
                  pl.BlockSpec((pair, MOBA_BLOCK), lambda b, h: (h, 0))],
        out_specs=trn,
        compiler_params=_params("parallel", "parallel"),
        name="moba_prompt",
    )(qt, k, kt, vt, nwt)


def _topk_kernel(q_ref, km_ref, o_ref, *, heads):
    seqs, rows, _ = q_ref.shape
    n_blk = km_ref.shape[2]
    lane_b = lax.broadcasted_iota(jnp.int32, (seqs * heads * rows, n_blk), 1).astype(F32)
    lane_o = lax.broadcasted_iota(jnp.int32, (rows, LANES), 1)
    gate = jnp.concatenate(
        [jnp.dot(q_ref[s, :, h * HEAD_DIM:(h + 1) * HEAD_DIM], km_ref[s, h * HEAD_DIM:(h + 1) * HEAD_DIM, :],
                 preferred_element_type=F32, precision=HIGHEST) for s in range(seqs) for h in range(heads)], axis=0)
    picks = []
    for r in range(MOBA_TOPK):
        m = jnp.max(gate, axis=1, keepdims=True)
        idx = jnp.min(jnp.where(gate == m, lane_b, float(n_blk)), axis=1, keepdims=True)
        gate = jnp.where(lane_b == idx, NEG_INF, gate)
        picks.append(idx)
    for s in range(seqs):
        out = jnp.zeros((rows, LANES), F32)
        for h in range(heads):
            base = (s * heads + h) * rows
            for r in range(MOBA_TOPK):
                out = jnp.where(lane_o == h * MOBA_TOPK + r, picks[r][base:base + rows], out)
        o_ref[s] = out.astype(jnp.int32)


def _topk(q8, kmt):
    bd, rows, W = q8.shape
    n_full = kmt.shape[2]
    seqs = max(s for s in (8, 4, 2, 1) if bd % s == 0)
    return pl.pallas_call(
        functools.partial(_topk_kernel, heads=W // HEAD_DIM),
        out_shape=jax.ShapeDtypeStruct((bd, rows, LANES), jnp.int32),
        grid=(bd // seqs,),
        in_specs=[pl.BlockSpec((seqs, rows, W), lambda b: (b, 0, 0)),
                  pl.BlockSpec((seqs, W, n_full), lambda b: (b, 0, 0))],
        out_specs=pl.BlockSpec((seqs, rows, LANES), lambda b: (b, 0, 0)),
        compiler_params=_params("parallel"),
        name="moba_topk",
    )(q8, kmt)


def _attend_kernel(top_ref, pt_ref, q_ref, kn_ref, vn_ref, nw_ref, ck_ref, cv_ref, o_ref,
                   kbuf, vbuf, sem, *, heads, tokens, n_pages, page):
    b = pl.program_id(0)
    nb = pl.num_programs(0)
    ppb = MOBA_BLOCK // page
    n_sel = MOBA_TOPK * MOBA_BLOCK
    slot = b % 2

    def page_copies(seq, buf_slot):
        out = []
        for h in range(heads):
            for t in range(tokens):
                for r in range(MOBA_TOPK):
                    blk = top_ref[((seq * tokens + t) * heads + h) * MOBA_TOPK + r]
                    for p in range(ppb):
                        phys = pt_ref[seq * n_pages + blk * ppb + p]
                        dst = pl.ds((r * ppb + p) * page, page)
                        out.append(pltpu.make_async_copy(ck_ref.at[phys, h], kbuf.at[buf_slot, h, t, :, dst],
                                                         sem.at[buf_slot, 0]))
                        out.append(pltpu.make_async_copy(cv_ref.at[phys, h], vbuf.at[buf_slot, h, t, :, dst],
                                                         sem.at[buf_slot, 1]))
        return out

    @pl.when(b == 0)
    def _():
        for c in page_copies(b, slot):
            c.start()

    @pl.when(b + 1 < nb)
    def _():
        for c in page_copies(b + 1, 1 - slot):
            c.start()

    for c in page_copies(b, slot):
        c.wait()
    kb = kbuf.at[slot]
    vb = vbuf.at[slot]
    zrows = jnp.zeros((LANES - kn_ref.shape[0], LANES), F32)
    for p in range(heads // 2):
        cs = slice(p * LANES, (p + 1) * LANES)
        knt = jnp.concatenate([kn_ref[:, cs], zrows], axis=0).T
        vnt = jnp.concatenate([vn_ref[:, cs], zrows], axis=0).T
        for hh in range(2):
            hr = slice(hh * HEAD_DIM, (hh + 1) * HEAD_DIM)
            for t in range(tokens):
                kb[2 * p + hh, t, :, n_sel:n_sel + LANES] = knt[hr]
                vb[2 * p + hh, t, :, n_sel:n_sel + LANES] = vnt[hr]

    rows = q_ref.shape[0]
    col = lax.broadcasted_iota(jnp.int32, (rows, n_sel + LANES), 1)
    row = lax.broadcasted_iota(jnp.int32, (rows, HEAD_DIM), 0)
    pairs = [(h, t) for h in range(heads) for t in range(tokens)]
    qs = [q_ref[:, h * HEAD_DIM:(h + 1) * HEAD_DIM].astype(BF16) for h in range(heads)]
    scores = [jnp.where(col <= n_sel + t,
                        jnp.dot(qs[h], kb[h, t].astype(BF16), preferred_element_type=F32), NEG_INF)
              for h, t in pairs]
    probs = []
    for s in scores:
        p = jnp.exp2(s - jnp.max(s, axis=1, keepdims=True))
        probs.append((p.astype(BF16), jnp.sum(p, axis=1, keepdims=True)))
    outs = [jnp.zeros((rows, HEAD_DIM), F32)] * heads
    for (h, t), (p, l) in zip(pairs, probs):
        o = lax.dot_general(p, vb[h, t].astype(BF16), NT, preferred_element_type=F32) / l
        outs[h] = jnp.where(row == t, o, outs[h])
    o_ref[...] = jnp.concatenate([_rms(outs[h], nw_ref[:, h * HEAD_DIM:(h + 1) * HEAD_DIM])
                                  for h in range(heads)], axis=1)


def _attend(top_flat, pt_flat, q8, kn8, vn8, nw, ckt4, cvt4, *, tokens, n_pages):
    bd, rows, W = q8.shape
    heads = W // HEAD_DIM
    page = ckt4.shape[3]
    n_keys = MOBA_TOPK * MOBA_BLOCK + LANES
    any_spec = pl.BlockSpec(memory_space=pl.ANY)
    return pl.pallas_call(
        functools.partial(_attend_kernel, heads=heads, tokens=tokens, n_pages=n_pages, page=page),
        out_shape=jax.ShapeDtypeStruct((bd, rows, W), F32),
        grid_spec=pltpu.PrefetchScalarGridSpec(
            num_scalar_prefetch=2,
            grid=(bd,),
            in_specs=[pl.BlockSpec((None, rows, W), lambda b, *_: (b, 0, 0)),
                      pl.BlockSpec((None, rows, W), lambda b, *_: (b, 0, 0)),
                      pl.BlockSpec((None, rows, W), lambda b, *_: (b, 0, 0)),
                      pl.BlockSpec((1, W), lambda b, *_: (0, 0)),
                      any_spec, any_spec],
            out_specs=pl.BlockSpec((None, rows, W), lambda b, *_: (b, 0, 0)),
            scratch_shapes=[pltpu.VMEM((2, heads, tokens, HEAD_DIM, n_keys), F32),
                            pltpu.VMEM((2, heads, tokens, HEAD_DIM, n_keys), F32),
                            pltpu.SemaphoreType.DMA((2, 2))],
        ),
        compiler_params=_params("arbitrary"),
        name="moba_attend",
    )(top_flat, pt_flat, q8, kn8, vn8, nw, ckt4, cvt4)


FF_CHUNKS = 3
SIDE_SLOTS = FF_CHUNKS + 1
POOL_SLOTS = 3


def _outffn_compute(x_ref, ro_ref, mo_ref, g1_ref, sh2_ref, sc2_ref, g2_ref, n2w_ref, fw_ref,
                    wo_ref, wg_ref, wu_ref, wd_ref, y_ref, *, ff_chunks, moba_transposed, side_work):
    side_work(0)
    rw = ro_ref.shape[1]
    mo = mo_ref[...].astype(BF16)
    attn = (jnp.dot(ro_ref[...].astype(BF16), wo_ref[0:rw, :], preferred_element_type=F32)
            + lax.dot_general(mo, wo_ref[rw:, :], TN if moba_transposed else (((1,), (0,)), ((), ())),
                              preferred_element_type=F32))
    x1 = x_ref[...] + g1_ref[...] * attn
    h2 = (_rms(x1, n2w_ref[...]) * (1.0 + sc2_ref[...]) + sh2_ref[...]).astype(BF16)
    acc = None
    for c, (lo, hi) in enumerate(ff_chunks):
        side_work(1 + c)
        gate = jnp.dot(h2, wg_ref[:, lo:hi], preferred_element_type=F32)
        up = jnp.dot(h2, wu_ref[:, lo:hi], preferred_element_type=F32)
        part = jnp.dot((_silu(gate) * up).astype(BF16), wd_ref[lo:hi, :], preferred_element_type=F32)
        acc = part if acc is None else acc + part
    x2 = x1 + g2_ref[...] * acc
    y_ref[...] = _rms(x2, fw_ref[...])


def _outffn_kernel(*refs, ff_chunks, moba_transposed):
    _outffn_compute(*refs, ff_chunks=ff_chunks, moba_transposed=moba_transposed, side_work=lambda c: None)


def _outffn_pool_kernel(pt_ref, *refs, ff_chunks, moba_transposed, inner_steps, pages_per_step):
    ck_ref, y_ref, km_ref, pbuf, sem = refs[-5:]
    step = pl.program_id(0) * inner_steps + pl.program_id(1)
    n_steps = pl.num_programs(0) * inner_steps
    n_pages = pt_ref.shape[1]
    steps_per_seq = n_pages // pages_per_step
    chunk_pages = pages_per_step // SIDE_SLOTS
    ppb = MOBA_BLOCK // pbuf.shape[3]
    chunk_blocks = chunk_pages // ppb
    lane = lax.broadcasted_iota(jnp.int32, km_ref.shape, 1)

    def chunk_slot(st, c):
        return lax.rem(st * SIDE_SLOTS + c, POOL_SLOTS)

    def chunk_copies(st, c):
        seq = st // steps_per_seq
        base = (st % steps_per_seq) * pages_per_step + c * chunk_pages
        slot = chunk_slot(st, c)
        return [pltpu.make_async_copy(ck_ref.at[pt_ref[seq, base + p]], pbuf.at[slot, p], sem.at[slot])
                for p in range(chunk_pages)]

    def side_work(c):
        if c == 0:
            @pl.when(step == 0)
            def _():
                for a in range(POOL_SLOTS - 1):
                    for cp in chunk_copies(step, a):
                        cp.start()

            @pl.when(step % steps_per_seq == 0)
            def _():
                km_ref[...] = jnp.zeros_like(km_ref)
        ahead = c + POOL_SLOTS - 1
        if ahead < SIDE_SLOTS:
            for cp in chunk_copies(step, ahead):
                cp.start()
        else:
            @pl.when(step + 1 < n_steps)
            def _():
                for cp in chunk_copies(step + 1, ahead - SIDE_SLOTS):
                    cp.start()
        for cp in chunk_copies(step, c):
            cp.wait()
        slot = chunk_slot(step, c)
        first_block = (step % steps_per_seq) * (pages_per_step // ppb) + c * chunk_blocks
        acc = km_ref[...]
        for n in range(chunk_blocks):
            tot = pbuf[slot, n * ppb]
            for p in range(1, ppb):
                tot = tot + pbuf[slot, n * ppb + p]
            col = jnp.sum(tot, axis=1, keepdims=True) * (1.0 / MOBA_BLOCK)
            acc = jnp.where(lane == first_block + n, col, acc)
        km_ref[...] = acc

    _outffn_compute(*refs[:-5], y_ref, ff_chunks=ff_chunks, moba_transposed=moba_transposed, side_work=side_work)


def _outffn(x, ro, mo, g1, sh2, sc2, g2, n2w, fw, wo, wg, wu, wd, *, tm, moba_transposed, pool=None):
    G, R, D = x.shape
    W = ro.shape[2]
    dff = wg.shape[1]
    cuts = [round(dff * c / FF_CHUNKS / 256) * 256 for c in range(FF_CHUNKS)] + [dff]
    ff_chunks = tuple(zip(cuts[:-1], cuts[1:]))
    inner = R // tm
    row = pl.BlockSpec((None, tm, D), lambda g, i, *_: (g, i, 0))
    act = pl.BlockSpec((None, tm, W), lambda g, i, *_: (g, i, 0))
    act_t = pl.BlockSpec((None, W, tm), lambda g, i, *_: (g, 0, i))
    in_specs = [row, act, act_t if moba_transposed else act,
                _mod_spec(g1, tm), _mod_spec(sh2, tm), _mod_spec(sc2, tm), _mod_spec(g2, tm),
                _const_spec(n2w), _const_spec(fw), _const_spec(wo), _const_spec(wg), _const_spec(wu),
                _const_spec(wd)]
    args = (x, ro, mo, g1, sh2, sc2, g2, n2w, fw, wo, wg, wu, wd)
    y_shape = jax.ShapeDtypeStruct((G, R, D), F32)
    if pool is None:
        return pl.pallas_call(
            functools.partial(_outffn_kernel, ff_chunks=ff_chunks, moba_transposed=moba_transposed),
            out_shape=y_shape, grid=(G, inner), in_specs=in_specs, out_specs=row,
            compiler_params=_params("parallel", "parallel"),
            name="outproj_ffn",
        )(*args)
    page_table, ckt = pool
    bd, n_pages = page_table.shape
    _, hd, page = ckt.shape
    ppb = MOBA_BLOCK // page
    n_steps = G * inner
    pages_per_step = bd * n_pages // n_steps
    assert pages_per_step * n_steps == bd * n_pages and n_pages % pages_per_step == 0
    assert pages_per_step % (SIDE_SLOTS * ppb) == 0 and POOL_SLOTS - 1 <= SIDE_SLOTS
    steps_per_seq = n_pages // pages_per_step
    n_blocks = n_pages // ppb
    km_spec = pl.BlockSpec((None, hd, n_blocks), lambda g, i, *_: ((g * inner + i) // steps_per_seq, 0, 0))
    return pl.pallas_call(
        functools.partial(_outffn_pool_kernel, ff_chunks=ff_chunks, moba_transposed=moba_transposed,
                          inner_steps=inner, pages_per_step=pages_per_step),
        out_shape=[y_shape, jax.ShapeDtypeStruct((bd, hd, n_blocks), F32)],
        grid_spec=pltpu.PrefetchScalarGridSpec(
            num_scalar_prefetch=1,
            grid=(G, inner),
            in_specs=in_specs + [pl.BlockSpec(memory_space=pl.ANY)],
            out_specs=[row, km_spec],
            scratch_shapes=[pltpu.VMEM((POOL_SLOTS, pages_per_step // SIDE_SLOTS, hd, page), F32),
                            pltpu.SemaphoreType.DMA((POOL_SLOTS,))],
        ),
        compiler_params=_params("arbitrary", "arbitrary"),
        name="outproj_ffn_pool",
    )(page_table, *args, ckt)


def _rope_angles(pos):
    inv = ROPE_BASE ** (-jnp.arange(HALF, dtype=F32) / HALF)
    ang = pos.astype(F32)[:, None] * inv[None, :]
    return jnp.cos(ang), jnp.sin(ang)


def _rope_tables(pos):
    cos, sin = _rope_angles(pos)
    reps = LANES // HEAD_DIM
    return jnp.tile(cos, (1, 2 * reps)), jnp.tile(jnp.concatenate([-sin, sin], axis=1), (1, reps))


def kernel(x_prompt, x_sample, cache_k, cache_v, state_ret, page_table, c_prompt, c_sample,
           norm1_w, norm2_w, final_w, w_mod, b_mod, w_in, ret_gn_w, moba_norm_w, w_out,
           w_gate, w_up, w_down):
    Bp, S, D = x_prompt.shape
    Bd, T, _ = x_sample.shape
    depth = w_in.shape[0]
    assert depth == 1, "single decoder layer"
    n_pool, page, m_heads = cache_k.shape[1], cache_k.shape[2], cache_k.shape[3]
    n_pages = page_table.shape[1]
    past_len = n_pages * page
    assert past_len % MOBA_BLOCK == 0 and MOBA_BLOCK % page == 0
    n_full = past_len // MOBA_BLOCK
    moba_w = m_heads * HEAD_DIM
    ret_w = (w_in.shape[2] - 3 * moba_w) // 4
    assert ret_w == moba_w, "the two head groups share one projection width"
    W = ret_w
    r_heads = ret_w // HEAD_DIM
    fw = final_w.reshape(1, D)
    rows_s = Bd * T
    pad8 = SUBLANES

    w_in0 = w_in[0].astype(BF16)
    col = lambda g: w_in0[:, g * W:(g + 1) * W]
    w_tok = jnp.concatenate([col(0), col(2), col(3)], axis=1)
    w_trn = jnp.concatenate([col(1), col(4), col(5), col(6)], axis=1).T
    wo, wg, wu, wd = (w[0].astype(BF16) for w in (w_out, w_gate, w_up, w_down))
    n1w, n2w = norm1_w[0].reshape(1, D), norm2_w[0].reshape(1, D)
    gnw, mnw = ret_gn_w[0].reshape(1, ret_w), moba_norm_w[0].reshape(1, moba_w)

    mod = _modulation(jnp.concatenate([c_prompt, c_sample], axis=0), w_mod[0], b_mod[0])
    mod_p = [m[:, None, :] for m in jnp.split(mod[:Bp], 6, axis=-1)]
    mod_s = [jnp.repeat(m, T, axis=0)[None] for m in jnp.split(mod[Bp:], 6, axis=-1)]

    cos_a, sin_a = _rope_angles(jnp.arange(S))
    cos_p, sin_p = _rope_tables(jnp.arange(S))
    rq, rkt, rv, rg, mqt, mk, mkt, mvt = _inproj_prompt(x_prompt, mod_p[0], mod_p[1], n1w, cos_p, sin_p,
                                                        cos_a.T, sin_a.T, w_tok, w_trn, tm=1024)
    s0_p = jnp.zeros((Bp, r_heads, HEAD_DIM, HEAD_DIM), F32)
    ret_o, ret_p = _retention(rq, rkt, rv, rg, s0_p, _decay_tables(r_heads, RET_CHUNK, RET_CHUNK, RET_CHUNK), gnw,
                              tl=min(S, 2048), out_dtype=BF16)
    nwt = jnp.broadcast_to(mnw.reshape(moba_w, 1), (moba_w, MOBA_BLOCK))
    moba_ot = _moba_prompt(mqt, mk, mkt, mvt, nwt, out_dtype=BF16)
    ckt4 = jnp.transpose(cache_k[0], (0, 2, 3, 1))
    cvt4 = jnp.transpose(cache_v[0], (0, 2, 3, 1))
    y_prompt, kmt = _outffn(x_prompt, ret_o, moba_ot, mod_p[2], mod_p[3], mod_p[4], mod_p[5], n2w, fw,
                            wo, wg, wu, wd, tm=512, moba_transposed=True,
                            pool=(page_table, ckt4.reshape(n_pool, moba_w, page)))
    to_rows = lambda t: t.reshape(1, Bp, m_heads, HEAD_DIM, S).transpose(0, 1, 4, 2, 3)
    k_prompt, v_prompt = to_rows(mkt), to_rows(mvt)

    xs = x_sample.reshape(1, rows_s, D)
    cos_s, sin_s = _rope_tables(past_len + jnp.arange(rows_s) % T)
    sq, sk, sv, sg, smq, smk, smv = _inproj_sample(xs, mod_s[0], mod_s[1], n1w, cos_s, sin_s, w_in0)
    rows_pad = 4 * SUBLANES
    padc = lambda t: jnp.pad(t.reshape(Bd, T, ret_w), ((0, 0), (0, rows_pad - T), (0, 0)))
    dmat2_s, qdec_s, kdect_s, cdec_s = _decay_tables(r_heads, T, rows_pad, LANES)
    ret_os, ret_s = _retention_step(padc(sq), padc(sk), padc(sv), padc(sg), state_ret[0],
                                    (dmat2_s, qdec_s, kdect_s[:, :rows_pad].T, cdec_s), gnw)
    ret_os = ret_os[:, :T].reshape(1, rows_s, ret_w)

    assert kmt.shape[2] == n_full
    pad_rows = lambda t: jnp.pad(t.reshape(Bd, T, moba_w), ((0, 0), (0, pad8 - T), (0, 0)))
    q8 = pad_rows(smq)
    top = _topk(q8, kmt)[:, :T, :m_heads * MOBA_TOPK]
    moba_os = _attend(top.reshape(-1), page_table.reshape(-1), q8, pad_rows(smk), pad_rows(smv), mnw, ckt4, cvt4,
                      tokens=T, n_pages=n_pages)
    moba_os = moba_os[:, :T].reshape(1, rows_s, moba_w)
    y_sample = _outffn(xs, ret_os, moba_os, mod_s[2], mod_s[3], mod_s[4], mod_s[5], n2w, fw,
                       wo, wg, wu, wd, tm=rows_s, moba_transposed=False).reshape(Bd, T, D)
    k_sample = smk.reshape(1, Bd, T, m_heads, HEAD_DIM)
    v_sample = smv.reshape(1, Bd, T, m_heads, HEAD_DIM)

    return (y_prompt, y_sample, k_prompt, v_prompt, ret_p[None], k_sample, v_sample, ret_s[None])
```

```python
import functools
import math

import jax
import jax.numpy as jnp
from jax import lax
from jax.experimental import pallas as pl
from jax.experimental.pallas import tpu as pltpu

F32 = jnp.float32
BF16 = jnp.bfloat16
HIGHEST = lax.Precision.HIGHEST

HEAD_DIM = 64
HALF = HEAD_DIM // 2
RET_CHUNK = 128
MOBA_BLOCK = 256
MOBA_TOPK = 3
ROPE_BASE = 10000.0
NORM_EPS = 1e-6
LANES = 128
SUBLANES = 8
MXU_WIDTH = 256
VMEM_LIMIT = 56 * 1024 * 1024
NEG_INF = float("-inf")
MOBA_QSCALE = HEAD_DIM ** -0.5 * math.log2(math.e)
NT = (((1,), (1,)), ((), ()))
TN = (((0,), (0,)), ((), ()))


def _params(*sem):
    return pltpu.CompilerParams(dimension_semantics=sem, vmem_limit_bytes=VMEM_LIMIT)


def _rms(x, w):
    return x * lax.rsqrt(jnp.mean(x * x, axis=-1, keepdims=True) + NORM_EPS) * w


def _silu(x):
    return x * jax.nn.sigmoid(x)


def _const_spec(a):
    return pl.BlockSpec(a.shape, lambda *_: (0,) * a.ndim, pipeline_mode=pl.Buffered(1))


def _mod_kernel(c_ref, w_ref, b_ref, o_ref):
    s = _silu(c_ref[...])
    o_ref[...] = jnp.dot(s, w_ref[...], preferred_element_type=F32, precision=HIGHEST) + b_ref[...]


def _modulation(c, w_mod, b_mod):
    n, d = c.shape
    cols = w_mod.shape[1]
    tn = 1536
    return pl.pallas_call(
        _mod_kernel,
        out_shape=jax.ShapeDtypeStruct((n, cols), F32),
        grid=(cols // tn,),
        in_specs=[pl.BlockSpec((n, d), lambda j: (0, 0)),
                  pl.BlockSpec((d, tn), lambda j: (0, j)),
                  pl.BlockSpec((1, tn), lambda j: (0, j))],
        out_specs=pl.BlockSpec((n, tn), lambda j: (0, j)),
        compiler_params=_params("arbitrary"),
        name="modulation",
    )(c, w_mod, b_mod.reshape(1, cols))


def _mod_spec(arr, tm):
    if arr.shape[1] == 1:
        return pl.BlockSpec((None, 1, arr.shape[2]), lambda g, i, *_: (g, 0, 0))
    return pl.BlockSpec((None, tm, arr.shape[2]), lambda g, i, *_: (g, i, 0))


def _normed_input(x_ref, sh_ref, sc_ref, nw_ref):
    return (_rms(x_ref[...], nw_ref[...]) * (1.0 + sc_ref[...]) + sh_ref[...]).astype(BF16)


def _rope_store(z, cos, sin, o_ref, scale):
    lane = lax.broadcasted_iota(jnp.int32, cos.shape, 1)
    first_half = (lane % HEAD_DIM) < HALF
    for c in range(z.shape[1] // LANES):
        zc = z[:, c * LANES:(c + 1) * LANES]
        partner = jnp.where(first_half, pltpu.roll(zc, LANES - HALF, 1), pltpu.roll(zc, HALF, 1))
        o_ref[:, c * LANES:(c + 1) * LANES] = ((zc * cos + partner * sin) * scale).astype(o_ref.dtype)


def _inproj_sample_kernel(x_ref, sh_ref, sc_ref, nw_ref, cos_ref, sin_ref, w_ref,
                          rq_ref, rk_ref, rv_ref, rg_ref, mq_ref, mk_ref, mv_ref, *, width):
    h = _normed_input(x_ref, sh_ref, sc_ref, nw_ref)
    proj = lambda g: jnp.dot(h, w_ref[:, g * width:(g + 1) * width], preferred_element_type=F32)
    _rope_store(proj(0), cos_ref[...], sin_ref[...], rq_ref, 1.0)
    _rope_store(proj(1), cos_ref[...], sin_ref[...], rk_ref, HEAD_DIM ** -0.5)
    rv_ref[...] = proj(2)
    rg_ref[...] = _silu(proj(3))
    mq_ref[...] = proj(4) * MOBA_QSCALE
    mk_ref[...] = proj(5)
    mv_ref[...] = proj(6)


def _inproj_sample(x, sh, sc, nw, cos_t, sin_t, w_bf):
    G, R, D = x.shape
    width = w_bf.shape[1] // 7
    row = pl.BlockSpec((None, R, D), lambda g, i: (g, i, 0))
    tab = pl.BlockSpec((R, LANES), lambda g, i: (i, 0))
    act = pl.BlockSpec((None, R, width), lambda g, i: (g, i, 0))
    return pl.pallas_call(
        functools.partial(_inproj_sample_kernel, width=width),
        out_shape=[jax.ShapeDtypeStruct((G, R, width), F32)] * 7,
        grid=(G, 1),
        in_specs=[row, _mod_spec(sh, R), _mod_spec(sc, R), _const_spec(nw), tab, tab, _const_spec(w_bf)],
        out_specs=[act] * 7,
        compiler_params=_params("parallel", "parallel"),
        name="inproj_sample",
    )(x, sh, sc, nw, cos_t, sin_t, w_bf)


def _inproj_prompt_kernel(x_ref, sh_ref, sc_ref, nw_ref, cos_ref, sin_ref, cost_ref, sint_ref, w_ref, wt_ref,
                          rq_ref, rkt_ref, rv_ref, rg_ref, mqt_ref, mk_ref, mkt_ref, mvt_ref, *, width):
    h = _normed_input(x_ref, sh_ref, sc_ref, nw_ref)
    proj = lambda g: jnp.dot(h, w_ref[:, g * width:(g + 1) * width], preferred_element_type=F32)
    proj_t = lambda g: lax.dot_general(wt_ref[g * width:(g + 1) * width, :], h, NT, preferred_element_type=F32)
    _rope_store(proj(0), cos_ref[...], sin_ref[...], rq_ref, 1.0)
    rv_ref[...] = proj(1).astype(rv_ref.dtype)
    rg_ref[...] = _silu(proj(2)).astype(rg_ref.dtype)
    zt = proj_t(0)
    cost = cost_ref[...]
    sint = sint_ref[...]
    scale = HEAD_DIM ** -0.5
    for hd in range(width // HEAD_DIM):
        lo = slice(hd * HEAD_DIM, hd * HEAD_DIM + HALF)
        hi = slice(hd * HEAD_DIM + HALF, (hd + 1) * HEAD_DIM)
        a, b = zt[lo], zt[hi]
        rkt_ref[lo, :] = ((a * cost - b * sint) * scale).astype(rkt_ref.dtype)
        rkt_ref[hi, :] = ((a * sint + b * cost) * scale).astype(rkt_ref.dtype)
    mqt_ref[...] = (proj_t(1) * MOBA_QSCALE).astype(mqt_ref.dtype)
    mkt = proj_t(2)
    mkt_ref[...] = mkt
    mk_ref[...] = mkt.T.astype(mk_ref.dtype)
    mvt_ref[...] = proj_t(3)


def _inproj_prompt(x, sh, sc, nw, cos_t, sin_t, cos_tt, sin_tt, w_tok, w_trn, *, tm):
    G, R, D = x.shape
    width = w_tok.shape[1] // 3
    row = pl.BlockSpec((None, tm, D), lambda g, i: (g, i, 0))
    tab = pl.BlockSpec((tm, LANES), lambda g, i: (i, 0))
    tab_t = pl.BlockSpec((HALF, tm), lambda g, i: (0, i))
    act = pl.BlockSpec((None, tm, width), lambda g, i: (g, i, 0))
    act_t = pl.BlockSpec((None, width, tm), lambda g, i: (g, 0, i))
    tok = jax.ShapeDtypeStruct((G, R, width), BF16)
    trn = lambda dt: jax.ShapeDtypeStruct((G, width, R), dt)
    return pl.pallas_call(
        functools.partial(_inproj_prompt_kernel, width=width),
        out_shape=[tok, trn(BF16), tok, tok, trn(BF16), tok, trn(F32), trn(F32)],
        grid=(G, R // tm),
        in_specs=[row, _mod_spec(sh, tm), _mod_spec(sc, tm), _const_spec(nw), tab, tab, tab_t, tab_t,
                  _const_spec(w_tok), _const_spec(w_trn)],
        out_specs=[act, act_t, act, act, act_t, act, act_t, act_t],
        compiler_params=_params("parallel", "parallel"),
        name="inproj_prompt",
    )(x, sh, sc, nw, cos_t, sin_t, cos_tt, sin_tt, w_tok, w_trn)


def _retention_kernel(q_ref, kt_ref, v_ref, g_ref, s0_ref, dmat2_ref, qdec_ref, kdect_ref, cdec_ref, gnw_ref,
                      avg_ref, o_ref, sout_ref, s_scr, o_scr, *, heads, chunks):
    j = pl.program_id(1)
    pairs = heads // 2
    lane = lax.broadcasted_iota(jnp.int32, (LANES, LANES), 1)
    sub = lax.broadcasted_iota(jnp.int32, (LANES, LANES), 0)
    first = lane < HEAD_DIM
    diag = first == (sub < HEAD_DIM)
    keep_a = jnp.where(first, 1.0, 0.0).astype(BF16)
    keep_b = jnp.where(first, 0.0, 1.0).astype(BF16)

    @pl.when(j == 0)
    def _():
        z = jnp.zeros((HEAD_DIM, HEAD_DIM), F32)
        for p in range(pairs):
            s_scr[p] = jnp.concatenate([jnp.concatenate([s0_ref[2 * p], z], axis=1),
                                        jnp.concatenate([z, s0_ref[2 * p + 1]], axis=1)], axis=0)

    gnw = gnw_ref[...]
    avg = avg_ref[...]

    group_mean = functools.partial(_group_mean, avg=avg)

    for c in range(chunks):
        rows = slice(c * RET_CHUNK, (c + 1) * RET_CHUNK)
        for p in range(pairs):
            cs = slice(p * LANES, (p + 1) * LANES)
            qp = q_ref[rows, cs].astype(BF16)
            vp = v_ref[rows, cs].astype(BF16)
            ktp = kt_ref[cs, rows]
            q2 = jnp.concatenate([qp * keep_a, qp * keep_b], axis=0)
            att2 = jnp.dot(q2, ktp.astype(BF16), preferred_element_type=F32) * dmat2_ref[p]
            o2 = jnp.dot(att2.astype(BF16), vp, preferred_element_type=F32)
            s = s_scr[p]
            o = (jnp.where(first, o2[:RET_CHUNK], o2[RET_CHUNK:])
                 + jnp.dot(qp, s.astype(BF16), preferred_element_type=F32) * qdec_ref[:, cs])
            kdt = (ktp.astype(F32) * kdect_ref[cs, :]).astype(BF16)
            s_scr[p] = s * cdec_ref[p] + jnp.where(diag, jnp.dot(kdt, vp, preferred_element_type=F32), 0.0)
            o_scr[rows, cs] = o

    o_all = o_scr[...]
    d = o_all - group_mean(o_all)
    var = group_mean(d * d)
    o_ref[...] = (g_ref[...].astype(F32) * (d * lax.rsqrt(var + NORM_EPS) * gnw)).astype(o_ref.dtype)

    @pl.when(j == pl.num_programs(1) - 1)
    def _():
        for p in range(pairs):
            s = s_scr[p]
            sout_ref[2 * p] = s[:HEAD_DIM, :HEAD_DIM]
            sout_ref[2 * p + 1] = s[HEAD_DIM:, HEAD_DIM:]


def _retention(q, kt, v, gs, s0, tables, gnw, *, tl, out_dtype):
    B, L, W = q.shape
    heads = W // HEAD_DIM
    seq = pl.BlockSpec((None, tl, W), lambda b, j: (b, j, 0))
    seq_t = pl.BlockSpec((None, W, tl), lambda b, j: (b, 0, j))
    st = pl.BlockSpec((None, heads, HEAD_DIM, HEAD_DIM), lambda b, j: (b, 0, 0, 0))
    return pl.pallas_call(
        functools.partial(_retention_kernel, heads=heads, chunks=tl // RET_CHUNK),
        out_shape=[jax.ShapeDtypeStruct((B, L, W), out_dtype),
                   jax.ShapeDtypeStruct((B, heads, HEAD_DIM, HEAD_DIM), F32)],
        grid=(B, L // tl),
        in_specs=[seq, seq_t, seq, seq, st] + [_const_spec(t) for t in tables] + [_const_spec(gnw),
                                                                                   _const_spec(_head_average(W))],
        out_specs=[seq, st],
        scratch_shapes=[pltpu.VMEM((heads // 2, LANES, LANES), F32), pltpu.VMEM((tl, W), F32)],
        compiler_params=_params("parallel", "arbitrary"),
        name="retention",
    )(q, kt, v, gs, s0, *tables, gnw, _head_average(W))


def _retention_step_kernel(q_ref, k_ref, v_ref, g_ref, s0_ref, dmat2_ref, qdec_ref, kdec_ref, cdec_ref, gnw_ref,
                           avg_ref, o_ref, sout_ref, *, heads):
    seqs, rows, _ = q_ref.shape
    pairs = heads // 2
    lane = lax.broadcasted_iota(jnp.int32, (LANES, LANES), 1)
    sub = lax.broadcasted_iota(jnp.int32, (LANES, LANES), 0)
    diag = (lane < HEAD_DIM) == (sub < HEAD_DIM)
    first = lax.broadcasted_iota(jnp.int32, (rows, LANES), 1) < HEAD_DIM
    keep_a = jnp.where(first, 1.0, 0.0).astype(BF16)
    keep_b = jnp.where(first, 0.0, 1.0).astype(BF16)
    zpad = jnp.zeros((LANES - rows, LANES), BF16)
    zs = jnp.zeros((HEAD_DIM, HEAD_DIM), F32)
    avg = avg_ref[...]
    gnw = gnw_ref[...]

    group_mean = functools.partial(_group_mean, avg=avg)

    for s in range(seqs):
        outs = []
        for p in range(pairs):
            cs = slice(p * LANES, (p + 1) * LANES)
            qp = q_ref[s, :, cs].astype(BF16)
            kp = k_ref[s, :, cs]
            pad = lambda t: jnp.concatenate([t.astype(BF16), zpad], axis=0)
            kpad, vpad, kdpad = pad(kp), pad(v_ref[s, :, cs]), pad(kp * kdec_ref[:, cs])
            q2 = jnp.concatenate([qp * keep_a, qp * keep_b], axis=0)
            att2 = lax.dot_general(q2, kpad, NT, preferred_element_type=F32) * dmat2_ref[p]
            o2 = jnp.dot(att2.astype(BF16), vpad, preferred_element_type=F32)
            state = jnp.concatenate([jnp.concatenate([s0_ref[s, 2 * p], zs], axis=1),
                                     jnp.concatenate([zs, s0_ref[s, 2 * p + 1]], axis=1)], axis=0)
            outs.append(jnp.where(first, o2[:rows], o2[rows:])
                        + jnp.dot(qp, state.astype(BF16), preferred_element_type=F32) * qdec_ref[:, cs])
            state = state * cdec_ref[p] + jnp.where(
                diag, lax.dot_general(kdpad, vpad, TN, preferred_element_type=F32), 0.0)
            sout_ref[s, 2 * p] = state[:HEAD_DIM, :HEAD_DIM]
            sout_ref[s, 2 * p + 1] = state[HEAD_DIM:, HEAD_DIM:]
        o_all = jnp.concatenate(outs, axis=1)
        d = o_all - group_mean(o_all)
        var = group_mean(d * d)
        o_ref[s] = g_ref[s] * (d * lax.rsqrt(var + NORM_EPS) * gnw)


def _retention_step(q, k, v, gs, s0, tables, gnw):
    bd, rows, W = q.shape
    heads = W // HEAD_DIM
    seqs = max(s for s in (8, 4, 2, 1) if bd % s == 0)
    seq = pl.BlockSpec((seqs, rows, W), lambda b: (b, 0, 0))
    st = pl.BlockSpec((seqs, heads, HEAD_DIM, HEAD_DIM), lambda b: (b, 0, 0, 0))
    consts = list(tables) + [gnw, _head_average(W)]
    return pl.pallas_call(
        functools.partial(_retention_step_kernel, heads=heads),
        out_shape=[jax.ShapeDtypeStruct((bd, rows, W), F32),
                   jax.ShapeDtypeStruct((bd, heads, HEAD_DIM, HEAD_DIM), F32)],
        grid=(bd // seqs,),
        in_specs=[seq, seq, seq, seq, st] + [_const_spec(t) for t in consts],
        out_specs=[seq, st],
        compiler_params=_params("parallel"),
        name="retention_step",
    )(q, k, v, gs, s0, *consts)


def _group_mean(t, avg):
    hi = t.astype(BF16)
    lo = (t - hi.astype(F32)).astype(BF16)
    w = avg.shape[0]
    cols = [jnp.dot(hi[:, c:c + w], avg, preferred_element_type=F32)
            + jnp.dot(lo[:, c:c + w], avg, preferred_element_type=F32) for c in range(0, t.shape[1], w)]
    return jnp.concatenate(cols, axis=1) if len(cols) > 1 else cols[0]


def _head_average(width):
    width = min(width, MXU_WIDTH)
    r = jnp.arange(width) // HEAD_DIM
    return jnp.where(r[:, None] == r[None, :], 1.0 / HEAD_DIM, 0.0).astype(BF16)


def _decay_tables(heads, c_len, q_pad, k_pad):
    log_g = jnp.log1p(-jnp.exp2(-5.0 - jnp.arange(heads, dtype=F32)))
    idx = jnp.arange(c_len, dtype=F32)
    diff = idx[:, None] - idx[None, :]
    dmat = jnp.where(diff >= 0, jnp.exp(jnp.maximum(diff, 0.0) * log_g[:, None, None]), 0.0)
    q_dec = jnp.exp((idx + 1.0) * log_g[:, None])
    k_dec = jnp.exp((c_len - 1.0 - idx) * log_g[:, None])
    c_dec = jnp.exp(c_len * log_g)
    qp, kp = q_pad - c_len, k_pad - c_len
    dmat2 = jnp.pad(dmat, ((0, 0), (0, qp), (0, kp))).reshape(heads // 2, 2 * q_pad, k_pad)
    qdec = jnp.pad(jnp.repeat(q_dec.T, HEAD_DIM, axis=1), ((0, qp), (0, 0)))
    kdect = jnp.pad(jnp.repeat(k_dec, HEAD_DIM, axis=0), ((0, 0), (0, kp)))
    cdec = jnp.broadcast_to(jnp.repeat(c_dec, HEAD_DIM).reshape(heads // 2, 2 * HEAD_DIM, 1),
                            (heads // 2, 2 * HEAD_DIM, 2 * HEAD_DIM))
    return dmat2, qdec, kdect, cdec


def _col_reduce(x, op, final):
    while x.shape[0] % (2 * SUBLANES) == 0:
        half = x.shape[0] // 2
        x = op(x[:half], x[half:])
    return final(x, axis=0, keepdims=True)


def _moba_prompt_kernel(qt_ref, k_ref, kt_ref, vt_ref, nwt_ref, o_ref, *, seq):
    nb = seq // MOBA_BLOCK
    nbp = -(-nb // SUBLANES) * SUBLANES
    qb = MOBA_BLOCK
    pair = 2 * HEAD_DIM
    n_heads = qt_ref.shape[0] // HEAD_DIM
    rows_of = lambda h: slice(h * HEAD_DIM, (h + 1) * HEAD_DIM)
    pair_of = lambda h: slice((h // 2) * pair, (h // 2 + 1) * pair)
    km_heads = []
    for pr in ([slice(p * pair, (p + 1) * pair) for p in range(n_heads // 2)] if nb > MOBA_TOPK + 1 else []):
        lane_n = lax.broadcasted_iota(jnp.int32, (pair, LANES), 1)
        kmt = jnp.zeros((pair, LANES), F32)
        for n in range(nb):
            tot = kt_ref[pr, n * MOBA_BLOCK:n * MOBA_BLOCK + LANES]
            for c in range(1, MOBA_BLOCK // LANES):
                tot = tot + kt_ref[pr, n * MOBA_BLOCK + c * LANES:n * MOBA_BLOCK + (c + 1) * LANES]
            kmt = jnp.where(lane_n == n, jnp.sum(tot, axis=1, keepdims=True) * (1.0 / MOBA_BLOCK), kmt)
        km = kmt.T[0:nbp]
        lane_k = lax.broadcasted_iota(jnp.int32, km.shape, 1)
        km_heads += [jnp.where(lane_k < HEAD_DIM, km, 0.0), jnp.where(lane_k < HEAD_DIM, 0.0, km)]
    blk = lax.broadcasted_iota(jnp.int32, (nbp, qb), 0)
    key_i = lax.broadcasted_iota(jnp.int32, (qb, qb), 0)
    qry_i = lax.broadcasted_iota(jnp.int32, (qb, qb), 1)
    causal = key_i <= qry_i
    zeros = jnp.zeros((HEAD_DIM, qb), BF16)
    ones = jnp.ones((2 * SUBLANES, seq), BF16)
    vt_ones = [jnp.concatenate([vt_ref[rows_of(h), :].astype(BF16), ones], axis=0) for h in range(n_heads)]

    def scores(i, h):
        qh = qt_ref[rows_of(h), i * qb:(i + 1) * qb]
        qm = jnp.concatenate([qh, zeros] if h % 2 == 0 else [zeros, qh], axis=0)
        return jnp.dot(k_ref[0:(i + 1) * MOBA_BLOCK, pair_of(h)], qm, preferred_element_type=F32)

    def softmax(i, h, st):
        biases = [None] * i
        if i > MOBA_TOPK:
            qt = qt_ref[pair_of(h), i * qb:(i + 1) * qb].astype(F32)
            gate = jnp.dot(km_heads[h], qt, preferred_element_type=F32, precision=HIGHEST)
            valid = blk < i
            for n in range(i):
                gn = gate[n:n + 1, :]
                ahead = valid & ((gate > gn) | ((gate == gn) & (blk < n)))
                rank = jnp.sum(ahead.astype(F32), axis=0, keepdims=True)
                biases[n] = jnp.where(rank < MOBA_TOPK, 0.0, NEG_INF)
        past = [st[n * MOBA_BLOCK:(n + 1) * MOBA_BLOCK] for n in range(i)]
        own = jnp.where(causal, st[i * MOBA_BLOCK:(i + 1) * MOBA_BLOCK], NEG_INF)
        m = _col_reduce(own, jnp.maximum, jnp.max)
        for sb, bias in zip(past, biases):
            mb = _col_reduce(sb, jnp.maximum, jnp.max)
            m = jnp.maximum(m, mb if bias is None else mb + bias)
        pieces = [jnp.exp2(sb + (-m if bias is None else bias - m)).astype(BF16) for sb, bias in zip(past, biases)]
        pieces.append(jnp.exp2(own - m).astype(BF16))
        return jnp.concatenate(pieces, axis=0) if i > 0 else pieces[0]

    def output(i, h, p):
        hr = rows_of(h)
        ot = jnp.dot(vt_ones[h][:, 0:(i + 1) * MOBA_BLOCK], p, preferred_element_type=F32)
        ot = ot[0:HEAD_DIM] / ot[HEAD_DIM:HEAD_DIM + 1]
        ms = jnp.mean(ot * ot, axis=0, keepdims=True)
        o_ref[hr, i * qb:(i + 1) * qb] = (ot * lax.rsqrt(ms + NORM_EPS) * nwt_ref[hr, :]).astype(o_ref.dtype)

    bodies = [(i, h) for i in range(nb) for h in range(n_heads)]
    ahead, behind = 2, 2
    queue = [scores(*b) for b in bodies[:ahead]]
    pending = []
    for idx, body in enumerate(bodies):
        st = queue.pop(0)
        if idx + ahead < len(bodies):
            queue.append(scores(*bodies[idx + ahead]))
        pending.append((*body, softmax(*body, st)))
        if len(pending) > behind:
            output(*pending.pop(0))
    for item in pending:
        output(*item)


def _moba_prompt(qt, k, kt, vt, nwt, *, out_dtype):
    B, W, S = qt.shape
    pair = 2 * HEAD_DIM * (2 if W % (4 * HEAD_DIM) == 0 else 1)
    trn = pl.BlockSpec((None, pair, S), lambda b, h: (b, h, 0))
    return pl.pallas_call(
        functools.partial(_moba_prompt_kernel, seq=S),
        out_shape=jax.ShapeDtypeStruct((B, W, S), out_dtype),
        grid=(B, W // pair),
        in_specs=[trn, pl.BlockSpec((None, S, pair), lambda b, h: (b, 0, h)), trn, trn,
                  pl.BlockSpec((pair, MOBA_BLOCK), lambda b, h: (h, 0))],
        out_specs=trn,
        compiler_params=_params("parallel", "parallel"),
        name="moba_prompt",
    )(qt, k, kt, vt, nwt)


def _topk_kernel(q_ref, km_ref, o_ref, *, heads):
    seqs, rows, _ = q_ref.shape
    n_blk = km_ref.shape[2]
    lane_b = lax.broadcasted_iota(jnp.int32, (seqs * heads * rows, n_blk), 1).astype(F32)
    lane_o = lax.broadcasted_iota(jnp.int32, (rows, LANES), 1)
    gate = jnp.concatenate(
        [jnp.dot(q_ref[s, :, h * HEAD_DIM:(h + 1) * HEAD_DIM], km_ref[s, h * HEAD_DIM:(h + 1) * HEAD_DIM, :],
                 preferred_element_type=F32, precision=HIGHEST) for s in range(seqs) for h in range(heads)], axis=0)
    picks = []
    for r in range(MOBA_TOPK):
        m = jnp.max(gate, axis=1, keepdims=True)
        idx = jnp.min(jnp.where(gate == m, lane_b, float(n_blk)), axis=1, keepdims=True)
        gate = jnp.where(lane_b == idx, NEG_INF, gate)
        picks.append(idx)
    for s in range(seqs):
        out = jnp.zeros((rows, LANES), F32)
        for h in range(heads):
            base = (s * heads + h) * rows
            for r in range(MOBA_TOPK):
                out = jnp.where(lane_o == h * MOBA_TOPK + r, picks[r][base:base + rows], out)
        o_ref[s] = out.astype(jnp.int32)


def _topk(q8, kmt):
    bd, rows, W = q8.shape
    n_full = kmt.shape[2]
    seqs = max(s for s in (8, 4, 2, 1) if bd % s == 0)
    return pl.pallas_call(
        functools.partial(_topk_kernel, heads=W // HEAD_DIM),
        out_shape=jax.ShapeDtypeStruct((bd, rows, LANES), jnp.int32),
        grid=(bd // seqs,),
        in_specs=[pl.BlockSpec((seqs, rows, W), lambda b: (b, 0, 0)),
                  pl.BlockSpec((seqs, W, n_full), lambda b: (b, 0, 0))],
        out_specs=pl.BlockSpec((seqs, rows, LANES), lambda b: (b, 0, 0)),
        compiler_params=_params("parallel"),
        name="moba_topk",
    )(q8, kmt)


def _attend_kernel(top_ref, pt_ref, q_ref, kn_ref, vn_ref, nw_ref, ck_ref, cv_ref, o_ref,
                   kbuf, vbuf, sem, *, heads, tokens, n_pages, page):
    b = pl.program_id(0)
    nb = pl.num_programs(0)
    ppb = MOBA_BLOCK // page
    n_sel = MOBA_TOPK * MOBA_BLOCK
    slot = b % 2

    def page_copies(seq, buf_slot):
        out = []
        for h in range(heads):
            for t in range(tokens):
                for r in range(MOBA_TOPK):
                    blk = top_ref[((seq * tokens + t) * heads + h) * MOBA_TOPK + r]
                    for p in range(ppb):
                        phys = pt_ref[seq * n_pages + blk * ppb + p]
                        dst = pl.ds((r * ppb + p) * page, page)
                        out.append(pltpu.make_async_copy(ck_ref.at[phys, h], kbuf.at[buf_slot, h, t, :, dst],
                                                         sem.at[buf_slot, 0]))
                        out.append(pltpu.make_async_copy(cv_ref.at[phys, h], vbuf.at[buf_slot, h, t, :, dst],
                                                         sem.at[buf_slot, 1]))
        return out

    @pl.when(b == 0)
    def _():
        for n, c in enumerate(page_copies(b, slot)):
            c.start(priority=n % 2)

    @pl.when(b + 1 < nb)
    def _():
        for n, c in enumerate(page_copies(b + 1, 1 - slot)):
            c.start(priority=n % 2)

    for c in page_copies(b, slot):
        c.wait()
    kb = kbuf.at[slot]
    vb = vbuf.at[slot]
    zrows = jnp.zeros((LANES - kn_ref.shape[0], LANES), F32)
    for p in range(heads // 2):
        cs = slice(p * LANES, (p + 1) * LANES)
        knt = jnp.concatenate([kn_ref[:, cs], zrows], axis=0).T
        vnt = jnp.concatenate([vn_ref[:, cs], zrows], axis=0).T
        for hh in range(2):
            hr = slice(hh * HEAD_DIM, (hh + 1) * HEAD_DIM)
            for t in range(tokens):
                kb[2 * p + hh, t, :, n_sel:n_sel + LANES] = knt[hr]
                vb[2 * p + hh, t, :, n_sel:n_sel + LANES] = vnt[hr]

    rows = q_ref.shape[0]
    col = lax.broadcasted_iota(jnp.int32, (rows, n_sel + LANES), 1)
    row = lax.broadcasted_iota(jnp.int32, (rows, HEAD_DIM), 0)
    pairs = [(h, t) for h in range(heads) for t in range(tokens)]
    qs = [q_ref[:, h * HEAD_DIM:(h + 1) * HEAD_DIM].astype(BF16) for h in range(heads)]
    scores = [jnp.where(col <= n_sel + t,
                        jnp.dot(qs[h], kb[h, t].astype(BF16), preferred_element_type=F32), NEG_INF)
              for h, t in pairs]
    probs = []
    for s in scores:
        p = jnp.exp2(s - jnp.max(s, axis=1, keepdims=True))
        probs.append((p.astype(BF16), jnp.sum(p, axis=1, keepdims=True)))
    outs = [jnp.zeros((rows, HEAD_DIM), F32)] * heads
    for (h, t), (p, l) in zip(pairs, probs):
        o = lax.dot_general(p, vb[h, t].astype(BF16), NT, preferred_element_type=F32) / l
        outs[h] = jnp.where(row == t, o, outs[h])
    o_ref[...] = jnp.concatenate([_rms(outs[h], nw_ref[:, h * HEAD_DIM:(h + 1) * HEAD_DIM])
                                  for h in range(heads)], axis=1)


def _attend(top_flat, pt_flat, q8, kn8, vn8, nw, ckt4, cvt4, *, tokens, n_pages):
    bd, rows, W = q8.shape
    heads = W // HEAD_DIM
    page = ckt4.shape[3]
    n_keys = MOBA_TOPK * MOBA_BLOCK + LANES
    any_spec = pl.BlockSpec(memory_space=pl.ANY)
    return pl.pallas_call(
        functools.partial(_attend_kernel, heads=heads, tokens=tokens, n_pages=n_pages, page=page),
        out_shape=jax.ShapeDtypeStruct((bd, rows, W), F32),
        grid_spec=pltpu.PrefetchScalarGridSpec(
            num_scalar_prefetch=2,
            grid=(bd,),
            in_specs=[pl.BlockSpec((None, rows, W), lambda b, *_: (b, 0, 0)),
                      pl.BlockSpec((None, rows, W), lambda b, *_: (b, 0, 0)),
                      pl.BlockSpec((None, rows, W), lambda b, *_: (b, 0, 0)),
                      pl.BlockSpec((1, W), lambda b, *_: (0, 0)),
                      any_spec, any_spec],
            out_specs=pl.BlockSpec((None, rows, W), lambda b, *_: (b, 0, 0)),
            scratch_shapes=[pltpu.VMEM((2, heads, tokens, HEAD_DIM, n_keys), F32),
                            pltpu.VMEM((2, heads, tokens, HEAD_DIM, n_keys), F32),
                            pltpu.SemaphoreType.DMA((2, 2))],
        ),
        compiler_params=_params("arbitrary"),
        name="moba_attend",
    )(top_flat, pt_flat, q8, kn8, vn8, nw, ckt4, cvt4)


FF_CHUNKS = 3
SIDE_SLOTS = FF_CHUNKS + 1
POOL_SLOTS = 3


def _outffn_compute(x_ref, ro_ref, mo_ref, g1_ref, sh2_ref, sc2_ref, g2_ref, n2w_ref, fw_ref,
                    wo_ref, wg_ref, wu_ref, wd_ref, y_ref, *, ff_chunks, moba_transposed, side_work):
    side_work(0)
    rw = ro_ref.shape[1]
    mo = mo_ref[...].astype(BF16)
    attn = (jnp.dot(ro_ref[...].astype(BF16), wo_ref[0:rw, :], preferred_element_type=F32)
            + lax.dot_general(mo, wo_ref[rw:, :], TN if moba_transposed else (((1,), (0,)), ((), ())),
                              preferred_element_type=F32))
    x1 = x_ref[...] + g1_ref[...] * attn
    h2 = (_rms(x1, n2w_ref[...]) * (1.0 + sc2_ref[...]) + sh2_ref[...]).astype(BF16)
    acc = None
    for c, (lo, hi) in enumerate(ff_chunks):
        side_work(1 + c)
        gate = jnp.dot(h2, wg_ref[:, lo:hi], preferred_element_type=F32)
        up = jnp.dot(h2, wu_ref[:, lo:hi], preferred_element_type=F32)
        part = jnp.dot((_silu(gate) * up).astype(BF16), wd_ref[lo:hi, :], preferred_element_type=F32)
        acc = part if acc is None else acc + part
    x2 = x1 + g2_ref[...] * acc
    y_ref[...] = _rms(x2, fw_ref[...])


def _outffn_kernel(*refs, ff_chunks, moba_transposed):
    _outffn_compute(*refs, ff_chunks=ff_chunks, moba_transposed=moba_transposed, side_work=lambda c: None)


def _outffn_pool_kernel(pt_ref, *refs, ff_chunks, moba_transposed, inner_steps, pages_per_step):
    ck_ref, y_ref, km_ref, pbuf, sem = refs[-5:]
    step = pl.program_id(0) * inner_steps + pl.program_id(1)
    n_steps = pl.num_programs(0) * inner_steps
    n_pages = pt_ref.shape[1]
    steps_per_seq = n_pages // pages_per_step
    chunk_pages = pages_per_step // SIDE_SLOTS
    ppb = MOBA_BLOCK // pbuf.shape[3]
    chunk_blocks = chunk_pages // ppb
    lane = lax.broadcasted_iota(jnp.int32, km_ref.shape, 1)

    def chunk_slot(st, c):
        return lax.rem(st * SIDE_SLOTS + c, POOL_SLOTS)

    def chunk_copies(st, c):
        seq = st // steps_per_seq
        base = (st % steps_per_seq) * pages_per_step + c * chunk_pages
        slot = chunk_slot(st, c)
        return [pltpu.make_async_copy(ck_ref.at[pt_ref[seq, base + p]], pbuf.at[slot, p], sem.at[slot])
                for p in range(chunk_pages)]

    def side_work(c):
        if c == 0:
            @pl.when(step == 0)
            def _():
                for a in range(POOL_SLOTS - 1):
                    for cp in chunk_copies(step, a):
                        cp.start()

            @pl.when(step % steps_per_seq == 0)
            def _():
                km_ref[...] = jnp.zeros_like(km_ref)
        ahead = c + POOL_SLOTS - 1
        if ahead < SIDE_SLOTS:
            for cp in chunk_copies(step, ahead):
                cp.start()
        else:
            @pl.when(step + 1 < n_steps)
            def _():
                for cp in chunk_copies(step + 1, ahead - SIDE_SLOTS):
                    cp.start()
        for cp in chunk_copies(step, c):
            cp.wait()
        slot = chunk_slot(step, c)
        first_block = (step % steps_per_seq) * (pages_per_step // ppb) + c * chunk_blocks
        acc = km_ref[...]
        for n in range(chunk_blocks):
            tot = pbuf[slot, n * ppb]
            for p in range(1, ppb):
                tot = tot + pbuf[slot, n * ppb + p]
            col = jnp.sum(tot, axis=1, keepdims=True) * (1.0 / MOBA_BLOCK)
            acc = jnp.where(lane == first_block + n, col, acc)
        km_ref[...] = acc

    _outffn_compute(*refs[:-5], y_ref, ff_chunks=ff_chunks, moba_transposed=moba_transposed, side_work=side_work)


def _outffn(x, ro, mo, g1, sh2, sc2, g2, n2w, fw, wo, wg, wu, wd, *, tm, moba_transposed, pool=None):
    G, R, D = x.shape
    W = ro.shape[2]
    dff = wg.shape[1]
    cuts = [round(dff * c / FF_CHUNKS / 256) * 256 for c in range(FF_CHUNKS)] + [dff]
    ff_chunks = tuple(zip(cuts[:-1], cuts[1:]))
    inner = R // tm
    row = pl.BlockSpec((None, tm, D), lambda g, i, *_: (g, i, 0))
    act = pl.BlockSpec((None, tm, W), lambda g, i, *_: (g, i, 0))
    act_t = pl.BlockSpec((None, W, tm), lambda g, i, *_: (g, 0, i))
    in_specs = [row, act, act_t if moba_transposed else act,
                _mod_spec(g1, tm), _mod_spec(sh2, tm), _mod_spec(sc2, tm), _mod_spec(g2, tm),
                _const_spec(n2w), _const_spec(fw), _const_spec(wo), _const_spec(wg), _const_spec(wu),
                _const_spec(wd)]
    args = (x, ro, mo, g1, sh2, sc2, g2, n2w, fw, wo, wg, wu, wd)
    y_shape = jax.ShapeDtypeStruct((G, R, D), F32)
    if pool is None:
        return pl.pallas_call(
            functools.partial(_outffn_kernel, ff_chunks=ff_chunks, moba_transposed=moba_transposed),
            out_shape=y_shape, grid=(G, inner), in_specs=in_specs, out_specs=row,
            compiler_params=_params("parallel", "parallel"),
            name="outproj_ffn",
        )(*args)
    page_table, ckt = pool
    bd, n_pages = page_table.shape
    _, hd, page = ckt.shape
    ppb = MOBA_BLOCK // page
    n_steps = G * inner
    pages_per_step = bd * n_pages // n_steps
    assert pages_per_step * n_steps == bd * n_pages and n_pages % pages_per_step == 0
    assert pages_per_step % (SIDE_SLOTS * ppb) == 0 and POOL_SLOTS - 1 <= SIDE_SLOTS
    steps_per_seq = n_pages // pages_per_step
    n_blocks = n_pages // ppb
    km_spec = pl.BlockSpec((None, hd, n_blocks), lambda g, i, *_: ((g * inner + i) // steps_per_seq, 0, 0))
    return pl.pallas_call(
        functools.partial(_outffn_pool_kernel, ff_chunks=ff_chunks, moba_transposed=moba_transposed,
                          inner_steps=inner, pages_per_step=pages_per_step),
        out_shape=[y_shape, jax.ShapeDtypeStruct((bd, hd, n_blocks), F32)],
        grid_spec=pltpu.PrefetchScalarGridSpec(
            num_scalar_prefetch=1,
            grid=(G, inner),
            in_specs=in_specs + [pl.BlockSpec(memory_space=pl.ANY)],
            out_specs=[row, km_spec],
            scratch_shapes=[pltpu.VMEM((POOL_SLOTS, pages_per_step // SIDE_SLOTS, hd, page), F32),
                            pltpu.SemaphoreType.DMA((POOL_SLOTS,))],
        ),
        compiler_params=_params("arbitrary", "arbitrary"),
        name="outproj_ffn_pool",
    )(page_table, *args, ckt)


def _rope_angles(pos):
    inv = ROPE_BASE ** (-jnp.arange(HALF, dtype=F32) / HALF)
    ang = pos.astype(F32)[:, None] * inv[None, :]
    return jnp.cos(ang), jnp.sin(ang)


def _rope_tables(pos):
    cos, sin = _rope_angles(pos)
    reps = LANES // HEAD_DIM
    return jnp.tile(cos, (1, 2 * reps)), jnp.tile(jnp.concatenate([-sin, sin], axis=1), (1, reps))


def kernel(x_prompt, x_sample, cache_k, cache_v, state_ret, page_table, c_prompt, c_sample,
           norm1_w, norm2_w, final_w, w_mod, b_mod, w_in, ret_gn_w, moba_norm_w, w_out,
           w_gate, w_up, w_down):
    Bp, S, D = x_prompt.shape
    Bd, T, _ = x_sample.shape
    depth = w_in.shape[0]
    assert depth == 1, "single decoder layer"
    n_pool, page, m_heads = cache_k.shape[1], cache_k.shape[2], cache_k.shape[3]
    n_pages = page_table.shape[1]
    past_len = n_pages * page
    assert past_len % MOBA_BLOCK == 0 and MOBA_BLOCK % page == 0
    n_full = past_len // MOBA_BLOCK
    moba_w = m_heads * HEAD_DIM
    ret_w = (w_in.shape[2] - 3 * moba_w) // 4
    assert ret_w == moba_w, "the two head groups share one projection width"
    W = ret_w
    r_heads = ret_w // HEAD_DIM
    fw = final_w.reshape(1, D)
    rows_s = Bd * T
    pad8 = SUBLANES

    w_in0 = w_in[0].astype(BF16)
    col = lambda g: w_in0[:, g * W:(g + 1) * W]
    w_tok = jnp.concatenate([col(0), col(2), col(3)], axis=1)
    w_trn = jnp.concatenate([col(1), col(4), col(5), col(6)], axis=1).T
    wo, wg, wu, wd = (w[0].astype(BF16) for w in (w_out, w_gate, w_up, w_down))
    n1w, n2w = norm1_w[0].reshape(1, D), norm2_w[0].reshape(1, D)
    gnw, mnw = ret_gn_w[0].reshape(1, ret_w), moba_norm_w[0].reshape(1, moba_w)

    mod = _modulation(jnp.concatenate([c_prompt, c_sample], axis=0), w_mod[0], b_mod[0])
    mod_p = [m[:, None, :] for m in jnp.split(mod[:Bp], 6, axis=-1)]
    mod_s = [jnp.repeat(m, T, axis=0)[None] for m in jnp.split(mod[Bp:], 6, axis=-1)]

    cos_a, sin_a = _rope_angles(jnp.arange(S))
    cos_p, sin_p = _rope_tables(jnp.arange(S))
    rq, rkt, rv, rg, mqt, mk, mkt, mvt = _inproj_prompt(x_prompt, mod_p[0], mod_p[1], n1w, cos_p, sin_p,
                                                        cos_a.T, sin_a.T, w_tok, w_trn, tm=1024)
    s0_p = jnp.zeros((Bp, r_heads, HEAD_DIM, HEAD_DIM), F32)
    ret_o, ret_p = _retention(rq, rkt, rv, rg, s0_p, _decay_tables(r_heads, RET_CHUNK, RET_CHUNK, RET_CHUNK), gnw,
                              tl=min(S, 2048), out_dtype=BF16)
    nwt = jnp.broadcast_to(mnw.reshape(moba_w, 1), (moba_w, MOBA_BLOCK))
    moba_ot = _moba_prompt(mqt, mk, mkt, mvt, nwt, out_dtype=BF16)
    ckt4 = jnp.transpose(cache_k[0], (0, 2, 3, 1))
    cvt4 = jnp.transpose(cache_v[0], (0, 2, 3, 1))
    y_prompt, kmt = _outffn(x_prompt, ret_o, moba_ot, mod_p[2], mod_p[3], mod_p[4], mod_p[5], n2w, fw,
                            wo, wg, wu, wd, tm=512, moba_transposed=True,
                            pool=(page_table, ckt4.reshape(n_pool, moba_w, page)))
    to_rows = lambda t: t.reshape(1, Bp, m_heads, HEAD_DIM, S).transpose(0, 1, 4, 2, 3)
    k_prompt, v_prompt = to_rows(mkt), to_rows(mvt)

    xs = x_sample.reshape(1, rows_s, D)
    cos_s, sin_s = _rope_tables(past_len + jnp.arange(rows_s) % T)
    sq, sk, sv, sg, smq, smk, smv = _inproj_sample(xs, mod_s[0], mod_s[1], n1w, cos_s, sin_s, w_in0)
    rows_pad = 4 * SUBLANES
    padc = lambda t: jnp.pad(t.reshape(Bd, T, ret_w), ((0, 0), (0, rows_pad - T), (0, 0)))
    dmat2_s, qdec_s, kdect_s, cdec_s = _decay_tables(r_heads, T, rows_pad, LANES)
    ret_os, ret_s = _retention_step(padc(sq), padc(sk), padc(sv), padc(sg), state_ret[0],
                                    (dmat2_s, qdec_s, kdect_s[:, :rows_pad].T, cdec_s), gnw)
    ret_os = ret_os[:, :T].reshape(1, rows_s, ret_w)

    assert kmt.shape[2] == n_full
    pad_rows = lambda t: jnp.pad(t.reshape(Bd, T, moba_w), ((0, 0), (0, pad8 - T), (0, 0)))
    q8 = pad_rows(smq)
    top = _topk(q8, kmt)[:, :T, :m_heads * MOBA_TOPK]
    moba_os = _attend(top.reshape(-1), page_table.reshape(-1), q8, pad_rows(smk), pad_rows(smv), mnw, ckt4, cvt4,
                      tokens=T, n_pages=n_pages)
    moba_os = moba_os[:, :T].reshape(1, rows_s, moba_w)
    y_sample = _outffn(xs, ret_os, moba_os, mod_s[2], mod_s[3], mod_s[4], mod_s[5], n2w, fw,
                       wo, wg, wu, wd, tm=rows_s, moba_transposed=False).reshape(Bd, T, D)
    k_sample = smk.reshape(1, Bd, T, m_heads, HEAD_DIM)
    v_sample = smv.reshape(1, Bd, T, m_heads, HEAD_DIM)

    return (y_prompt, y_sample, k_prompt, v_prompt, ret_p[None], k_sample, v_sample, ret_s[None])
```
